```python
import math
import jax, jax.numpy as jnp
from jax import lax
import numpy as np

D_MODEL = 2048
BATCH = 2
SEQ = 16384
DEPTH = 4
DEC_BATCH = 8
DEC_SEQ = 2048
PAST_LEN = 128

GRID_W = 64
MLA_HEADS = 8
MLA_Q_RANK = 512
MLA_KV_RANK = 256
MLA_NOPE = 128
MLA_ROPE = 64
MLA_V = 128
ROPE_THETA = 10000.0
NA_HEADS = 4
NA_HEAD_DIM = 128
NA_WIN_ROWS = 8
NA_WIN_COLS = 16
DIFF_HEADS = 4
DIFF_QK_DIM = 64
DIFF_V_DIM = 2 * DIFF_QK_DIM
IN_SPLITS = (MLA_Q_RANK, MLA_KV_RANK, MLA_ROPE,
             NA_HEADS * NA_HEAD_DIM, NA_HEADS * NA_HEAD_DIM, NA_HEADS * NA_HEAD_DIM,
             DIFF_HEADS * 2 * DIFF_QK_DIM, DIFF_HEADS * 2 * DIFF_QK_DIM, DIFF_HEADS * DIFF_V_DIM)
IN_WIDTH = sum(IN_SPLITS)
MIX_WIDTH = MLA_HEADS * MLA_V + NA_HEADS * NA_HEAD_DIM + DIFF_HEADS * DIFF_V_DIM
D_FF = 5632
CONV_W = 3
PLE_DIM = 256
QUERY_BLOCK = 128
RMS_EPS = 1e-6

kernel_name = 'hybrid_mla_natten_diff_encoder'


def _rmsnorm(x, g):
    xf = x.astype(jnp.float32)
    y = xf * lax.rsqrt(jnp.mean(xf * xf, axis=-1, keepdims=True) + RMS_EPS)
    return (y * g.astype(jnp.float32)).astype(x.dtype)


def _rope(x, pos):
    half = x.shape[-1] // 2
    freqs = jnp.power(ROPE_THETA, -jnp.arange(half, dtype=jnp.float32) / half)
    ang = pos.astype(jnp.float32)[:, None] * freqs[None, :]
    bshape = (1, pos.shape[0]) + (1,) * (x.ndim - 3) + (half,)
    cos = jnp.cos(ang).reshape(bshape)
    sin = jnp.sin(ang).reshape(bshape)
    xf = x.astype(jnp.float32)
    x1, x2 = xf[..., :half], xf[..., half:]
    return jnp.concatenate([x1 * cos - x2 * sin, x1 * sin + x2 * cos], axis=-1).astype(x.dtype)


def _to_blocks(x):
    b, l = x.shape[:2]
    return x.reshape((b, l // QUERY_BLOCK, QUERY_BLOCK) + x.shape[2:]).swapaxes(0, 1)


def _from_blocks(y):
    nb, b, qb = y.shape[:3]
    return y.swapaxes(0, 1).reshape((b, nb * qb) + y.shape[3:])


def _mla_attention(q, k, v):
    scale = (MLA_NOPE + MLA_ROPE) ** -0.5

    def block(qb):
        s = jnp.einsum('bqhd,bkhd->bhqk', qb, k).astype(jnp.float32) * scale
        p = jax.nn.softmax(s, axis=-1).astype(v.dtype)
        return jnp.einsum('bhqk,bkhd->bqhd', p, v)

    return _from_blocks(lax.map(block, _to_blocks(q)))


def _diff_attention(q1, q2, k1, k2, v, lam):
    seq_len = k1.shape[1]
    scale = DIFF_QK_DIM ** -0.5
    slopes = jnp.exp2(-8.0 * jnp.arange(1, DIFF_HEADS + 1, dtype=jnp.float32) / DIFF_HEADS)
    key_pos = jnp.arange(seq_len, dtype=jnp.int32)

    def block(args):
        q1b, q2b, start = args
        qpos = start + jnp.arange(QUERY_BLOCK, dtype=jnp.int32)
        dist = jnp.abs(qpos[:, None] - key_pos[None, :]).astype(jnp.float32)
        bias = -slopes[:, None, None] * dist[None]
        s1 = jnp.einsum('bqhd,bkhd->bhqk', q1b, k1).astype(jnp.float32) * scale + bias
        s2 = jnp.einsum('bqhd,bkhd->bhqk', q2b, k2).astype(jnp.float32) * scale + bias
        a = jax.nn.softmax(s1, axis=-1) - lam * jax.nn.softmax(s2, axis=-1)
        return jnp.einsum('bhqk,bkhd->bqhd', a.astype(v.dtype), v)

    starts = jnp.arange(seq_len // QUERY_BLOCK, dtype=jnp.int32) * QUERY_BLOCK
    return _from_blocks(lax.map(block, (_to_blocks(q1), _to_blocks(q2), starts)))


def _neighbourhood_attention(q, k, v, rpb):
    b, seq_len, nh, dh = q.shape
    rows = seq_len // GRID_W
    wr = min(NA_WIN_ROWS, rows)
    wc = NA_WIN_COLS
    nk = wr * wc
    scale = dh ** -0.5
    col = jnp.arange(GRID_W, dtype=jnp.int32)
    col_start = jnp.clip(col - wc // 2, 0, GRID_W - wc)
    key_col = col_start[:, None] + jnp.arange(wc, dtype=jnp.int32)[None, :]
    col_off = key_col - col[:, None] + (NA_WIN_COLS - 1)
    drow = jnp.arange(wr, dtype=jnp.int32)

    def row_block(args):
        q_row, r = args
        row_start = jnp.clip(r - wr // 2, 0, rows - wr)
        key_row = row_start + drow
        idx = (key_row[None, :, None] * GRID_W + key_col[:, None, :]).reshape(GRID_W, nk)
        k_sel = k[:, idx]
        v_sel = v[:, idx]
        row_off = key_row - r + (NA_WIN_ROWS - 1)
        bias = rpb[:, row_off[None, :, None], col_off[:, None, :]].reshape(nh, GRID_W, nk)
        s = jnp.einsum('bqhd,bqkhd->bhqk', q_row, k_sel).astype(jnp.float32) * scale
        s = s + bias.astype(jnp.float32)[None]
        p = jax.nn.softmax(s, axis=-1).astype(v.dtype)
        return jnp.einsum('bhqk,bqkhd->bqhd', p, v_sel)

    q_rows = q.reshape(b, rows, GRID_W, nh, dh).swapaxes(0, 1)
    out = lax.map(row_block, (q_rows, jnp.arange(rows, dtype=jnp.int32)))
    return out.swapaxes(0, 1).reshape(b, seq_len, nh * dh)


def _token_mixers(hn, pos, layer, w_in, g_qa, w_qb, g_kva, w_kvb, rpb,
                  lam_q1, lam_k1, lam_q2, lam_k2, g_subln):
    b, seq_len, _ = hn.shape
    z = hn @ w_in
    offsets = [int(o) for o in np.cumsum(IN_SPLITS)[:-1]]
    c_q, c_kv, k_pe, na_q, na_k, na_v, d_q, d_k, d_v = jnp.split(z, offsets, axis=-1)

    q = (_rmsnorm(c_q, g_qa) @ w_qb).reshape(b, seq_len, MLA_HEADS, MLA_NOPE + MLA_ROPE)
    q = jnp.concatenate([q[..., :MLA_NOPE], _rope(q[..., MLA_NOPE:], pos)], axis=-1)
    kv = (_rmsnorm(c_kv, g_kva) @ w_kvb).reshape(b, seq_len, MLA_HEADS, MLA_NOPE + MLA_V)
    k_pe = _rope(k_pe, pos)
    k = jnp.concatenate([kv[..., :MLA_NOPE],
                         jnp.broadcast_to(k_pe[:, :, None, :], (b, seq_len, MLA_HEADS, MLA_ROPE))], axis=-1)
    o_mla = _mla_attention(q, k, kv[..., MLA_NOPE:]).reshape(b, seq_len, MLA_HEADS * MLA_V)

    na_shape = (b, seq_len, NA_HEADS, NA_HEAD_DIM)
    o_na = _neighbourhood_attention(na_q.reshape(na_shape), na_k.reshape(na_shape),
                                    na_v.reshape(na_shape), rpb)

    lam_init = 0.8 - 0.6 * math.exp(-0.3 * layer)
    f32 = jnp.float32
    lam = (jnp.exp(jnp.sum(lam_q1.astype(f32) * lam_k1.astype(f32)))
           - jnp.exp(jnp.sum(lam_q2.astype(f32) * lam_k2.astype(f32))) + lam_init)
    dq = d_q.reshape(b, seq_len, DIFF_HEADS, 2, DIFF_QK_DIM)
    dk = d_k.reshape(b, seq_len, DIFF_HEADS, 2, DIFF_QK_DIM)
    o_diff = _diff_attention(dq[..., 0, :], dq[..., 1, :], dk[..., 0, :], dk[..., 1, :],
                             d_v.reshape(b, seq_len, DIFF_HEADS, DIFF_V_DIM), lam)
    o_diff = (_rmsnorm(o_diff, g_subln) * (1.0 - lam_init)).reshape(b, seq_len, DIFF_HEADS * DIFF_V_DIM)

    return jnp.concatenate([o_mla, o_na, o_diff], axis=-1)


def _dwconv(g, w, bias):
    seq_len = g.shape[1]
    gp = jnp.pad(g, ((0, 0), (1, 1), (0, 0)))
    return gp[:, :seq_len] * w[0] + gp[:, 1:seq_len + 1] * w[1] + gp[:, 2:] * w[2] + bias


def _trunk(h, p, g_attn, w_in, g_qa, w_qb, g_kva, w_kvb, rpb, lam_q1, lam_k1, lam_q2, lam_k2,
           g_subln, w_out, g_ffn, w_gu, conv_w, conv_b, w_down, g_ple, w_plg, w_ple, g_final):
    pos = jnp.arange(h.shape[1], dtype=jnp.int32)
    for i in range(DEPTH):
        mix = _token_mixers(_rmsnorm(h, g_attn[i]), pos, i, w_in[i], g_qa[i], w_qb[i], g_kva[i],
                            w_kvb[i], rpb[i], lam_q1[i], lam_k1[i], lam_q2[i], lam_k2[i], g_subln[i])
        h = h + mix @ w_out[i]
        gate_branch, up_branch = jnp.split(_rmsnorm(h, g_ffn[i]) @ w_gu[i], 2, axis=-1)
        h = h + (jax.nn.gelu(_dwconv(gate_branch, conv_w[i], conv_b[i])) * up_branch) @ w_down[i]
        ple_gate = jax.nn.sigmoid(_rmsnorm(h, g_ple[i]) @ w_plg[i])
        h = h + (p[i] @ w_ple[i]) * ple_gate
    return _rmsnorm(h, g_final)


def setup_inputs(seed: int = 0) -> dict:
    key = jax.random.key(seed)
    ks = jax.random.split(key, 26)
    f32 = jnp.float32

    def nrm(k, shape, scale):
        return jax.random.normal(k, shape, f32) * scale

    def gain(k, shape):
        return 1.0 + 0.02 * jax.random.normal(k, shape, f32)

    return {
        'x_prompt': nrm(ks[0], (BATCH, SEQ, D_MODEL), 1.0),
        'x_sample': nrm(ks[1], (DEC_BATCH, DEC_SEQ, D_MODEL), 1.0),
        'p_prompt': nrm(ks[2], (DEPTH, BATCH, SEQ, PLE_DIM), 1.0),
        'p_sample': nrm(ks[3], (DEPTH, DEC_BATCH, DEC_SEQ, PLE_DIM), 1.0),
        'g_attn': gain(ks[4], (DEPTH, D_MODEL)),
        'w_in': nrm(ks[5], (DEPTH, D_MODEL, IN_WIDTH), D_MODEL ** -0.5),
        'g_qa': gain(ks[6], (DEPTH, MLA_Q_RANK)),
        'w_qb': nrm(ks[7], (DEPTH, MLA_Q_RANK, MLA_HEADS * (MLA_NOPE + MLA_ROPE)), MLA_Q_RANK ** -0.5),
        'g_kva': gain(ks[8], (DEPTH, MLA_KV_RANK)),
        'w_kvb': nrm(ks[9], (DEPTH, MLA_KV_RANK, MLA_HEADS * (MLA_NOPE + MLA_V)), MLA_KV_RANK ** -0.5),
        'rpb': nrm(ks[10], (DEPTH, NA_HEADS, 2 * NA_WIN_ROWS - 1, 2 * NA_WIN_COLS - 1), 0.1),
        'lam_q1': nrm(ks[11], (DEPTH, DIFF_QK_DIM), 0.1),
        'lam_k1': nrm(ks[12], (DEPTH, DIFF_QK_DIM), 0.1),
        'lam_q2': nrm(ks[13], (DEPTH, DIFF_QK_DIM), 0.1),
        'lam_k2': nrm(ks[14], (DEPTH, DIFF_QK_DIM), 0.1),
        'g_subln': gain(ks[15], (DEPTH, DIFF_V_DIM)),
        'w_out': nrm(ks[16], (DEPTH, MIX_WIDTH, D_MODEL), MIX_WIDTH ** -0.5),
        'g_ffn': gain(ks[17], (DEPTH, D_MODEL)),
        'w_gu': nrm(ks[18], (DEPTH, D_MODEL, 2 * D_FF), D_MODEL ** -0.5),
        'conv_w': nrm(ks[19], (DEPTH, CONV_W, D_FF), CONV_W ** -0.5),
        'conv_b': nrm(ks[20], (DEPTH, D_FF), 0.01),
        'w_down': nrm(ks[21], (DEPTH, D_FF, D_MODEL), D_FF ** -0.5),
        'g_ple': gain(ks[22], (DEPTH, D_MODEL)),
        'w_plg': nrm(ks[23], (DEPTH, D_MODEL, D_MODEL), D_MODEL ** -0.5),
        'w_ple': nrm(ks[24], (DEPTH, PLE_DIM, D_MODEL), PLE_DIM ** -0.5),
        'g_final': gain(ks[25], (D_MODEL,)),
    }


def reference(x_prompt, x_sample, p_prompt, p_sample, g_attn, w_in, g_qa, w_qb, g_kva, w_kvb, rpb,
              lam_q1, lam_k1, lam_q2, lam_k2, g_subln, w_out, g_ffn, w_gu, conv_w, conv_b, w_down,
              g_ple, w_plg, w_ple, g_final):
    weights = (g_attn, w_in, g_qa, w_qb, g_kva, w_kvb, rpb, lam_q1, lam_k1, lam_q2, lam_k2,
               g_subln, w_out, g_ffn, w_gu, conv_w, conv_b, w_down, g_ple, w_plg, w_ple, g_final)
    y_prompt = _trunk(x_prompt, p_prompt, *weights)
    y_sample = _trunk(x_sample, p_sample, *weights)
    return (y_prompt, y_sample)
```

```python
import functools
import math

import jax
import jax.numpy as jnp
import numpy as np
from jax import lax
from jax.experimental import pallas as pl
from jax.experimental.pallas import tpu as pltpu

F32 = jnp.float32
BF16 = jnp.bfloat16

GRID_W = 64
MLA_HEADS = 8
MLA_NOPE = 128
MLA_ROPE = 64
MLA_V = 128
MLA_QK_PAD = 256
ROPE_THETA = 10000.0
NA_HEADS = 4
NA_HEAD_DIM = 128
NA_WIN_ROWS = 8
NA_WIN_COLS = 16
NA_ROWS_PER_STEP = 8
DIFF_HEADS = 4
DIFF_QK_DIM = 64
DIFF_V_DIM = 128
RMS_EPS = 1e-6
LOG2E = 1.4426950408889634
MASK_VALUE = -1e30
LANES = 128
BF16_ROWS = 16
VMEM_LIMIT = 56 * 1024 * 1024


def _tile(n, pref):
    t = min(n, pref)
    assert n % t == 0, (n, pref)
    return t


def _rms(x, g):
    ms = jnp.mean(x * x, axis=-1, keepdims=True)
    return x * lax.rsqrt(ms + RMS_EPS) * g


def _params(*sem):
    return pltpu.CompilerParams(dimension_semantics=sem, vmem_limit_bytes=VMEM_LIMIT)


def _norm_matmul_kernel(x_ref, g_ref, w_ref, cs_ref, o_ref, xn_ref):
    @pl.when(pl.program_id(1) == 0)
    def _():
        xn_ref[...] = _rms(x_ref[...], g_ref[...]).astype(BF16)

    acc = jnp.dot(xn_ref[...], w_ref[...], preferred_element_type=F32)
    o_ref[...] = (acc * cs_ref[...]).astype(o_ref.dtype)


def _norm_matmul(x, g, w, col_scale, out_dtype, tm=1024, tn=512):
    t, d = x.shape
    n = w.shape[1]
    tm, tn = _tile(t, tm), _tile(n, tn)
    return pl.pallas_call(
        _norm_matmul_kernel,
        grid=(t // tm, n // tn),
        in_specs=[
            pl.BlockSpec((tm, d), lambda i, j: (i, 0)),
            pl.BlockSpec((1, d), lambda i, j: (0, 0)),
            pl.BlockSpec((d, tn), lambda i, j: (0, j)),
            pl.BlockSpec((1, tn), lambda i, j: (0, j)),
        ],
        out_specs=pl.BlockSpec((tm, tn), lambda i, j: (i, j)),
        out_shape=jax.ShapeDtypeStruct((t, n), out_dtype),
        scratch_shapes=[pltpu.VMEM((tm, d), BF16)],
        compiler_params=_params("parallel", "arbitrary"),
        name="norm_matmul",
    )(x, g, w, col_scale)


def _mla_proj_kernel(zf_ref, gq_ref, gkv_ref, wq_ref, wkv_ref, c_ref, s_ref, q_ref, k_ref, v_ref, *,
                     q_rank, kv_rank, q_scale):
    zf = zf_ref[...]
    cos, sin = c_ref[...], s_ref[...]
    qn = _rms(zf[:, :q_rank], gq_ref[...]).astype(BF16)
    qa = jnp.dot(qn, wq_ref[...], preferred_element_type=F32)
    kvn = _rms(zf[:, q_rank:q_rank + kv_rank], gkv_ref[...]).astype(BF16)
    kva = jnp.dot(kvn, wkv_ref[...], preferred_element_type=F32)
    o = q_rank + kv_rank
    kpe = (zf[:, o:o + LANES] * cos + zf[:, o + LANES:o + 2 * LANES] * sin).astype(BF16)
    hp = MLA_HEADS * LANES
    for h in range(MLA_HEADS):
        lo = h * LANES
        rope = qa[:, hp + lo:hp + lo + LANES] * cos + qa[:, 2 * hp + lo:2 * hp + lo + LANES] * sin
        q_ref[:, 2 * lo:2 * lo + LANES] = (qa[:, lo:lo + LANES] * q_scale).astype(BF16)
        q_ref[:, 2 * lo + LANES:2 * lo + 2 * LANES] = (rope * q_scale).astype(BF16)
        k_ref[:, 2 * lo:2 * lo + LANES] = kva[:, 2 * lo:2 * lo + LANES].astype(BF16)
        k_ref[:, 2 * lo + LANES:2 * lo + 2 * LANES] = kpe
        v_ref[:, lo:lo + LANES] = kva[:, 2 * lo + LANES:2 * lo + 2 * LANES].astype(BF16)


def _mla_proj(zf, gq, gkv, wq, wkv, rope_c, rope_s, seq, tm=512):
    t, zw = zf.shape
    tm = _tile(seq, tm)
    nl = seq // tm
    q_rank, kv_rank = gq.shape[1], gkv.shape[1]
    kern = functools.partial(_mla_proj_kernel, q_rank=q_rank, kv_rank=kv_rank,
                             q_scale=(MLA_NOPE + MLA_ROPE) ** -0.5 * LOG2E)
    qk_w = MLA_HEADS * MLA_QK_PAD
    return pl.pallas_call(
        kern,
        grid=(t // tm,),
        in_specs=[
            pl.BlockSpec((tm, zw), lambda i: (i, 0)),
            pl.BlockSpec((1, q_rank), lambda i: (0, 0)),
            pl.BlockSpec((1, kv_rank), lambda i: (0, 0)),
            pl.BlockSpec(wq.shape, lambda i: (0, 0)),
            pl.BlockSpec(wkv.shape, lambda i: (0, 0)),
            pl.BlockSpec((tm, LANES), lambda i: (i % nl, 0)),
            pl.BlockSpec((tm, LANES), lambda i: (i % nl, 0)),
        ],
        out_specs=[
            pl.BlockSpec((tm, qk_w), lambda i: (i, 0)),
            pl.BlockSpec((tm, qk_w), lambda i: (i, 0)),
            pl.BlockSpec((tm, MLA_HEADS * MLA_V), lambda i: (i, 0)),
        ],
        out_shape=[
            jax.ShapeDtypeStruct((t, qk_w), BF16),
            jax.ShapeDtypeStruct((t, qk_w), BF16),
            jax.ShapeDtypeStruct((t, MLA_HEADS * MLA_V), BF16),
        ],
        compiler_params=_params("parallel"),
        name="mla_proj",
    )(zf, gq, gkv, wq, wkv, rope_c, rope_s)


def _online_softmax_step(s, v, m_ref, l_ref, acc_ref, shift):
    m_prev = m_ref[...]
    m_cur = jnp.max(s, axis=1, keepdims=True)
    if shift is not None:
        m_cur = m_cur + shift
    m_next = jnp.maximum(m_prev, m_cur)
    alpha = jnp.exp2(m_prev - m_next)
    m_sub = m_next if shift is None else m_next - shift
    p = jnp.exp2(s - pltpu.repeat(m_sub, s.shape[1] // LANES, axis=1))
    l_ref[...] = alpha * l_ref[...] + jnp.sum(p, axis=1, keepdims=True)
    acc_ref[...] = alpha * acc_ref[...] + jnp.dot(p.astype(BF16), v, preferred_element_type=F32)
    m_ref[...] = m_next


def _init_softmax_state(m_ref, l_ref, acc_ref):
    m_ref[...] = jnp.full(m_ref.shape, MASK_VALUE, F32)
    l_ref[...] = jnp.zeros(l_ref.shape, F32)
    acc_ref[...] = jnp.zeros(acc_ref.shape, F32)


def _qk(q, k):
    return lax.dot_general(q, k, (((1,), (1,)), ((), ())), preferred_element_type=F32)


def _mla_flash_kernel(q_ref, k_ref, v_ref, o_ref, m_ref, l_ref, acc_ref, *, tkc):
    ki = pl.program_id(3)

    @pl.when(ki == 0)
    def _():
        _init_softmax_state(m_ref, l_ref, acc_ref)

    q = q_ref[...]

    def chunk(c, carry):
        off = pl.multiple_of(c * tkc, tkc)
        s = _qk(q, k_ref[pl.ds(off, tkc), :])
        _online_softmax_step(s, v_ref[pl.ds(off, tkc), :], m_ref, l_ref, acc_ref, None)
        return carry

    lax.fori_loop(0, k_ref.shape[0] // tkc, chunk, 0)

    @pl.when(ki == pl.num_programs(3) - 1)
    def _():
        o_ref[...] = (acc_ref[...] / l_ref[...]).astype(o_ref.dtype)


def _mla_flash(q, k, v, batch, seq, tq=1024, tkb=2048, tkc=512):
    t = q.shape[0]
    tq, tkb = _tile(seq, tq), _tile(seq, tkb)
    tkc = _tile(tkb, tkc)
    nq, nk = seq // tq, seq // tkb
    return pl.pallas_call(
        functools.partial(_mla_flash_kernel, tkc=tkc),
        grid=(MLA_HEADS, batch, nq, nk),
        in_specs=[
            pl.BlockSpec((tq, MLA_QK_PAD), lambda g, b, qi, ki: (b * nq + qi, g)),
            pl.BlockSpec((tkb, MLA_QK_PAD), lambda g, b, qi, ki: (b * nk + ki, g)),
            pl.BlockSpec((tkb, MLA_V), lambda g, b, qi, ki: (b * nk + ki, g)),
        ],
        out_specs=pl.BlockSpec((tq, MLA_V), lambda g, b, qi, ki: (b * nq + qi, g)),
        out_shape=jax.ShapeDtypeStruct((t, MLA_HEADS * MLA_V), BF16),
        scratch_shapes=[pltpu.VMEM((tq, LANES), F32), pltpu.VMEM((tq, LANES), F32),
                        pltpu.VMEM((tq, MLA_V), F32)],
        compiler_params=_params("parallel", "parallel", "parallel", "arbitrary"),
        name="mla_flash",
    )(q, k, v)


def _diff_flash_kernel(slope_ref, lq1_ref, lk1_ref, lq2_ref, lk2_ref, gsub_ref, q_ref, k_ref, v_ref, o_ref,
                       qs_ref, m_ref, l_ref, acc_ref, bias_ref, *, tq, lam_init):
    g, qi, ki = pl.program_id(0), pl.program_id(2), pl.program_id(3)
    slope = slope_ref[g]

    @pl.when(ki == 0)
    def _():
        _init_softmax_state(m_ref, l_ref, acc_ref)
        q = q_ref[...]
        lane = lax.broadcasted_iota(jnp.int32, q.shape, 1)
        qs_ref[:tq] = jnp.where(lane < DIFF_QK_DIM, q, jnp.zeros_like(q))
        qs_ref[tq:] = jnp.where(lane >= DIFF_QK_DIM, q, jnp.zeros_like(q))
        shape = (2 * tq, tq)
        row = lax.broadcasted_iota(jnp.int32, shape, 0)
        row = jnp.where(row >= tq, row - tq, row)
        d = (row - lax.broadcasted_iota(jnp.int32, shape, 1)).astype(F32)
        bias_ref[0] = -slope * d
        bias_ref[1] = -slope * jnp.abs(d)
        bias_ref[2] = slope * d

    qs = qs_ref[...]
    q0 = qi * tq

    def chunk(c, carry):
        off = pl.multiple_of(c * tq, tq)
        k0 = ki * k_ref.shape[0] + off
        which = jnp.where(k0 < q0, 0, jnp.where(k0 == q0, 1, 2))
        gap = jnp.full((1, LANES), jnp.abs(q0 - k0), jnp.int32).astype(F32)
        s = _qk(qs, k_ref[pl.ds(off, tq), :]) + bias_ref[which]
        _online_softmax_step(s, v_ref[pl.ds(off, tq), :], m_ref, l_ref, acc_ref, -slope * gap)
        return carry

    lax.fori_loop(0, k_ref.shape[0] // tq, chunk, 0)

    @pl.when(ki == pl.num_programs(3) - 1)
    def _():
        o = acc_ref[...] / l_ref[...]
        lam = (jnp.exp(jnp.sum(lq1_ref[...] * lk1_ref[...], axis=-1, keepdims=True))
               - jnp.exp(jnp.sum(lq2_ref[...] * lk2_ref[...], axis=-1, keepdims=True)) + lam_init)
        d = o[:tq] - lam * o[tq:]
        o_ref[...] = (_rms(d, gsub_ref[...]) * (1.0 - lam_init)).astype(o_ref.dtype)


def _diff_flash(zb, col0, slopes, lq1, lk1, lq2, lk2, gsub, lam_init, batch, seq, tq=512, tkb=2048):
    t = zb.shape[0]
    tq = _tile(seq, tq)
    tkb = _tile(seq, tkb)
    assert tkb % tq == 0
    nq, nk = seq // tq, seq // tkb
    qc, kc, vc = col0
    vec = lambda a: pl.BlockSpec(a.shape, lambda g, b, qi, ki: (0, 0))
    return pl.pallas_call(
        functools.partial(_diff_flash_kernel, tq=tq, lam_init=lam_init),
        grid=(DIFF_HEADS, batch, nq, nk),
        in_specs=[
            pl.BlockSpec(memory_space=pltpu.SMEM),
            vec(lq1), vec(lk1), vec(lq2), vec(lk2), vec(gsub),
            pl.BlockSpec((tq, LANES), lambda g, b, qi, ki: (b * nq + qi, qc + g)),
            pl.BlockSpec((tkb, LANES), lambda g, b, qi, ki: (b * nk + ki, kc + g)),
            pl.BlockSpec((tkb, DIFF_V_DIM), lambda g, b, qi, ki: (b * nk + ki, vc + g)),
        ],
        out_specs=pl.BlockSpec((tq, DIFF_V_DIM), lambda g, b, qi, ki: (b * nq + qi, g)),
        out_shape=jax.ShapeDtypeStruct((t, DIFF_HEADS * DIFF_V_DIM), BF16),
        scratch_shapes=[pltpu.VMEM((2 * tq, LANES), BF16), pltpu.VMEM((2 * tq, LANES), F32),
                        pltpu.VMEM((2 * tq, LANES), F32), pltpu.VMEM((2 * tq, DIFF_V_DIM), F32),
                        pltpu.VMEM((3, 2 * tq, tq), F32)],
        compiler_params=_params("parallel", "parallel", "parallel", "arbitrary"),
        name="diff_flash",
    )(slopes, lq1, lk1, lq2, lk2, gsub, zb, zb, zb)


def _na_kernel(q_ref, kp_ref, kc_ref, kn_ref, vp_ref, vc_ref, vn_ref, bias_ref, o_ref, kb_ref, vb_ref, *, rows):
    j = pl.program_id(2)
    blk = NA_ROWS_PER_STEP * GRID_W
    band = NA_WIN_ROWS * GRID_W
    for n, (kr, vr) in enumerate(((kp_ref, vp_ref), (kc_ref, vc_ref), (kn_ref, vn_ref))):
        kb_ref[n * blk:(n + 1) * blk] = kr[...]
        vb_ref[n * blk:(n + 1) * blk] = vr[...]
    for i in range(NA_ROWS_PER_STEP):
        r = j * NA_ROWS_PER_STEP + i
        row_start = jnp.clip(r - NA_WIN_ROWS // 2, 0, rows - NA_WIN_ROWS)
        start = pl.multiple_of((row_start - (j - 1) * NA_ROWS_PER_STEP) * GRID_W, GRID_W)
        q = q_ref[i * GRID_W:(i + 1) * GRID_W, :]
        s = _qk(q, kb_ref[pl.ds(start, band), :]) + bias_ref[r - row_start]
        p = jnp.exp2(s - jnp.max(s, axis=1, keepdims=True))
        o = jnp.dot(p.astype(BF16), vb_ref[pl.ds(start, band), :], preferred_element_type=F32)
        o_ref[i * GRID_W:(i + 1) * GRID_W, :] = (o / jnp.sum(p, axis=1, keepdims=True)).astype(o_ref.dtype)


def _na_attention(zb, col0, bias, batch, seq):
    t = zb.shape[0]
    rows = seq // GRID_W
    assert rows % NA_ROWS_PER_STEP == 0 and rows >= NA_WIN_ROWS
    nj = rows // NA_ROWS_PER_STEP
    blk = NA_ROWS_PER_STEP * GRID_W
    qc, kc, vc = col0

    def spec(c0, dj):
        return pl.BlockSpec((blk, NA_HEAD_DIM),
                            lambda h, b, j: (b * nj + jnp.clip(j + dj, 0, nj - 1), c0 + h))

    return pl.pallas_call(
        functools.partial(_na_kernel, rows=rows),
        grid=(NA_HEADS, batch, nj),
        in_specs=[spec(qc, 0), spec(kc, -1), spec(kc, 0), spec(kc, 1), spec(vc, -1), spec(vc, 0), spec(vc, 1),
                  pl.BlockSpec((None,) + bias.shape[1:], lambda h, b, j: (h, 0, 0, 0))],
        out_specs=pl.BlockSpec((blk, NA_HEAD_DIM), lambda h, b, j: (b * nj + j, h)),
        out_shape=jax.ShapeDtypeStruct((t, NA_HEADS * NA_HEAD_DIM), BF16),
        scratch_shapes=[pltpu.VMEM((3 * blk, NA_HEAD_DIM), BF16), pltpu.VMEM((3 * blk, NA_HEAD_DIM), BF16)],
        compiler_params=_params("parallel", "parallel", "parallel"),
        name="na_attention",
    )(zb, zb, zb, zb, zb, zb, zb, bias)


def _na_bias_table(rpb):
    col = np.arange(GRID_W)
    col_start = np.clip(col - NA_WIN_COLS // 2, 0, GRID_W - NA_WIN_COLS)
    kc = np.arange(GRID_W)
    valid = (kc[None, :] >= col_start[:, None]) & (kc[None, :] < col_start[:, None] + NA_WIN_COLS)
    col_off = np.clip(kc[None, :] - col[:, None] + NA_WIN_COLS - 1, 0, 2 * NA_WIN_COLS - 2)
    tt = np.arange(NA_WIN_ROWS)
    dd = np.arange(NA_WIN_ROWS)
    row_off = dd[None, :] - tt[:, None] + NA_WIN_ROWS - 1
    tbl = rpb[:, row_off[:, None, :, None], col_off[None, :, None, :]]
    tbl = jnp.where(valid[None, None, :, None, :], tbl * LOG2E, MASK_VALUE)
    return tbl.reshape(rpb.shape[0], NA_WIN_ROWS, GRID_W, NA_WIN_ROWS * GRID_W).astype(F32)


def _out_proj_kernel(a_ref, b_ref, c_ref, wa_ref, wb_ref, wc_ref, h_ref, o_ref):
    acc = jnp.dot(a_ref[...], wa_ref[...], preferred_element_type=F32)
    acc = acc + jnp.dot(b_ref[...], wb_ref[...], preferred_element_type=F32)
    acc = acc + jnp.dot(c_ref[...], wc_ref[...], preferred_element_type=F32)
    o_ref[...] = h_ref[...] + acc


def _out_proj(a, b, c, wa, wb, wc, h, tm=1024, tn=512):
    t, d = h.shape
    tm, tn = _tile(t, tm), _tile(d, tn)
    act = lambda x: pl.BlockSpec((tm, x.shape[1]), lambda i, j: (i, 0))
    wgt = lambda w: pl.BlockSpec((w.shape[0], tn), lambda i, j: (0, j))
    return pl.pallas_call(
        _out_proj_kernel,
        grid=(t // tm, d // tn),
        in_specs=[act(a), act(b), act(c), wgt(wa), wgt(wb), wgt(wc),
                  pl.BlockSpec((tm, tn), lambda i, j: (i, j))],
        out_specs=pl.BlockSpec((tm, tn), lambda i, j: (i, j)),
        out_shape=jax.ShapeDtypeStruct((t, d), F32),
        compiler_params=_params("parallel", "arbitrary"),
        name="out_proj",
    )(a, b, c, wa, wb, wc, h)


def _ffn_kernel(h_ref, hp_ref, hn_ref, g_ref, wg_ref, wu_ref, cw_ref, cb_ref, wd_ref, o_ref, xn_ref, *,
                tm, tiles_per_seq):
    i, j = pl.program_id(0), pl.program_id(1)

    @pl.when(j == 0)
    def _():
        x = h_ref[...]
        g = g_ref[...]
        xn_ref[:tm] = _rms(x, g).astype(BF16)
        pos = i % tiles_per_seq
        before = jnp.where(pos == 0, 0.0, _rms(hp_ref[...], g)[BF16_ROWS - 1:BF16_ROWS])
        after = jnp.where(pos == tiles_per_seq - 1, 0.0, _rms(hn_ref[...], g)[0:1])
        rid = lax.broadcasted_iota(jnp.int32, (BF16_ROWS, x.shape[1]), 0)
        halo = jnp.where(rid == 0, before, jnp.where(rid == 1, after, 0.0))
        xn_ref[tm:] = halo.astype(BF16)
        o_ref[...] = x

    ga = jnp.dot(xn_ref[...], wg_ref[...], preferred_element_type=F32)
    up = jnp.dot(xn_ref[:tm], wu_ref[...], preferred_element_type=F32)
    gate = ga[:tm]
    rid = lax.broadcasted_iota(jnp.int32, gate.shape, 0)
    g_prev = jnp.where(rid == 0, ga[tm:tm + 1], pltpu.roll(gate, 1, 0))
    g_next = jnp.where(rid == tm - 1, ga[tm + 1:tm + 2], pltpu.roll(gate, tm - 1, 0))
    cw = cw_ref[...]
    conv = g_prev * cw[0:1] + gate * cw[1:2] + g_next * cw[2:3] + cb_ref[...]
    act = (jax.nn.gelu(conv) * up).astype(BF16)
    o_ref[...] += jnp.dot(act, wd_ref[...], preferred_element_type=F32)


def _ffn(h, g, w_gu, conv_w, conv_b, w_down, seq, tm=512, tf=512):
    t, d = h.shape
    f = w_down.shape[0]
    tm, tf = _tile(seq, tm), _tile(f, tf)
    nf = f // tf
    hb = tm // BF16_ROWS
    nhb = t // BF16_ROWS
    return pl.pallas_call(
        functools.partial(_ffn_kernel, tm=tm, tiles_per_seq=seq // tm),
        grid=(t // tm, nf),
        in_specs=[
            pl.BlockSpec((tm, d), lambda i, j: (i, 0)),
            pl.BlockSpec((BF16_ROWS, d), lambda i, j: (jnp.maximum(i * hb - 1, 0), 0)),
            pl.BlockSpec((BF16_ROWS, d), lambda i, j: (jnp.minimum((i + 1) * hb, nhb - 1), 0)),
            pl.BlockSpec((1, d), lambda i, j: (0, 0)),
            pl.BlockSpec((d, tf), lambda i, j: (0, j)),
            pl.BlockSpec((d, tf), lambda i, j: (0, nf + j)),
            pl.BlockSpec((conv_w.shape[0], tf), lambda i, j: (0, j)),
            pl.BlockSpec((1, tf), lambda i, j: (0, j)),
            pl.BlockSpec((tf, d), lambda i, j: (j, 0)),
        ],
        out_specs=pl.BlockSpec((tm, d), lambda i, j: (i, 0)),
        out_shape=jax.ShapeDtypeStruct((t, d), F32),
        scratch_shapes=[pltpu.VMEM((tm + BF16_ROWS, d), BF16)],
        compiler_params=_params("parallel", "arbitrary"),
        name="ffn",
    )(h, h, h, g, w_gu, w_gu, conv_w, conv_b, w_down)


def _ple_kernel(hrow_ref, h_ref, p_ref, g_ref, wg_ref, wp_ref, o_ref, xn_ref):
    @pl.when(pl.program_id(1) == 0)
    def _():
        xn_ref[...] = _rms(hrow_ref[...], g_ref[...]).astype(BF16)

    gate = jax.nn.sigmoid(jnp.dot(xn_ref[...], wg_ref[...], preferred_element_type=F32))
    emb = jnp.dot(p_ref[...].astype(BF16), wp_ref[...], preferred_element_type=F32)
    o_ref[...] = h_ref[...] + emb * gate


def _ple(h, p, g, w_plg, w_ple, tm=1024, tn=512):
    t, d = h.shape
    tm, tn = _tile(t, tm), _tile(d, tn)
    return pl.pallas_call(
        _ple_kernel,
        grid=(t // tm, d // tn),
        in_specs=[
            pl.BlockSpec((tm, d), lambda i, j: (i, 0)),
            pl.BlockSpec((tm, tn), lambda i, j: (i, j)),
            pl.BlockSpec((tm, p.shape[1]), lambda i, j: (i, 0)),
            pl.BlockSpec((1, d), lambda i, j: (0, 0)),
            pl.BlockSpec((d, tn), lambda i, j: (0, j)),
            pl.BlockSpec((p.shape[1], tn), lambda i, j: (0, j)),
        ],
        out_specs=pl.BlockSpec((tm, tn), lambda i, j: (i, j)),
        out_shape=jax.ShapeDtypeStruct((t, d), F32),
        scratch_shapes=[pltpu.VMEM((tm, d), BF16)],
        compiler_params=_params("parallel", "arbitrary"),
        name="ple",
    )(h, h, p, g, w_plg, w_ple)


def _final_norm_kernel(x_ref, g_ref, o_ref):
    o_ref[...] = _rms(x_ref[...], g_ref[...])


def _final_norm(h, g, tm=1024):
    t, d = h.shape
    tm = _tile(t, tm)
    return pl.pallas_call(
        _final_norm_kernel,
        grid=(t // tm,),
        in_specs=[pl.BlockSpec((tm, d), lambda i: (i, 0)), pl.BlockSpec((1, d), lambda i: (0, 0))],
        out_specs=pl.BlockSpec((tm, d), lambda i: (i, 0)),
        out_shape=jax.ShapeDtypeStruct((t, d), F32),
        compiler_params=_params("parallel"),
        name="final_norm",
    )(h, g)


def _swap_halves(x):
    half = x.shape[-1] // 2
    return jnp.concatenate([x[..., half:], x[..., :half]], axis=-1)


def _pad_lanes(x):
    return jnp.concatenate([x, jnp.zeros(x.shape[:-1] + (LANES - x.shape[-1],), x.dtype)], axis=-1)


def _prepare_layer(i, w_in, g_qa, w_qb, g_kva, w_kvb, rpb, w_out, w_gu, w_down, w_plg, w_ple):
    d = w_in.shape[0]
    q_rank, kv_rank = g_qa.shape[0], g_kva.shape[0]
    o = q_rank + kv_rank
    kpe = w_in[:, o:o + MLA_ROPE]
    w_f = jnp.concatenate([w_in[:, :o], _pad_lanes(kpe), _pad_lanes(_swap_halves(kpe))], axis=1)
    w_b = w_in[:, o + MLA_ROPE:]
    na_w = NA_HEADS * NA_HEAD_DIM
    diff_w = DIFF_HEADS * 2 * DIFF_QK_DIM
    scale_b = np.ones((1, w_b.shape[1]), np.float32)
    scale_b[:, :na_w] = NA_HEAD_DIM ** -0.5 * LOG2E
    scale_b[:, 3 * na_w:3 * na_w + diff_w] = DIFF_QK_DIM ** -0.5 * LOG2E
    wq = w_qb.reshape(q_rank, MLA_HEADS, MLA_NOPE + MLA_ROPE)
    pe = wq[..., MLA_NOPE:]
    wq = jnp.concatenate([wq[..., :MLA_NOPE].reshape(q_rank, -1), _pad_lanes(pe).reshape(q_rank, -1),
                          _pad_lanes(_swap_halves(pe)).reshape(q_rank, -1)], axis=1)
    mla_w = MLA_HEADS * MLA_V
    return dict(
        w_f=w_f.astype(BF16), scale_f=jnp.ones((1, w_f.shape[1]), F32),
        w_b=w_b.astype(BF16), scale_b=jnp.asarray(scale_b),
        wq=wq.astype(BF16), wkv=w_kvb.astype(BF16),
        na_bias=_na_bias_table(rpb),
        wo_mla=w_out[:mla_w].astype(BF16), wo_na=w_out[mla_w:mla_w + na_w].astype(BF16),
        wo_diff=w_out[mla_w + na_w:].astype(BF16),
        w_gu=w_gu.astype(BF16), w_down=w_down.astype(BF16),
        w_plg=w_plg.astype(BF16), w_ple=w_ple.astype(BF16),
        lam_init=0.8 - 0.6 * math.exp(-0.3 * i),
    )


def _rope_tables(seq):
    half = MLA_ROPE // 2
    freqs = jnp.power(ROPE_THETA, -jnp.arange(half, dtype=F32) / half)
    ang = jnp.arange(seq, dtype=jnp.int32).astype(F32)[:, None] * freqs[None, :]
    cos, sin = jnp.cos(ang), jnp.sin(ang)
    return _pad_lanes(jnp.concatenate([cos, cos], axis=1)), _pad_lanes(jnp.concatenate([-sin, sin], axis=1))


def _trunk(x, p, layers, vecs, g_final):
    batch, seq, d = x.shape
    t = batch * seq
    h = x.reshape(t, d)
    rope_c, rope_s = _rope_tables(seq)
    slopes = jnp.exp2(-8.0 * jnp.arange(1, DIFF_HEADS + 1, dtype=F32) / DIFF_HEADS) * LOG2E
    nb = NA_HEADS * NA_HEAD_DIM // LANES
    for i, (lw, lv) in enumerate(zip(layers, vecs)):
        zb = _norm_matmul(h, lv["g_attn"], lw["w_b"], lw["scale_b"], BF16)
        zf = _norm_matmul(h, lv["g_attn"], lw["w_f"], lw["scale_f"], F32)
        q, k, v = _mla_proj(zf, lv["g_qa"], lv["g_kva"], lw["wq"], lw["wkv"], rope_c, rope_s, seq)
        o_mla = _mla_flash(q, k, v, batch, seq)
        o_na = _na_attention(zb, (0, nb, 2 * nb), lw["na_bias"], batch, seq)
        o_diff = _diff_flash(zb, (3 * nb, 4 * nb, 5 * nb), slopes, lv["lam_q1"], lv["lam_k1"], lv["lam_q2"],
                             lv["lam_k2"], lv["g_subln"], lw["lam_init"], batch, seq)
        h = _out_proj(o_mla, o_na, o_diff, lw["wo_mla"], lw["wo_na"], lw["wo_diff"], h)
        h = _ffn(h, lv["g_ffn"], lw["w_gu"], lv["conv_w"], lv["conv_b"], lw["w_down"], seq)
        h = _ple(h, p[i].reshape(t, -1), lv["g_ple"], lw["w_plg"], lw["w_ple"])
    return _final_norm(h, g_final.reshape(1, d)).reshape(batch, seq, d)


def kernel(x_prompt, x_sample, p_prompt, p_sample, g_attn, w_in, g_qa, w_qb, g_kva, w_kvb, rpb, lam_q1, lam_k1,
           lam_q2, lam_k2, g_subln, w_out, g_ffn, w_gu, conv_w, conv_b, w_down, g_ple, w_plg, w_ple, g_final):
    depth = w_in.shape[0]
    layers = [_prepare_layer(i, w_in[i], g_qa[i], w_qb[i], g_kva[i], w_kvb[i], rpb[i], w_out[i], w_gu[i],
                             w_down[i], w_plg[i], w_ple[i]) for i in range(depth)]
    row = lambda a: a.reshape(1, -1)
    vecs = [dict(g_attn=row(g_attn[i]), g_qa=row(g_qa[i]), g_kva=row(g_kva[i]), lam_q1=row(lam_q1[i]),
                 lam_k1=row(lam_k1[i]), lam_q2=row(lam_q2[i]), lam_k2=row(lam_k2[i]), g_subln=row(g_subln[i]),
                 g_ffn=row(g_ffn[i]), conv_w=conv_w[i], conv_b=row(conv_b[i]), g_ple=row(g_ple[i]))
            for i in range(depth)]
    return (_trunk(x_prompt, p_prompt, layers, vecs, g_final), _trunk(x_sample, p_sample, layers, vecs, g_final))
```

```python
import functools
import math

import jax
import jax.numpy as jnp
import numpy as np
from jax import lax
from jax.experimental import pallas as pl
from jax.experimental.pallas import tpu as pltpu

F32 = jnp.float32
BF16 = jnp.bfloat16

GRID_W = 64
MLA_HEADS = 8
MLA_NOPE = 128
MLA_ROPE = 64
MLA_V = 128
MLA_QK_PAD = 256
ROPE_THETA = 10000.0
NA_HEADS = 4
NA_HEAD_DIM = 128
NA_WIN_ROWS = 8
NA_WIN_COLS = 16
NA_ROWS_PER_STEP = 8
DIFF_HEADS = 4
DIFF_QK_DIM = 64
DIFF_V_DIM = 128
RMS_EPS = 1e-6
LOG2E = 1.4426950408889634
MASK_VALUE = -1e30
EXP2_UNDERFLOW = 160.0
NORM_BOUND_SLACK = 1.001
LANES = 128
BF16_ROWS = 16
VMEM_LIMIT = 56 * 1024 * 1024


def _tile(n, pref):
    t = min(n, pref)
    assert n % t == 0, (n, pref)
    return t


def _rms(x, g):
    ms = jnp.mean(x * x, axis=-1, keepdims=True)
    return x * lax.rsqrt(ms + RMS_EPS) * g


def _params(*sem):
    return pltpu.CompilerParams(dimension_semantics=sem, vmem_limit_bytes=VMEM_LIMIT)


def _norm_matmul_kernel(x_ref, g_ref, w_ref, cs_ref, o_ref, xn_ref):
    @pl.when(pl.program_id(1) == 0)
    def _():
        xn_ref[...] = _rms(x_ref[...], g_ref[...]).astype(BF16)

    acc = jnp.dot(xn_ref[...], w_ref[...], preferred_element_type=F32)
    o_ref[...] = (acc * cs_ref[...]).astype(o_ref.dtype)


def _norm_matmul(x, g, w, col_scale, out_dtype, tm=1024, tn=512):
    t, d = x.shape
    n = w.shape[1]
    tm, tn = _tile(t, tm), _tile(n, tn)
    return pl.pallas_call(
        _norm_matmul_kernel,
        grid=(t // tm, n // tn),
        in_specs=[
            pl.BlockSpec((tm, d), lambda i, j: (i, 0)),
            pl.BlockSpec((1, d), lambda i, j: (0, 0)),
            pl.BlockSpec((d, tn), lambda i, j: (0, j)),
            pl.BlockSpec((1, tn), lambda i, j: (0, j)),
        ],
        out_specs=pl.BlockSpec((tm, tn), lambda i, j: (i, j)),
        out_shape=jax.ShapeDtypeStruct((t, n), out_dtype),
        scratch_shapes=[pltpu.VMEM((tm, d), BF16)],
        compiler_params=_params("parallel", "arbitrary"),
        name="norm_matmul",
    )(x, g, w, col_scale)


def _mla_proj_kernel(zf_ref, gq_ref, gkv_ref, wq_ref, wkv_ref, c_ref, s_ref, q_ref, k_ref, v_ref, *,
                     q_rank, kv_rank, q_scale):
    zf = zf_ref[...]
    cos, sin = c_ref[...], s_ref[...]
    qn = _rms(zf[:, :q_rank], gq_ref[...]).astype(BF16)
    qa = jnp.dot(qn, wq_ref[...], preferred_element_type=F32)
    kvn = _rms(zf[:, q_rank:q_rank + kv_rank], gkv_ref[...]).astype(BF16)
    kva = jnp.dot(kvn, wkv_ref[...], preferred_element_type=F32)
    o = q_rank + kv_rank
    kpe = (zf[:, o:o + LANES] * cos + zf[:, o + LANES:o + 2 * LANES] * sin).astype(BF16)
    hp = MLA_HEADS * LANES
    for h in range(MLA_HEADS):
        lo = h * LANES
        rope = qa[:, hp + lo:hp + lo + LANES] * cos + qa[:, 2 * hp + lo:2 * hp + lo + LANES] * sin
        q_ref[:, 2 * lo:2 * lo + LANES] = (qa[:, lo:lo + LANES] * q_scale).astype(BF16)
        q_ref[:, 2 * lo + LANES:2 * lo + 2 * LANES] = (rope * q_scale).astype(BF16)
        k_ref[:, 2 * lo:2 * lo + LANES] = kva[:, 2 * lo:2 * lo + LANES].astype(BF16)
        k_ref[:, 2 * lo + LANES:2 * lo + 2 * LANES] = kpe
        v_ref[:, lo:lo + LANES] = kva[:, 2 * lo + LANES:2 * lo + 2 * LANES].astype(BF16)


def _mla_proj(zf, gq, gkv, wq, wkv, rope_c, rope_s, seq, tm=512):
    t, zw = zf.shape
    tm = _tile(seq, tm)
    nl = seq // tm
    q_rank, kv_rank = gq.shape[1], gkv.shape[1]
    kern = functools.partial(_mla_proj_kernel, q_rank=q_rank, kv_rank=kv_rank,
                             q_scale=(MLA_NOPE + MLA_ROPE) ** -0.5 * LOG2E)
    qk_w = MLA_HEADS * MLA_QK_PAD
    return pl.pallas_call(
        kern,
        grid=(t // tm,),
        in_specs=[
            pl.BlockSpec((tm, zw), lambda i: (i, 0)),
            pl.BlockSpec((1, q_rank), lambda i: (0, 0)),
            pl.BlockSpec((1, kv_rank), lambda i: (0, 0)),
            pl.BlockSpec(wq.shape, lambda i: (0, 0)),
            pl.BlockSpec(wkv.shape, lambda i: (0, 0)),
            pl.BlockSpec((tm, LANES), lambda i: (i % nl, 0)),
            pl.BlockSpec((tm, LANES), lambda i: (i % nl, 0)),
        ],
        out_specs=[
            pl.BlockSpec((tm, qk_w), lambda i: (i, 0)),
            pl.BlockSpec((tm, qk_w), lambda i: (i, 0)),
            pl.BlockSpec((tm, MLA_HEADS * MLA_V), lambda i: (i, 0)),
        ],
        out_shape=[
            jax.ShapeDtypeStruct((t, qk_w), BF16),
            jax.ShapeDtypeStruct((t, qk_w), BF16),
            jax.ShapeDtypeStruct((t, MLA_HEADS * MLA_V), BF16),
        ],
        compiler_params=_params("parallel"),
        name="mla_proj",
    )(zf, gq, gkv, wq, wkv, rope_c, rope_s)


def _online_softmax_step(s, v, m_ref, acc_ref, shift):
    m_prev = m_ref[...]
    m_cur = jnp.max(s, axis=1, keepdims=True)
    if shift is not None:
        m_cur = m_cur + shift
    m_next = jnp.maximum(m_prev, m_cur)
    alpha = jnp.exp2(m_prev - m_next)
    m_sub = m_next if shift is None else m_next - shift
    p = jnp.exp2(s - pltpu.repeat(m_sub, s.shape[1] // LANES, axis=1))
    v1 = jnp.concatenate([v, jnp.ones_like(v)], axis=1)
    acc_ref[...] = (pltpu.repeat(alpha, 2, axis=1) * acc_ref[...]
                    + jnp.dot(p.astype(BF16), v1, preferred_element_type=F32))
    m_ref[...] = m_next


def _init_softmax_state(m_ref, acc_ref):
    m_ref[...] = jnp.full(m_ref.shape, MASK_VALUE, F32)
    acc_ref[...] = jnp.zeros(acc_ref.shape, F32)


def _softmax_result(acc_ref):
    acc = acc_ref[...]
    return acc[:, :LANES] / acc[:, LANES:]


def _qk(q, k):
    return lax.dot_general(q, k, (((1,), (1,)), ((), ())), preferred_element_type=F32)


def _mla_flash_kernel(q_ref, k_ref, v_ref, o_ref, m_ref, acc_ref, *, tkc):
    ki = pl.program_id(3)

    @pl.when(ki == 0)
    def _():
        _init_softmax_state(m_ref, acc_ref)

    q = q_ref[...]
    for c in range(k_ref.shape[0] // tkc):
        s = _qk(q, k_ref[c * tkc:(c + 1) * tkc, :])
        _online_softmax_step(s, v_ref[c * tkc:(c + 1) * tkc, :], m_ref, acc_ref, None)

    @pl.when(ki == pl.num_programs(3) - 1)
    def _():
        o_ref[...] = _softmax_result(acc_ref).astype(o_ref.dtype)


def _mla_flash(q, k, v, batch, seq, tq=1024, tkb=4096, tkc=512):
    t = q.shape[0]
    tq, tkb = _tile(seq, tq), _tile(seq, tkb)
    tkc = _tile(tkb, tkc)
    nq, nk = seq // tq, seq // tkb
    return pl.pallas_call(
        functools.partial(_mla_flash_kernel, tkc=tkc),
        grid=(MLA_HEADS, batch, nq, nk),
        in_specs=[
            pl.BlockSpec((tq, MLA_QK_PAD), lambda g, b, qi, ki: (b * nq + qi, g)),
            pl.BlockSpec((tkb, MLA_QK_PAD), lambda g, b, qi, ki: (b * nk + ki, g)),
            pl.BlockSpec((tkb, MLA_V), lambda g, b, qi, ki: (b * nk + ki, g)),
        ],
        out_specs=pl.BlockSpec((tq, MLA_V), lambda g, b, qi, ki: (b * nq + qi, g)),
        out_shape=jax.ShapeDtypeStruct((t, MLA_HEADS * MLA_V), BF16),
        scratch_shapes=[pltpu.VMEM((tq, LANES), F32), pltpu.VMEM((tq, MLA_V + LANES), F32)],
        compiler_params=_params("parallel", "parallel", "parallel", "arbitrary"),
        name="mla_flash",
    )(q, k, v)


def _key_norm_kernel(k_ref, o_ref):
    @pl.when(pl.program_id(2) == 0)
    def _():
        o_ref[...] = jnp.zeros(o_ref.shape, F32)

    k = k_ref[...].astype(F32)
    top = jnp.max(jnp.sum(k * k, axis=1, keepdims=True), axis=0, keepdims=True)
    o_ref[...] = jnp.maximum(o_ref[...], top)


def _max_key_norm_sq(zb, kc, batch, seq, tk=2048):
    tk = _tile(seq, tk)
    nk = seq // tk
    return pl.pallas_call(
        _key_norm_kernel,
        grid=(batch, DIFF_HEADS, nk),
        in_specs=[pl.BlockSpec((tk, LANES), lambda b, g, i: (b * nk + i, kc + g))],
        out_specs=pl.BlockSpec((8, LANES), lambda b, g, i: (b * DIFF_HEADS + g, 0)),
        out_shape=jax.ShapeDtypeStruct((batch * DIFF_HEADS * 8, LANES), F32),
        compiler_params=_params("parallel", "parallel", "arbitrary"),
        name="key_norm",
    )(zb)


def _diff_flash_kernel(slope_ref, kn_ref, lq1_ref, lk1_ref, lq2_ref, lk2_ref, gsub_ref, q_ref, k_ref, v_ref, o_ref,
                       qs_ref, m_ref, acc_ref, bias_ref, cap_ref, *, tq, lam_init):
    g, qi, ki = pl.program_id(0), pl.program_id(2), pl.program_id(3)
    slope = slope_ref[g]
    tkb = k_ref.shape[0]
    q0 = qi * tq
    kb0 = ((q0 // tkb + ki) % pl.num_programs(3)) * tkb

    @pl.when(ki == 0)
    def _():
        _init_softmax_state(m_ref, acc_ref)
        q = q_ref[...]
        lane = lax.broadcasted_iota(jnp.int32, q.shape, 1)
        qs_ref[:tq] = jnp.where(lane < DIFF_QK_DIM, q, jnp.zeros_like(q))
        qs_ref[tq:] = jnp.where(lane >= DIFF_QK_DIM, q, jnp.zeros_like(q))
        qf = q.astype(F32)
        q_norm = jnp.sqrt(jnp.sum(qf * qf, axis=1, keepdims=True))
        cap = q_norm * jnp.sqrt(kn_ref[0:1, :]) * NORM_BOUND_SLACK + NORM_BOUND_SLACK
        cap_ref[:tq] = cap
        cap_ref[tq:] = cap
        shape = (2 * tq, tq)
        row = lax.broadcasted_iota(jnp.int32, shape, 0)
        row = jnp.where(row >= tq, row - tq, row)
        d = (row - lax.broadcasted_iota(jnp.int32, shape, 1)).astype(F32)
        bias_ref[0] = -slope * d
        bias_ref[1] = -slope * jnp.abs(d)
        bias_ref[2] = slope * d

    nearest = jnp.maximum(jnp.maximum(q0 - (kb0 + tkb) + 1, kb0 - (q0 + tq) + 1), 0)
    best = jnp.max(cap_ref[...] - m_ref[...], axis=0, keepdims=True)
    floor = slope * jnp.full((1, LANES), nearest, jnp.int32).astype(F32) - EXP2_UNDERFLOW
    live = jnp.where(best >= floor, 1, 0)[0, 0]

    @pl.when(live == 1)
    def _():
        qs = qs_ref[...]
        for c in range(tkb // tq):
            k0 = kb0 + c * tq
            which = jnp.where(k0 < q0, 0, jnp.where(k0 == q0, 1, 2))
            gap = jnp.full((1, LANES), jnp.abs(q0 - k0), jnp.int32).astype(F32)
            s = _qk(qs, k_ref[c * tq:(c + 1) * tq, :]) + bias_ref[which]
            _online_softmax_step(s, v_ref[c * tq:(c + 1) * tq, :], m_ref, acc_ref, -slope * gap)

    @pl.when(ki == pl.num_programs(3) - 1)
    def _():
        o = _softmax_result(acc_ref)
        lam = (jnp.exp(jnp.sum(lq1_ref[...] * lk1_ref[...], axis=-1, keepdims=True))
               - jnp.exp(jnp.sum(lq2_ref[...] * lk2_ref[...], axis=-1, keepdims=True)) + lam_init)
        d = o[:tq] - lam * o[tq:]
        o_ref[...] = (_rms(d, gsub_ref[...]) * (1.0 - lam_init)).astype(o_ref.dtype)


def _diff_flash(zb, col0, slopes, lq1, lk1, lq2, lk2, gsub, lam_init, batch, seq, tq=512, tkb=4096):
    t = zb.shape[0]
    tq = _tile(seq, tq)
    tkb = _tile(seq, tkb)
    assert tkb % tq == 0
    nq, nk = seq // tq, seq // tkb
    qc, kc, vc = col0
    key_norm = _max_key_norm_sq(zb, kc, batch, seq)
    vec = lambda a: pl.BlockSpec(a.shape, lambda g, b, qi, ki: (0, 0))
    kblock = lambda b, qi, ki: b * nk + (qi * tq // tkb + ki) % nk
    return pl.pallas_call(
        functools.partial(_diff_flash_kernel, tq=tq, lam_init=lam_init),
        grid=(DIFF_HEADS, batch, nq, nk),
        in_specs=[
            pl.BlockSpec(memory_space=pltpu.SMEM),
            pl.BlockSpec((8, LANES), lambda g, b, qi, ki: (b * DIFF_HEADS + g, 0)),
            vec(lq1), vec(lk1), vec(lq2), vec(lk2), vec(gsub),
            pl.BlockSpec((tq, LANES), lambda g, b, qi, ki: (b * nq + qi, qc + g)),
            pl.BlockSpec((tkb, LANES), lambda g, b, qi, ki: (kblock(b, qi, ki), kc + g)),
            pl.BlockSpec((tkb, DIFF_V_DIM), lambda g, b, qi, ki: (kblock(b, qi, ki), vc + g)),
        ],
        out_specs=pl.BlockSpec((tq, DIFF_V_DIM), lambda g, b, qi, ki: (b * nq + qi, g)),
        out_shape=jax.ShapeDtypeStruct((t, DIFF_HEADS * DIFF_V_DIM), BF16),
        scratch_shapes=[pltpu.VMEM((2 * tq, LANES), BF16), pltpu.VMEM((2 * tq, LANES), F32),
                        pltpu.VMEM((2 * tq, DIFF_V_DIM + LANES), F32), pltpu.VMEM((3, 2 * tq, tq), F32),
                        pltpu.VMEM((2 * tq, LANES), F32)],
        compiler_params=_params("parallel", "parallel", "parallel", "arbitrary"),
        name="diff_flash",
    )(slopes, key_norm, lq1, lk1, lq2, lk2, gsub, zb, zb, zb)


def _na_kernel(q_ref, kp_ref, kc_ref, kn_ref, vp_ref, vc_ref, vn_ref, bias_ref, o_ref, kb_ref, vb_ref, *, rows):
    j = pl.program_id(2)
    blk = NA_ROWS_PER_STEP * GRID_W
    band = NA_WIN_ROWS * GRID_W
    for n, (kr, vr) in enumerate(((kp_ref, vp_ref), (kc_ref, vc_ref), (kn_ref, vn_ref))):
        kb_ref[n * blk:(n + 1) * blk] = kr[...]
        vb_ref[n * blk:(n + 1) * blk] = vr[...]
    for i in range(NA_ROWS_PER_STEP):
        r = j * NA_ROWS_PER_STEP + i
        row_start = jnp.clip(r - NA_WIN_ROWS // 2, 0, rows - NA_WIN_ROWS)
        start = pl.multiple_of((row_start - (j - 1) * NA_ROWS_PER_STEP) * GRID_W, GRID_W)
        q = q_ref[i * GRID_W:(i + 1) * GRID_W, :]
        s = _qk(q, kb_ref[pl.ds(start, band), :]) + bias_ref[r - row_start]
        p = jnp.exp2(s - jnp.max(s, axis=1, keepdims=True))
        o = jnp.dot(p.astype(BF16), vb_ref[pl.ds(start, band), :], preferred_element_type=F32)
        o_ref[i * GRID_W:(i + 1) * GRID_W, :] = (o / jnp.sum(p, axis=1, keepdims=True)).astype(o_ref.dtype)


def _na_attention(zb, col0, bias, batch, seq):
    t = zb.shape[0]
    rows = seq // GRID_W
    assert rows % NA_ROWS_PER_STEP == 0 and rows >= NA_WIN_ROWS
    nj = rows // NA_ROWS_PER_STEP
    blk = NA_ROWS_PER_STEP * GRID_W
    qc, kc, vc = col0

    def spec(c0, dj):
        return pl.BlockSpec((blk, NA_HEAD_DIM),
                            lambda h, b, j: (b * nj + jnp.clip(j + dj, 0, nj - 1), c0 + h))

    return pl.pallas_call(
        functools.partial(_na_kernel, rows=rows),
        grid=(NA_HEADS, batch, nj),
        in_specs=[spec(qc, 0), spec(kc, -1), spec(kc, 0), spec(kc, 1), spec(vc, -1), spec(vc, 0), spec(vc, 1),
                  pl.BlockSpec((None,) + bias.shape[1:], lambda h, b, j: (h, 0, 0, 0))],
        out_specs=pl.BlockSpec((blk, NA_HEAD_DIM), lambda h, b, j: (b * nj + j, h)),
        out_shape=jax.ShapeDtypeStruct((t, NA_HEADS * NA_HEAD_DIM), BF16),
        scratch_shapes=[pltpu.VMEM((3 * blk, NA_HEAD_DIM), BF16), pltpu.VMEM((3 * blk, NA_HEAD_DIM), BF16)],
        compiler_params=_params("parallel", "parallel", "parallel"),
        name="na_attention",
    )(zb, zb, zb, zb, zb, zb, zb, bias)


def _na_bias_table(rpb):
    col = np.arange(GRID_W)
    col_start = np.clip(col - NA_WIN_COLS // 2, 0, GRID_W - NA_WIN_COLS)
    kc = np.arange(GRID_W)
    valid = (kc[None, :] >= col_start[:, None]) & (kc[None, :] < col_start[:, None] + NA_WIN_COLS)
    by_row = jnp.stack([rpb[:, NA_WIN_ROWS - 1 - t:2 * NA_WIN_ROWS - 1 - t] for t in range(NA_WIN_ROWS)], axis=1)
    pad = GRID_W - NA_WIN_COLS
    padded = jnp.pad(by_row, ((0, 0), (0, 0), (0, 0), (pad, pad)))
    tbl = jnp.stack([padded[..., GRID_W - 1 - c:2 * GRID_W - 1 - c] for c in range(GRID_W)], axis=2)
    tbl = jnp.where(valid[None, None, :, None, :], tbl * LOG2E, MASK_VALUE)
    return tbl.reshape(rpb.shape[0], NA_WIN_ROWS, GRID_W, NA_WIN_ROWS * GRID_W).astype(F32)


def _out_proj_kernel(a_ref, b_ref, c_ref, wa_ref, wb_ref, wc_ref, h_ref, o_ref):
    acc = jnp.dot(a_ref[...], wa_ref[...], preferred_element_type=F32)
    acc = acc + jnp.dot(b_ref[...], wb_ref[...], preferred_element_type=F32)
    acc = acc + jnp.dot(c_ref[...], wc_ref[...], preferred_element_type=F32)
    o_ref[...] = h_ref[...] + acc


def _out_proj(a, b, c, wa, wb, wc, h, tm=1024, tn=512):
    t, d = h.shape
    tm, tn = _tile(t, tm), _tile(d, tn)
    act = lambda x: pl.BlockSpec((tm, x.shape[1]), lambda i, j: (i, 0))
    wgt = lambda w: pl.BlockSpec((w.shape[0], tn), lambda i, j: (0, j))
    return pl.pallas_call(
        _out_proj_kernel,
        grid=(t // tm, d // tn),
        in_specs=[act(a), act(b), act(c), wgt(wa), wgt(wb), wgt(wc),
                  pl.BlockSpec((tm, tn), lambda i, j: (i, j))],
        out_specs=pl.BlockSpec((tm, tn), lambda i, j: (i, j)),
        out_shape=jax.ShapeDtypeStruct((t, d), F32),
        compiler_params=_params("parallel", "arbitrary"),
        name="out_proj",
    )(a, b, c, wa, wb, wc, h)


def _ffn_kernel(h_ref, hp_ref, hn_ref, g_ref, wg_ref, wu_ref, cw_ref, cb_ref, wd_ref, o_ref, xn_ref, *,
                tm, tiles_per_seq):
    i, j = pl.program_id(0), pl.program_id(1)

    @pl.when(j == 0)
    def _():
        x = h_ref[...]
        g = g_ref[...]
        xn_ref[:tm] = _rms(x, g).astype(BF16)
        pos = i % tiles_per_seq
        before = jnp.where(pos == 0, 0.0, _rms(hp_ref[...], g)[BF16_ROWS - 1:BF16_ROWS])
        after = jnp.where(pos == tiles_per_seq - 1, 0.0, _rms(hn_ref[...], g)[0:1])
        rid = lax.broadcasted_iota(jnp.int32, (BF16_ROWS, x.shape[1]), 0)
        halo = jnp.where(rid == 0, before, jnp.where(rid == 1, after, 0.0))
        xn_ref[tm:] = halo.astype(BF16)
        o_ref[...] = x

    ga = jnp.dot(xn_ref[...], wg_ref[...], preferred_element_type=F32)
    up = jnp.dot(xn_ref[:tm], wu_ref[...], preferred_element_type=F32)
    gate = ga[:tm]
    rid = lax.broadcasted_iota(jnp.int32, gate.shape, 0)
    g_prev = jnp.where(rid == 0, ga[tm:tm + 1], pltpu.roll(gate, 1, 0))
    g_next = jnp.where(rid == tm - 1, ga[tm + 1:tm + 2], pltpu.roll(gate, tm - 1, 0))
    cw = cw_ref[...]
    conv = g_prev * cw[0:1] + gate * cw[1:2] + g_next * cw[2:3] + cb_ref[...]
    act = (jax.nn.gelu(conv) * up).astype(BF16)
    o_ref[...] += jnp.dot(act, wd_ref[...], preferred_element_type=F32)


def _ffn(h, g, w_gu, conv_w, conv_b, w_down, seq, tm=512, tf=512):
    t, d = h.shape
    f = w_down.shape[0]
    tm, tf = _tile(seq, tm), _tile(f, tf)
    nf = f // tf
    hb = tm // BF16_ROWS
    nhb = t // BF16_ROWS
    return pl.pallas_call(
        functools.partial(_ffn_kernel, tm=tm, tiles_per_seq=seq // tm),
        grid=(t // tm, nf),
        in_specs=[
            pl.BlockSpec((tm, d), lambda i, j: (i, 0)),
            pl.BlockSpec((BF16_ROWS, d), lambda i, j: (jnp.maximum(i * hb - 1, 0), 0)),
            pl.BlockSpec((BF16_ROWS, d), lambda i, j: (jnp.minimum((i + 1) * hb, nhb - 1), 0)),
            pl.BlockSpec((1, d), lambda i, j: (0, 0)),
            pl.BlockSpec((d, tf), lambda i, j: (0, j)),
            pl.BlockSpec((d, tf), lambda i, j: (0, nf + j)),
            pl.BlockSpec((conv_w.shape[0], tf), lambda i, j: (0, j)),
            pl.BlockSpec((1, tf), lambda i, j: (0, j)),
            pl.BlockSpec((tf, d), lambda i, j: (j, 0)),
        ],
        out_specs=pl.BlockSpec((tm, d), lambda i, j: (i, 0)),
        out_shape=jax.ShapeDtypeStruct((t, d), F32),
        scratch_shapes=[pltpu.VMEM((tm + BF16_ROWS, d), BF16)],
        compiler_params=_params("parallel", "arbitrary"),
        name="ffn",
    )(h, h, h, g, w_gu, w_gu, conv_w, conv_b, w_down)


def _ple_kernel(hrow_ref, h_ref, p_ref, g_ref, wg_ref, wp_ref, o_ref, xn_ref):
    @pl.when(pl.program_id(1) == 0)
    def _():
        xn_ref[...] = _rms(hrow_ref[...], g_ref[...]).astype(BF16)

    gate = jax.nn.sigmoid(jnp.dot(xn_ref[...], wg_ref[...], preferred_element_type=F32))
    emb = jnp.dot(p_ref[...].astype(BF16), wp_ref[...], preferred_element_type=F32)
    o_ref[...] = h_ref[...] + emb * gate


def _ple(h, p, g, w_plg, w_ple, tm=1024, tn=512):
    t, d = h.shape
    tm, tn = _tile(t, tm), _tile(d, tn)
    return pl.pallas_call(
        _ple_kernel,
        grid=(t // tm, d // tn),
        in_specs=[
            pl.BlockSpec((tm, d), lambda i, j: (i, 0)),
            pl.BlockSpec((tm, tn), lambda i, j: (i, j)),
            pl.BlockSpec((tm, p.shape[1]), lambda i, j: (i, 0)),
            pl.BlockSpec((1, d), lambda i, j: (0, 0)),
            pl.BlockSpec((d, tn), lambda i, j: (0, j)),
            pl.BlockSpec((p.shape[1], tn), lambda i, j: (0, j)),
        ],
        out_specs=pl.BlockSpec((tm, tn), lambda i, j: (i, j)),
        out_shape=jax.ShapeDtypeStruct((t, d), F32),
        scratch_shapes=[pltpu.VMEM((tm, d), BF16)],
        compiler_params=_params("parallel", "arbitrary"),
        name="ple",
    )(h, h, p, g, w_plg, w_ple)


def _final_norm_kernel(x_ref, g_ref, o_ref):
    o_ref[...] = _rms(x_ref[...], g_ref[...])


def _final_norm(h, g, tm=1024):
    t, d = h.shape
    tm = _tile(t, tm)
    return pl.pallas_call(
        _final_norm_kernel,
        grid=(t // tm,),
        in_specs=[pl.BlockSpec((tm, d), lambda i: (i, 0)), pl.BlockSpec((1, d), lambda i: (0, 0))],
        out_specs=pl.BlockSpec((tm, d), lambda i: (i, 0)),
        out_shape=jax.ShapeDtypeStruct((t, d), F32),
        compiler_params=_params("parallel"),
        name="final_norm",
    )(h, g)


def _swap_halves(x):
    half = x.shape[-1] // 2
    return jnp.concatenate([x[..., half:], x[..., :half]], axis=-1)


def _pad_lanes(x):
    return jnp.concatenate([x, jnp.zeros(x.shape[:-1] + (LANES - x.shape[-1],), x.dtype)], axis=-1)


def _prepare_layer(i, w_in, g_qa, w_qb, g_kva, w_kvb, rpb, w_out, w_gu, w_down, w_plg, w_ple):
    d = w_in.shape[0]
    q_rank, kv_rank = g_qa.shape[0], g_kva.shape[0]
    o = q_rank + kv_rank
    kpe = w_in[:, o:o + MLA_ROPE]
    w_f = jnp.concatenate([w_in[:, :o], _pad_lanes(kpe), _pad_lanes(_swap_halves(kpe))], axis=1)
    w_b = w_in[:, o + MLA_ROPE:]
    na_w = NA_HEADS * NA_HEAD_DIM
    diff_w = DIFF_HEADS * 2 * DIFF_QK_DIM
    scale_b = np.ones((1, w_b.shape[1]), np.float32)
    scale_b[:, :na_w] = NA_HEAD_DIM ** -0.5 * LOG2E
    scale_b[:, 3 * na_w:3 * na_w + diff_w] = DIFF_QK_DIM ** -0.5 * LOG2E
    wq = w_qb.reshape(q_rank, MLA_HEADS, MLA_NOPE + MLA_ROPE)
    pe = wq[..., MLA_NOPE:]
    wq = jnp.concatenate([wq[..., :MLA_NOPE].reshape(q_rank, -1), _pad_lanes(pe).reshape(q_rank, -1),
                          _pad_lanes(_swap_halves(pe)).reshape(q_rank, -1)], axis=1)
    mla_w = MLA_HEADS * MLA_V
    return dict(
        w_f=w_f.astype(BF16), scale_f=jnp.ones((1, w_f.shape[1]), F32),
        w_b=w_b.astype(BF16), scale_b=jnp.asarray(scale_b),
        wq=wq.astype(BF16), wkv=w_kvb.astype(BF16),
        na_bias=_na_bias_table(rpb),
        wo_mla=w_out[:mla_w].astype(BF16), wo_na=w_out[mla_w:mla_w + na_w].astype(BF16),
        wo_diff=w_out[mla_w + na_w:].astype(BF16),
        w_gu=w_gu.astype(BF16), w_down=w_down.astype(BF16),
        w_plg=w_plg.astype(BF16), w_ple=w_ple.astype(BF16),
        lam_init=0.8 - 0.6 * math.exp(-0.3 * i),
    )


def _rope_tables(seq):
    half = MLA_ROPE // 2
    freqs = jnp.power(ROPE_THETA, -jnp.arange(half, dtype=F32) / half)
    ang = jnp.arange(seq, dtype=jnp.int32).astype(F32)[:, None] * freqs[None, :]
    cos, sin = jnp.cos(ang), jnp.sin(ang)
    return _pad_lanes(jnp.concatenate([cos, cos], axis=1)), _pad_lanes(jnp.concatenate([-sin, sin], axis=1))


def _trunk(x, p, layers, vecs, g_final):
    batch, seq, d = x.shape
    t = batch * seq
    h = x.reshape(t, d)
    rope_c, rope_s = _rope_tables(seq)
    slopes = jnp.exp2(-8.0 * jnp.arange(1, DIFF_HEADS + 1, dtype=F32) / DIFF_HEADS) * LOG2E
    nb = NA_HEADS * NA_HEAD_DIM // LANES
    for i, (lw, lv) in enumerate(zip(layers, vecs)):
        zb = _norm_matmul(h, lv["g_attn"], lw["w_b"], lw["scale_b"], BF16)
        zf = _norm_matmul(h, lv["g_attn"], lw["w_f"], lw["scale_f"], F32)
        q, k, v = _mla_proj(zf, lv["g_qa"], lv["g_kva"], lw["wq"], lw["wkv"], rope_c, rope_s, seq)
        o_mla = _mla_flash(q, k, v, batch, seq)
        o_na = _na_attention(zb, (0, nb, 2 * nb), lw["na_bias"], batch, seq)
        o_diff = _diff_flash(zb, (3 * nb, 4 * nb, 5 * nb), slopes, lv["lam_q1"], lv["lam_k1"], lv["lam_q2"],
                             lv["lam_k2"], lv["g_subln"], lw["lam_init"], batch, seq)
        h = _out_proj(o_mla, o_na, o_diff, lw["wo_mla"], lw["wo_na"], lw["wo_diff"], h)
        h = _ffn(h, lv["g_ffn"], lw["w_gu"], lv["conv_w"], lv["conv_b"], lw["w_down"], seq)
        h = _ple(h, p[i].reshape(t, -1), lv["g_ple"], lw["w_plg"], lw["w_ple"])
    return _final_norm(h, g_final.reshape(1, d)).reshape(batch, seq, d)


def kernel(x_prompt, x_sample, p_prompt, p_sample, g_attn, w_in, g_qa, w_qb, g_kva, w_kvb, rpb, lam_q1, lam_k1,
           lam_q2, lam_k2, g_subln, w_out, g_ffn, w_gu, conv_w, conv_b, w_down, g_ple, w_plg, w_ple, g_final):
    depth = w_in.shape[0]
    layers = [_prepare_layer(i, w_in[i], g_qa[i], w_qb[i], g_kva[i], w_kvb[i], rpb[i], w_out[i], w_gu[i],
                             w_down[i], w_plg[i], w_ple[i]) for i in range(depth)]
    row = lambda a: a.reshape(1, -1)
    vecs = [dict(g_attn=row(g_attn[i]), g_qa=row(g_qa[i]), g_kva=row(g_kva[i]), lam_q1=row(lam_q1[i]),
                 lam_k1=row(lam_k1[i]), lam_q2=row(lam_q2[i]), lam_k2=row(lam_k2[i]), g_subln=row(g_subln[i]),
                 g_ffn=row(g_ffn[i]), conv_w=conv_w[i], conv_b=row(conv_b[i]), g_ple=row(g_ple[i]))
            for i in range(depth)]
    return (_trunk(x_prompt, p_prompt, layers, vecs, g_final), _trunk(x_sample, p_sample, layers, vecs, g_final))
```

```python
import functools
import math

import jax
import jax.numpy as jnp
import numpy as np
from jax import lax
from jax.experimental import pallas as pl
from jax.experimental.pallas import tpu as pltpu

F32 = jnp.float32
BF16 = jnp.bfloat16

GRID_W = 64
MLA_HEADS = 8
MLA_NOPE = 128
MLA_ROPE = 64
MLA_V = 128
MLA_QK_PAD = 256
ROPE_THETA = 10000.0
NA_HEADS = 4
NA_HEAD_DIM = 128
NA_WIN_ROWS = 8
NA_WIN_COLS = 16
NA_ROWS_PER_STEP = 8
DIFF_HEADS = 4
DIFF_QK_DIM = 64
DIFF_V_DIM = 128
RMS_EPS = 1e-6
LOG2E = 1.4426950408889634
MASK_VALUE = -1e30
EXP2_UNDERFLOW = 160.0
NORM_BOUND_SLACK = 1.001
LANES = 128
BF16_ROWS = 16
VMEM_LIMIT = 56 * 1024 * 1024


def _tile(n, pref):
    t = min(n, pref)
    assert n % t == 0, (n, pref)
    return t


def _rms(x, g):
    ms = jnp.mean(x * x, axis=-1, keepdims=True)
    return x * lax.rsqrt(ms + RMS_EPS) * g


def _lane_tile(x, n):
    return jnp.concatenate([x] * n, axis=1)


def _params(*sem):
    return pltpu.CompilerParams(dimension_semantics=sem, vmem_limit_bytes=VMEM_LIMIT)


def _in_proj_kernel(x_ref, g_ref, w_ref, cs_ref, zb_ref, zf_ref, xn_ref, *, nb):
    j = pl.program_id(1)

    @pl.when(j == 0)
    def _():
        xn_ref[...] = _rms(x_ref[...], g_ref[...]).astype(BF16)

    acc = jnp.dot(xn_ref[...], w_ref[...], preferred_element_type=F32)

    @pl.when(j < nb)
    def _():
        zb_ref[...] = (acc * cs_ref[...]).astype(BF16)

    @pl.when(j >= nb)
    def _():
        zf_ref[...] = acc


def _in_proj(x, g, w, col_scale, n_bf16, tm=1024, tn=512):
    t, d = x.shape
    n = w.shape[1]
    tm, tn = _tile(t, tm), _tile(math.gcd(n_bf16, n - n_bf16), tn)
    nb, nf = n_bf16 // tn, (n - n_bf16) // tn
    return pl.pallas_call(
        functools.partial(_in_proj_kernel, nb=nb),
        grid=(t // tm, nb + nf),
        in_specs=[
            pl.BlockSpec((tm, d), lambda i, j: (i, 0)),
            pl.BlockSpec((1, d), lambda i, j: (0, 0)),
            pl.BlockSpec((d, tn), lambda i, j: (0, j)),
            pl.BlockSpec((1, tn), lambda i, j: (0, jnp.minimum(j, nb - 1))),
        ],
        out_specs=[pl.BlockSpec((tm, tn), lambda i, j: (i, jnp.minimum(j, nb - 1))),
                   pl.BlockSpec((tm, tn), lambda i, j: (i, jnp.maximum(j - nb, 0)))],
        out_shape=[jax.ShapeDtypeStruct((t, n_bf16), BF16), jax.ShapeDtypeStruct((t, n - n_bf16), F32)],
        scratch_shapes=[pltpu.VMEM((tm, d), BF16)],
        compiler_params=_params("parallel", "arbitrary"),
        name="in_proj",
    )(x, g, w, col_scale)


def _mla_proj_kernel(zf_ref, gq_ref, gkv_ref, wq_ref, wkv_ref, c_ref, s_ref, q_ref, k_ref, v_ref, *,
                     q_rank, kv_rank, q_scale):
    zf = zf_ref[...]
    cos, sin = c_ref[...], s_ref[...]
    qn = _rms(zf[:, :q_rank], gq_ref[...]).astype(BF16)
    qa = jnp.dot(qn, wq_ref[...], preferred_element_type=F32)
    kvn = _rms(zf[:, q_rank:q_rank + kv_rank], gkv_ref[...]).astype(BF16)
    kva = jnp.dot(kvn, wkv_ref[...], preferred_element_type=F32)
    o = q_rank + kv_rank
    kpe = (zf[:, o:o + LANES] * cos + zf[:, o + LANES:o + 2 * LANES] * sin).astype(BF16)
    hp = MLA_HEADS * LANES
    for h in range(MLA_HEADS):
        lo = h * LANES
        rope = qa[:, hp + lo:hp + lo + LANES] * cos + qa[:, 2 * hp + lo:2 * hp + lo + LANES] * sin
        q_ref[:, 2 * lo:2 * lo + LANES] = (qa[:, lo:lo + LANES] * q_scale).astype(BF16)
        q_ref[:, 2 * lo + LANES:2 * lo + 2 * LANES] = (rope * q_scale).astype(BF16)
        k_ref[:, 2 * lo:2 * lo + LANES] = kva[:, 2 * lo:2 * lo + LANES].astype(BF16)
        k_ref[:, 2 * lo + LANES:2 * lo + 2 * LANES] = kpe
        v_ref[:, lo:lo + LANES] = kva[:, 2 * lo + LANES:2 * lo + 2 * LANES].astype(BF16)


def _mla_proj(zf, gq, gkv, wq, wkv, rope_c, rope_s, seq, tm=512):
    t, zw = zf.shape
    tm = _tile(seq, tm)
    nl = seq // tm
    q_rank, kv_rank = gq.shape[1], gkv.shape[1]
    kern = functools.partial(_mla_proj_kernel, q_rank=q_rank, kv_rank=kv_rank,
                             q_scale=(MLA_NOPE + MLA_ROPE) ** -0.5 * LOG2E)
    qk_w = MLA_HEADS * MLA_QK_PAD
    return pl.pallas_call(
        kern,
        grid=(t // tm,),
        in_specs=[
            pl.BlockSpec((tm, zw), lambda i: (i, 0)),
            pl.BlockSpec((1, q_rank), lambda i: (0, 0)),
            pl.BlockSpec((1, kv_rank), lambda i: (0, 0)),
            pl.BlockSpec(wq.shape, lambda i: (0, 0)),
            pl.BlockSpec(wkv.shape, lambda i: (0, 0)),
            pl.BlockSpec((tm, LANES), lambda i: (i % nl, 0)),
            pl.BlockSpec((tm, LANES), lambda i: (i % nl, 0)),
        ],
        out_specs=[
            pl.BlockSpec((tm, qk_w), lambda i: (i, 0)),
            pl.BlockSpec((tm, qk_w), lambda i: (i, 0)),
            pl.BlockSpec((tm, MLA_HEADS * MLA_V), lambda i: (i, 0)),
        ],
        out_shape=[
            jax.ShapeDtypeStruct((t, qk_w), BF16),
            jax.ShapeDtypeStruct((t, qk_w), BF16),
            jax.ShapeDtypeStruct((t, MLA_HEADS * MLA_V), BF16),
        ],
        compiler_params=_params("parallel"),
        name="mla_proj",
    )(zf, gq, gkv, wq, wkv, rope_c, rope_s)


def _online_softmax_step(s, v, m_ref, acc_ref, shift):
    m_prev = m_ref[...]
    m_cur = jnp.max(s, axis=1, keepdims=True)
    if shift is not None:
        m_cur = m_cur + shift
    m_next = jnp.maximum(m_prev, m_cur)
    alpha = jnp.exp2(m_prev - m_next)
    m_sub = m_next if shift is None else m_next - shift
    p = jnp.exp2(s - _lane_tile(m_sub, s.shape[1] // LANES))
    v1 = jnp.concatenate([v, jnp.ones_like(v)], axis=1)
    acc_ref[...] = _lane_tile(alpha, 2) * acc_ref[...] + jnp.dot(p.astype(BF16), v1, preferred_element_type=F32)
    m_ref[...] = m_next


def _init_softmax_state(m_ref, acc_ref):
    m_ref[...] = jnp.full(m_ref.shape, MASK_VALUE, F32)
    acc_ref[...] = jnp.zeros(acc_ref.shape, F32)


def _softmax_result(acc_ref):
    acc = acc_ref[...]
    return acc[:, :LANES] / acc[:, LANES:]


def _qk(q, k):
    return lax.dot_general(q, k, (((1,), (1,)), ((), ())), preferred_element_type=F32)


def _mla_flash_kernel(q_ref, k_ref, v_ref, o_ref, m_ref, acc_ref, *, tkc):
    ki = pl.program_id(3)

    @pl.when(ki == 0)
    def _():
        _init_softmax_state(m_ref, acc_ref)

    q = q_ref[...]
    for c in range(k_ref.shape[0] // tkc):
        s = _qk(q, k_ref[c * tkc:(c + 1) * tkc, :])
        _online_softmax_step(s, v_ref[c * tkc:(c + 1) * tkc, :], m_ref, acc_ref, None)

    @pl.when(ki == pl.num_programs(3) - 1)
    def _():
        o_ref[...] = _softmax_result(acc_ref).astype(o_ref.dtype)


def _mla_flash(q, k, v, batch, seq, tq=1024, tkb=8192, tkc=512):
    t = q.shape[0]
    tq, tkb = _tile(seq, tq), _tile(seq, tkb)
    tkc = _tile(tkb, tkc)
    nq, nk = seq // tq, seq // tkb
    return pl.pallas_call(
        functools.partial(_mla_flash_kernel, tkc=tkc),
        grid=(MLA_HEADS, batch, nq, nk),
        in_specs=[
            pl.BlockSpec((tq, MLA_QK_PAD), lambda g, b, qi, ki: (b * nq + qi, g)),
            pl.BlockSpec((tkb, MLA_QK_PAD), lambda g, b, qi, ki: (b * nk + ki, g)),
            pl.BlockSpec((tkb, MLA_V), lambda g, b, qi, ki: (b * nk + ki, g)),
        ],
        out_specs=pl.BlockSpec((tq, MLA_V), lambda g, b, qi, ki: (b * nq + qi, g)),
        out_shape=jax.ShapeDtypeStruct((t, MLA_HEADS * MLA_V), BF16),
        scratch_shapes=[pltpu.VMEM((tq, LANES), F32), pltpu.VMEM((tq, MLA_V + LANES), F32)],
        compiler_params=_params("parallel", "parallel", "parallel", "arbitrary"),
        name="mla_flash",
    )(q, k, v)


def _key_norm_kernel(k_ref, o_ref):
    @pl.when(pl.program_id(2) == 0)
    def _():
        o_ref[...] = jnp.zeros(o_ref.shape, F32)

    k = k_ref[...].astype(F32)
    top = jnp.max(jnp.sum(k * k, axis=1, keepdims=True), axis=0, keepdims=True)
    o_ref[...] = jnp.maximum(o_ref[...], top)


def _max_key_norm_sq(zb, kc, batch, seq, tk=2048):
    tk = _tile(seq, tk)
    nk = seq // tk
    return pl.pallas_call(
        _key_norm_kernel,
        grid=(batch, DIFF_HEADS, nk),
        in_specs=[pl.BlockSpec((tk, LANES), lambda b, g, i: (b * nk + i, kc + g))],
        out_specs=pl.BlockSpec((8, LANES), lambda b, g, i: (b * DIFF_HEADS + g, 0)),
        out_shape=jax.ShapeDtypeStruct((batch * DIFF_HEADS * 8, LANES), F32),
        compiler_params=_params("parallel", "parallel", "arbitrary"),
        name="key_norm",
    )(zb)


def _diff_flash_kernel(slope_ref, kn_ref, lq1_ref, lk1_ref, lq2_ref, lk2_ref, gsub_ref, q_ref, k_ref, v_ref, o_ref,
                       qs_ref, m_ref, acc_ref, bias_ref, cap_ref, *, tq, lam_init):
    g, qi, ki = pl.program_id(0), pl.program_id(2), pl.program_id(3)
    slope = slope_ref[g]
    tkb = k_ref.shape[0]
    q0 = qi * tq
    kb0 = ((q0 // tkb + ki) % pl.num_programs(3)) * tkb

    @pl.when(ki == 0)
    def _():
        _init_softmax_state(m_ref, acc_ref)
        q = q_ref[...]
        lane = lax.broadcasted_iota(jnp.int32, q.shape, 1)
        qs_ref[:tq] = jnp.where(lane < DIFF_QK_DIM, q, jnp.zeros_like(q))
        qs_ref[tq:] = jnp.where(lane >= DIFF_QK_DIM, q, jnp.zeros_like(q))
        qf = q.astype(F32)
        q_norm = jnp.sqrt(jnp.sum(qf * qf, axis=1, keepdims=True))
        cap = q_norm * jnp.sqrt(kn_ref[0:1, :]) * NORM_BOUND_SLACK + NORM_BOUND_SLACK
        cap_ref[:tq] = cap
        cap_ref[tq:] = cap
        shape = (2 * tq, tq)
        row = lax.broadcasted_iota(jnp.int32, shape, 0)
        row = jnp.where(row >= tq, row - tq, row)
        d = (row - lax.broadcasted_iota(jnp.int32, shape, 1)).astype(F32)
        bias_ref[0] = -slope * d
        bias_ref[1] = -slope * jnp.abs(d)
        bias_ref[2] = slope * d

    nch = tkb // tq
    best = jnp.max(cap_ref[...] - m_ref[...], axis=0, keepdims=True)
    cidx = lax.broadcasted_iota(jnp.int32, (8, LANES), 0)
    kc0 = kb0 + cidx * tq
    nearest = jnp.maximum(jnp.maximum(q0 - (kc0 + tq) + 1, kc0 - (q0 + tq) + 1), 0)
    live = (cidx < nch) & (best >= slope * nearest.astype(F32) - EXP2_UNDERFLOW)
    n_live = jnp.sum(jnp.where(live, 1, 0), axis=0, keepdims=True)[0, 0]
    first_live = jnp.min(jnp.where(live, cidx, nch), axis=0, keepdims=True)[0, 0]

    def chunk(c, off):
        k0 = kb0 + c * tq
        which = jnp.where(k0 < q0, 0, jnp.where(k0 == q0, 1, 2))
        gap = jnp.full((1, LANES), jnp.abs(q0 - k0), jnp.int32).astype(F32)
        s = _qk(qs_ref[...], k_ref[pl.ds(off, tq), :]) + bias_ref[which]
        _online_softmax_step(s, v_ref[pl.ds(off, tq), :], m_ref, acc_ref, -slope * gap)

    unrolled_from = min(nch, (5 * nch + 7) // 8)

    @pl.when(n_live >= unrolled_from)
    def _():
        for c in range(nch):
            chunk(c, c * tq)

    @pl.when((n_live > 0) & (n_live < unrolled_from))
    def _():
        def body(c, carry):
            chunk(c, pl.multiple_of(c * tq, tq))
            return carry

        lax.fori_loop(first_live, first_live + n_live, body, 0)

    @pl.when(ki == pl.num_programs(3) - 1)
    def _():
        o = _softmax_result(acc_ref)
        lam = (jnp.exp(jnp.sum(lq1_ref[...] * lk1_ref[...], axis=-1, keepdims=True))
               - jnp.exp(jnp.sum(lq2_ref[...] * lk2_ref[...], axis=-1, keepdims=True)) + lam_init)
        d = o[:tq] - lam * o[tq:]
        o_ref[...] = (_rms(d, gsub_ref[...]) * (1.0 - lam_init)).astype(o_ref.dtype)


def _diff_flash(zb, col0, slopes, lq1, lk1, lq2, lk2, gsub, lam_init, batch, seq, tq=512, tkb=4096):
    t = zb.shape[0]
    tq = _tile(seq, tq)
    tkb = _tile(seq, tkb)
    assert tkb % tq == 0 and tkb // tq <= 8
    nq, nk = seq // tq, seq // tkb
    qc, kc, vc = col0
    key_norm = _max_key_norm_sq(zb, kc, batch, seq)
    vec = lambda a: pl.BlockSpec(a.shape, lambda g, b, qi, ki: (0, 0))
    kblock = lambda b, qi, ki: b * nk + (qi * tq // tkb + ki) % nk
    return pl.pallas_call(
        functools.partial(_diff_flash_kernel, tq=tq, lam_init=lam_init),
        grid=(DIFF_HEADS, batch, nq, nk),
        in_specs=[
            pl.BlockSpec(memory_space=pltpu.SMEM),
            pl.BlockSpec((8, LANES), lambda g, b, qi, ki: (b * DIFF_HEADS + g, 0)),
            vec(lq1), vec(lk1), vec(lq2), vec(lk2), vec(gsub),
            pl.BlockSpec((tq, LANES), lambda g, b, qi, ki: (b * nq + qi, qc + g)),
            pl.BlockSpec((tkb, LANES), lambda g, b, qi, ki: (kblock(b, qi, ki), kc + g)),
            pl.BlockSpec((tkb, DIFF_V_DIM), lambda g, b, qi, ki: (kblock(b, qi, ki), vc + g)),
        ],
        out_specs=pl.BlockSpec((tq, DIFF_V_DIM), lambda g, b, qi, ki: (b * nq + qi, g)),
        out_shape=jax.ShapeDtypeStruct((t, DIFF_HEADS * DIFF_V_DIM), BF16),
        scratch_shapes=[pltpu.VMEM((2 * tq, LANES), BF16), pltpu.VMEM((2 * tq, LANES), F32),
                        pltpu.VMEM((2 * tq, DIFF_V_DIM + LANES), F32), pltpu.VMEM((3, 2 * tq, tq), F32),
                        pltpu.VMEM((2 * tq, LANES), F32)],
        compiler_params=_params("parallel", "parallel", "parallel", "arbitrary"),
        name="diff_flash",
    )(slopes, key_norm, lq1, lk1, lq2, lk2, gsub, zb, zb, zb)


def _na_kernel(q_ref, kp_ref, kc_ref, kn_ref, vp_ref, vc_ref, vn_ref, bias_ref, o_ref, kb_ref, vb_ref, *, rows):
    j = pl.program_id(2)
    blk = NA_ROWS_PER_STEP * GRID_W
    band = NA_WIN_ROWS * GRID_W
    for n, (kr, vr) in enumerate(((kp_ref, vp_ref), (kc_ref, vc_ref), (kn_ref, vn_ref))):
        kb_ref[n * blk:(n + 1) * blk] = kr[...]
        vb_ref[n * blk:(n + 1) * blk] = vr[...]
    for i in range(NA_ROWS_PER_STEP):
        r = j * NA_ROWS_PER_STEP + i
        row_start = jnp.clip(r - NA_WIN_ROWS // 2, 0, rows - NA_WIN_ROWS)
        start = pl.multiple_of((row_start - (j - 1) * NA_ROWS_PER_STEP) * GRID_W, GRID_W)
        q = q_ref[i * GRID_W:(i + 1) * GRID_W, :]
        s = _qk(q, kb_ref[pl.ds(start, band), :]) + bias_ref[r - row_start]
        p = jnp.exp2(s - jnp.max(s, axis=1, keepdims=True))
        o = jnp.dot(p.astype(BF16), vb_ref[pl.ds(start, band), :], preferred_element_type=F32)
        o_ref[i * GRID_W:(i + 1) * GRID_W, :] = (o / jnp.sum(p, axis=1, keepdims=True)).astype(o_ref.dtype)


def _na_attention(zb, col0, bias, batch, seq):
    t = zb.shape[0]
    rows = seq // GRID_W
    assert rows % NA_ROWS_PER_STEP == 0 and rows >= NA_WIN_ROWS
    nj = rows // NA_ROWS_PER_STEP
    blk = NA_ROWS_PER_STEP * GRID_W
    qc, kc, vc = col0

    def spec(c0, dj):
        return pl.BlockSpec((blk, NA_HEAD_DIM),
                            lambda h, b, j: (b * nj + jnp.clip(j + dj, 0, nj - 1), c0 + h))

    return pl.pallas_call(
        functools.partial(_na_kernel, rows=rows),
        grid=(NA_HEADS, batch, nj),
        in_specs=[spec(qc, 0), spec(kc, -1), spec(kc, 0), spec(kc, 1), spec(vc, -1), spec(vc, 0), spec(vc, 1),
                  pl.BlockSpec((None,) + bias.shape[1:], lambda h, b, j: (h, 0, 0, 0))],
        out_specs=pl.BlockSpec((blk, NA_HEAD_DIM), lambda h, b, j: (b * nj + j, h)),
        out_shape=jax.ShapeDtypeStruct((t, NA_HEADS * NA_HEAD_DIM), BF16),
        scratch_shapes=[pltpu.VMEM((3 * blk, NA_HEAD_DIM), BF16), pltpu.VMEM((3 * blk, NA_HEAD_DIM), BF16)],
        compiler_params=_params("parallel", "parallel", "parallel"),
        name="na_attention",
    )(zb, zb, zb, zb, zb, zb, zb, bias)


def _na_bias_table(rpb):
    col = np.arange(GRID_W)
    col_start = np.clip(col - NA_WIN_COLS // 2, 0, GRID_W - NA_WIN_COLS)
    kc = np.arange(GRID_W)
    valid = (kc[None, :] >= col_start[:, None]) & (kc[None, :] < col_start[:, None] + NA_WIN_COLS)
    by_row = jnp.stack([rpb[:, NA_WIN_ROWS - 1 - t:2 * NA_WIN_ROWS - 1 - t] for t in range(NA_WIN_ROWS)], axis=1)
    pad = GRID_W - NA_WIN_COLS
    padded = jnp.pad(by_row, ((0, 0), (0, 0), (0, 0), (pad, pad)))
    tbl = jnp.stack([padded[..., GRID_W - 1 - c:2 * GRID_W - 1 - c] for c in range(GRID_W)], axis=2)
    tbl = jnp.where(valid[None, None, :, None, :], tbl * LOG2E, MASK_VALUE)
    return tbl.reshape(rpb.shape[0], NA_WIN_ROWS, GRID_W, NA_WIN_ROWS * GRID_W).astype(F32)


def _out_proj_kernel(a_ref, b_ref, c_ref, wa_ref, wb_ref, wc_ref, h_ref, o_ref):
    acc = jnp.dot(a_ref[...], wa_ref[...], preferred_element_type=F32)
    acc = acc + jnp.dot(b_ref[...], wb_ref[...], preferred_element_type=F32)
    acc = acc + jnp.dot(c_ref[...], wc_ref[...], preferred_element_type=F32)
    o_ref[...] = h_ref[...] + acc


def _out_proj(a, b, c, wa, wb, wc, h, tm=1024, tn=512):
    t, d = h.shape
    tm, tn = _tile(t, tm), _tile(d, tn)
    act = lambda x: pl.BlockSpec((tm, x.shape[1]), lambda i, j: (i, 0))
    wgt = lambda w: pl.BlockSpec((w.shape[0], tn), lambda i, j: (0, j))
    return pl.pallas_call(
        _out_proj_kernel,
        grid=(t // tm, d // tn),
        in_specs=[act(a), act(b), act(c), wgt(wa), wgt(wb), wgt(wc),
                  pl.BlockSpec((tm, tn), lambda i, j: (i, j))],
        out_specs=pl.BlockSpec((tm, tn), lambda i, j: (i, j)),
        out_shape=jax.ShapeDtypeStruct((t, d), F32),
        compiler_params=_params("parallel", "arbitrary"),
        name="out_proj",
    )(a, b, c, wa, wb, wc, h)


def _ffn_kernel(h_ref, hp_ref, hn_ref, g_ref, wg_ref, wu_ref, cw_ref, cb_ref, wd_ref, o_ref, xn_ref, *,
                tm, tiles_per_seq):
    i, j = pl.program_id(0), pl.program_id(1)

    @pl.when(j == 0)
    def _():
        x = h_ref[...]
        g = g_ref[...]
        xn_ref[:tm] = _rms(x, g).astype(BF16)
        pos = i % tiles_per_seq
        before = jnp.where(pos == 0, 0.0, _rms(hp_ref[...], g)[BF16_ROWS - 1:BF16_ROWS])
        after = jnp.where(pos == tiles_per_seq - 1, 0.0, _rms(hn_ref[...], g)[0:1])
        rid = lax.broadcasted_iota(jnp.int32, (BF16_ROWS, x.shape[1]), 0)
        halo = jnp.where(rid == 0, before, jnp.where(rid == 1, after, 0.0))
        xn_ref[tm:] = halo.astype(BF16)
        o_ref[...] = x

    ga = jnp.dot(xn_ref[...], wg_ref[...], preferred_element_type=F32)
    up = jnp.dot(xn_ref[:tm], wu_ref[...], preferred_element_type=F32)
    gate = ga[:tm]
    rid = lax.broadcasted_iota(jnp.int32, gate.shape, 0)
    g_prev = jnp.where(rid == 0, ga[tm:tm + 1], pltpu.roll(gate, 1, 0))
    g_next = jnp.where(rid == tm - 1, ga[tm + 1:tm + 2], pltpu.roll(gate, tm - 1, 0))
    cw = cw_ref[...]
    conv = g_prev * cw[0:1] + gate * cw[1:2] + g_next * cw[2:3] + cb_ref[...]
    act = (jax.nn.gelu(conv) * up).astype(BF16)
    o_ref[...] += jnp.dot(act, wd_ref[...], preferred_element_type=F32)


def _ffn(h, g, w_gu, conv_w, conv_b, w_down, seq, tm=512, tf=512):
    t, d = h.shape
    f = w_down.shape[0]
    tm, tf = _tile(seq, tm), _tile(f, tf)
    nf = f // tf
    hb = tm // BF16_ROWS
    nhb = t // BF16_ROWS
    return pl.pallas_call(
        functools.partial(_ffn_kernel, tm=tm, tiles_per_seq=seq // tm),
        grid=(t // tm, nf),
        in_specs=[
            pl.BlockSpec((tm, d), lambda i, j: (i, 0)),
            pl.BlockSpec((BF16_ROWS, d), lambda i, j: (jnp.maximum(i * hb - 1, 0), 0)),
            pl.BlockSpec((BF16_ROWS, d), lambda i, j: (jnp.minimum((i + 1) * hb, nhb - 1), 0)),
            pl.BlockSpec((1, d), lambda i, j: (0, 0)),
            pl.BlockSpec((d, tf), lambda i, j: (0, j)),
            pl.BlockSpec((d, tf), lambda i, j: (0, nf + j)),
            pl.BlockSpec((conv_w.shape[0], tf), lambda i, j: (0, j)),
            pl.BlockSpec((1, tf), lambda i, j: (0, j)),
            pl.BlockSpec((tf, d), lambda i, j: (j, 0)),
        ],
        out_specs=pl.BlockSpec((tm, d), lambda i, j: (i, 0)),
        out_shape=jax.ShapeDtypeStruct((t, d), F32),
        scratch_shapes=[pltpu.VMEM((tm + BF16_ROWS, d), BF16)],
        compiler_params=_params("parallel", "arbitrary"),
        name="ffn",
    )(h, h, h, g, w_gu, w_gu, conv_w, conv_b, w_down)


def _ple_kernel(hrow_ref, h_ref, p_ref, g_ref, wg_ref, wp_ref, o_ref, xn_ref):
    @pl.when(pl.program_id(1) == 0)
    def _():
        xn_ref[...] = _rms(hrow_ref[...], g_ref[...]).astype(BF16)

    gate = jax.nn.sigmoid(jnp.dot(xn_ref[...], wg_ref[...], preferred_element_type=F32))
    emb = jnp.dot(p_ref[...].astype(BF16), wp_ref[...], preferred_element_type=F32)
    o_ref[...] = h_ref[...] + emb * gate


def _ple(h, p, g, w_plg, w_ple, tm=1024, tn=512):
    t, d = h.shape
    tm, tn = _tile(t, tm), _tile(d, tn)
    return pl.pallas_call(
        _ple_kernel,
        grid=(t // tm, d // tn),
        in_specs=[
            pl.BlockSpec((tm, d), lambda i, j: (i, 0)),
            pl.BlockSpec((tm, tn), lambda i, j: (i, j)),
            pl.BlockSpec((tm, p.shape[1]), lambda i, j: (i, 0)),
            pl.BlockSpec((1, d), lambda i, j: (0, 0)),
            pl.BlockSpec((d, tn), lambda i, j: (0, j)),
            pl.BlockSpec((p.shape[1], tn), lambda i, j: (0, j)),
        ],
        out_specs=pl.BlockSpec((tm, tn), lambda i, j: (i, j)),
        out_shape=jax.ShapeDtypeStruct((t, d), F32),
        scratch_shapes=[pltpu.VMEM((tm, d), BF16)],
        compiler_params=_params("parallel", "arbitrary"),
        name="ple",
    )(h, h, p, g, w_plg, w_ple)


def _final_norm_kernel(x_ref, g_ref, o_ref):
    o_ref[...] = _rms(x_ref[...], g_ref[...])


def _final_norm(h, g, tm=1024):
    t, d = h.shape
    tm = _tile(t, tm)
    return pl.pallas_call(
        _final_norm_kernel,
        grid=(t // tm,),
        in_specs=[pl.BlockSpec((tm, d), lambda i: (i, 0)), pl.BlockSpec((1, d), lambda i: (0, 0))],
        out_specs=pl.BlockSpec((tm, d), lambda i: (i, 0)),
        out_shape=jax.ShapeDtypeStruct((t, d), F32),
        compiler_params=_params("parallel"),
        name="final_norm",
    )(h, g)


def _swap_halves(x):
    half = x.shape[-1] // 2
    return jnp.concatenate([x[..., half:], x[..., :half]], axis=-1)


def _pad_lanes(x):
    return jnp.concatenate([x, jnp.zeros(x.shape[:-1] + (LANES - x.shape[-1],), x.dtype)], axis=-1)


def _prepare_layer(i, w_in, g_qa, w_qb, g_kva, w_kvb, rpb, w_out, w_gu, w_down, w_plg, w_ple):
    d = w_in.shape[0]
    q_rank, kv_rank = g_qa.shape[0], g_kva.shape[0]
    o = q_rank + kv_rank
    kpe = w_in[:, o:o + MLA_ROPE]
    w_f = jnp.concatenate([w_in[:, :o], _pad_lanes(kpe), _pad_lanes(_swap_halves(kpe))], axis=1)
    w_b = w_in[:, o + MLA_ROPE:]
    na_w = NA_HEADS * NA_HEAD_DIM
    diff_w = DIFF_HEADS * 2 * DIFF_QK_DIM
    scale_b = np.ones((1, w_b.shape[1]), np.float32)
    scale_b[:, :na_w] = NA_HEAD_DIM ** -0.5 * LOG2E
    scale_b[:, 3 * na_w:3 * na_w + diff_w] = DIFF_QK_DIM ** -0.5 * LOG2E
    wq = w_qb.reshape(q_rank, MLA_HEADS, MLA_NOPE + MLA_ROPE)
    pe = wq[..., MLA_NOPE:]
    wq = jnp.concatenate([wq[..., :MLA_NOPE].reshape(q_rank, -1), _pad_lanes(pe).reshape(q_rank, -1),
                          _pad_lanes(_swap_halves(pe)).reshape(q_rank, -1)], axis=1)
    mla_w = MLA_HEADS * MLA_V
    return dict(
        w_in=jnp.concatenate([w_b, w_f], axis=1).astype(BF16), n_bf16=w_b.shape[1], scale_b=jnp.asarray(scale_b),
        wq=wq.astype(BF16), wkv=w_kvb.astype(BF16),
        na_bias=_na_bias_table(rpb),
        wo_mla=w_out[:mla_w].astype(BF16), wo_na=w_out[mla_w:mla_w + na_w].astype(BF16),
        wo_diff=w_out[mla_w + na_w:].astype(BF16),
        w_gu=w_gu.astype(BF16), w_down=w_down.astype(BF16),
        w_plg=w_plg.astype(BF16), w_ple=w_ple.astype(BF16),
        lam_init=0.8 - 0.6 * math.exp(-0.3 * i),
    )


def _rope_tables(seq):
    half = MLA_ROPE // 2
    freqs = jnp.power(ROPE_THETA, -jnp.arange(half, dtype=F32) / half)
    ang = jnp.arange(seq, dtype=jnp.int32).astype(F32)[:, None] * freqs[None, :]
    cos, sin = jnp.cos(ang), jnp.sin(ang)
    return _pad_lanes(jnp.concatenate([cos, cos], axis=1)), _pad_lanes(jnp.concatenate([-sin, sin], axis=1))


def _trunk(x, p, layers, vecs, g_final):
    batch, seq, d = x.shape
    t = batch * seq
    h = x.reshape(t, d)
    rope_c, rope_s = _rope_tables(seq)
    slopes = jnp.exp2(-8.0 * jnp.arange(1, DIFF_HEADS + 1, dtype=F32) / DIFF_HEADS) * LOG2E
    nb = NA_HEADS * NA_HEAD_DIM // LANES
    for i, (lw, lv) in enumerate(zip(layers, vecs)):
        zb, zf = _in_proj(h, lv["g_attn"], lw["w_in"], lw["scale_b"], lw["n_bf16"])
        q, k, v = _mla_proj(zf, lv["g_qa"], lv["g_kva"], lw["wq"], lw["wkv"], rope_c, rope_s, seq)
        o_mla = _mla_flash(q, k, v, batch, seq)
        o_na = _na_attention(zb, (0, nb, 2 * nb), lw["na_bias"], batch, seq)
        o_diff = _diff_flash(zb, (3 * nb, 4 * nb, 5 * nb), slopes, lv["lam_q1"], lv["lam_k1"], lv["lam_q2"],
                             lv["lam_k2"], lv["g_subln"], lw["lam_init"], batch, seq)
        h = _out_proj(o_mla, o_na, o_diff, lw["wo_mla"], lw["wo_na"], lw["wo_diff"], h)
        h = _ffn(h, lv["g_ffn"], lw["w_gu"], lv["conv_w"], lv["conv_b"], lw["w_down"], seq)
        h = _ple(h, p[i].reshape(t, -1), lv["g_ple"], lw["w_plg"], lw["w_ple"])
    return _final_norm(h, g_final.reshape(1, d)).reshape(batch, seq, d)


def kernel(x_prompt, x_sample, p_prompt, p_sample, g_attn, w_in, g_qa, w_qb, g_kva, w_kvb, rpb, lam_q1, lam_k1,
           lam_q2, lam_k2, g_subln, w_out, g_ffn, w_gu, conv_w, conv_b, w_down, g_ple, w_plg, w_ple, g_final):
    depth = w_in.shape[0]
    layers = [_prepare_layer(i, w_in[i], g_qa[i], w_qb[i], g_kva[i], w_kvb[i], rpb[i], w_out[i], w_gu[i],
                             w_down[i], w_plg[i], w_ple[i]) for i in range(depth)]
    row = lambda a: a.reshape(1, -1)
    vecs = [dict(g_attn=row(g_attn[i]), g_qa=row(g_qa[i]), g_kva=row(g_kva[i]), lam_q1=row(lam_q1[i]),
                 lam_k1=row(lam_k1[i]), lam_q2=row(lam_q2[i]), lam_k2=row(lam_k2[i]), g_subln=row(g_subln[i]),
                 g_ffn=row(g_ffn[i]), conv_w=conv_w[i], conv_b=row(conv_b[i]), g_ple=row(g_ple[i]))
            for i in range(depth)]
    return (_trunk(x_prompt, p_prompt, layers, vecs, g_final), _trunk(x_sample, p_sample, layers, vecs, g_final))
```

```python
import functools
import math

import jax
import jax.numpy as jnp
import numpy as np
from jax import lax
from jax.experimental import pallas as pl
from jax.experimental.pallas import tpu as pltpu

F32 = jnp.float32
BF16 = jnp.bfloat16

GRID_W = 64
MLA_HEADS = 8
MLA_NOPE = 128
MLA_ROPE = 64
MLA_V = 128
MLA_QK_PAD = 256
ROPE_THETA = 10000.0
NA_HEADS = 4
NA_HEAD_DIM = 128
NA_WIN_ROWS = 8
NA_WIN_COLS = 16
NA_ROWS_PER_STEP = 8
NA_GROUP = 4
NA_UNION = NA_WIN_ROWS + NA_GROUP - 1
DIFF_HEADS = 4
DIFF_QK_DIM = 64
DIFF_V_DIM = 128
RMS_EPS = 1e-6
LOG2E = 1.4426950408889634
MASK_VALUE = -1e30
EXP2_UNDERFLOW = 160.0
NORM_BOUND_SLACK = 1.001
LANES = 128
BF16_ROWS = 16
VMEM_LIMIT = 56 * 1024 * 1024


def _tile(n, pref):
    t = min(n, pref)
    assert n % t == 0, (n, pref)
    return t


def _rms(x, g):
    ms = jnp.mean(x * x, axis=-1, keepdims=True)
    return x * lax.rsqrt(ms + RMS_EPS) * g


def _lane_tile(x, n):
    return jnp.concatenate([x] * n, axis=1)


def _params(*sem):
    return pltpu.CompilerParams(dimension_semantics=sem, vmem_limit_bytes=VMEM_LIMIT)


def _in_proj_kernel(x_ref, g_ref, w_ref, cs_ref, zb_ref, zf_ref, xn_ref, *, nb):
    j = pl.program_id(1)

    @pl.when(j == 0)
    def _():
        xn_ref[...] = _rms(x_ref[...], g_ref[...]).astype(BF16)

    acc = jnp.dot(xn_ref[...], w_ref[...], preferred_element_type=F32)

    @pl.when(j < nb)
    def _():
        zb_ref[...] = (acc * cs_ref[...]).astype(BF16)

    @pl.when(j >= nb)
    def _():
        zf_ref[...] = acc


def _in_proj(x, g, w, col_scale, n_bf16, tm=1024, tn=1024):
    t, d = x.shape
    n = w.shape[1]
    tm, tn = _tile(t, tm), _tile(math.gcd(n_bf16, n - n_bf16), tn)
    nb, nf = n_bf16 // tn, (n - n_bf16) // tn
    return pl.pallas_call(
        functools.partial(_in_proj_kernel, nb=nb),
        grid=(t // tm, nb + nf),
        in_specs=[
            pl.BlockSpec((tm, d), lambda i, j: (i, 0)),
            pl.BlockSpec((1, d), lambda i, j: (0, 0)),
            pl.BlockSpec((d, tn), lambda i, j: (0, j)),
            pl.BlockSpec((1, tn), lambda i, j: (0, jnp.minimum(j, nb - 1))),
        ],
        out_specs=[pl.BlockSpec((tm, tn), lambda i, j: (i, jnp.minimum(j, nb - 1))),
                   pl.BlockSpec((tm, tn), lambda i, j: (i, jnp.maximum(j - nb, 0)))],
        out_shape=[jax.ShapeDtypeStruct((t, n_bf16), BF16), jax.ShapeDtypeStruct((t, n - n_bf16), F32)],
        scratch_shapes=[pltpu.VMEM((tm, d), BF16)],
        compiler_params=_params("parallel", "arbitrary"),
        name="in_proj",
    )(x, g, w, col_scale)


def _mla_proj_kernel(zf_ref, gq_ref, gkv_ref, wq_ref, wkv_ref, c_ref, s_ref, q_ref, k_ref, v_ref, *,
                     q_rank, kv_rank, q_scale):
    zf = zf_ref[...]
    cos, sin = c_ref[...], s_ref[...]
    qn = _rms(zf[:, :q_rank], gq_ref[...]).astype(BF16)
    qa = jnp.dot(qn, wq_ref[...], preferred_element_type=F32)
    kvn = _rms(zf[:, q_rank:q_rank + kv_rank], gkv_ref[...]).astype(BF16)
    kva = jnp.dot(kvn, wkv_ref[...], preferred_element_type=F32)
    o = q_rank + kv_rank
    kpe = (zf[:, o:o + LANES] * cos + zf[:, o + LANES:o + 2 * LANES] * sin).astype(BF16)
    hp = MLA_HEADS * LANES
    for h in range(MLA_HEADS):
        lo = h * LANES
        rope = qa[:, hp + lo:hp + lo + LANES] * cos + qa[:, 2 * hp + lo:2 * hp + lo + LANES] * sin
        q_ref[:, 2 * lo:2 * lo + LANES] = (qa[:, lo:lo + LANES] * q_scale).astype(BF16)
        q_ref[:, 2 * lo + LANES:2 * lo + 2 * LANES] = (rope * q_scale).astype(BF16)
        k_ref[:, 2 * lo:2 * lo + LANES] = kva[:, 2 * lo:2 * lo + LANES].astype(BF16)
        k_ref[:, 2 * lo + LANES:2 * lo + 2 * LANES] = kpe
        v_ref[:, lo:lo + LANES] = kva[:, 2 * lo + LANES:2 * lo + 2 * LANES].astype(BF16)


def _mla_proj(zf, gq, gkv, wq, wkv, rope_c, rope_s, seq, tm=512):
    t, zw = zf.shape
    tm = _tile(seq, tm)
    nl = seq // tm
    q_rank, kv_rank = gq.shape[1], gkv.shape[1]
    kern = functools.partial(_mla_proj_kernel, q_rank=q_rank, kv_rank=kv_rank,
                             q_scale=(MLA_NOPE + MLA_ROPE) ** -0.5 * LOG2E)
    qk_w = MLA_HEADS * MLA_QK_PAD
    return pl.pallas_call(
        kern,
        grid=(t // tm,),
        in_specs=[
            pl.BlockSpec((tm, zw), lambda i: (i, 0)),
            pl.BlockSpec((1, q_rank), lambda i: (0, 0)),
            pl.BlockSpec((1, kv_rank), lambda i: (0, 0)),
            pl.BlockSpec(wq.shape, lambda i: (0, 0)),
            pl.BlockSpec(wkv.shape, lambda i: (0, 0)),
            pl.BlockSpec((tm, LANES), lambda i: (i % nl, 0)),
            pl.BlockSpec((tm, LANES), lambda i: (i % nl, 0)),
        ],
        out_specs=[
            pl.BlockSpec((tm, qk_w), lambda i: (i, 0)),
            pl.BlockSpec((tm, qk_w), lambda i: (i, 0)),
            pl.BlockSpec((tm, MLA_HEADS * MLA_V), lambda i: (i, 0)),
        ],
        out_shape=[
            jax.ShapeDtypeStruct((t, qk_w), BF16),
            jax.ShapeDtypeStruct((t, qk_w), BF16),
            jax.ShapeDtypeStruct((t, MLA_HEADS * MLA_V), BF16),
        ],
        compiler_params=_params("parallel"),
        name="mla_proj",
    )(zf, gq, gkv, wq, wkv, rope_c, rope_s)


def _online_softmax_step(s, v, m_ref, acc_ref, shift):
    m_prev = m_ref[...]
    m_cur = jnp.max(s, axis=1, keepdims=True)
    if shift is not None:
        m_cur = m_cur + shift
    m_next = jnp.maximum(m_prev, m_cur)
    alpha = jnp.exp2(m_prev - m_next)
    m_sub = m_next if shift is None else m_next - shift
    p = jnp.exp2(s - _lane_tile(m_sub, s.shape[1] // LANES))
    v1 = jnp.concatenate([v, jnp.ones_like(v)], axis=1)
    acc_ref[...] = _lane_tile(alpha, 2) * acc_ref[...] + jnp.dot(p.astype(BF16), v1, preferred_element_type=F32)
    m_ref[...] = m_next


def _init_softmax_state(m_ref, acc_ref):
    m_ref[...] = jnp.full(m_ref.shape, MASK_VALUE, F32)
    acc_ref[...] = jnp.zeros(acc_ref.shape, F32)


def _softmax_result(acc_ref):
    acc = acc_ref[...]
    return acc[:, :LANES] / acc[:, LANES:]


def _qk(q, k):
    return lax.dot_general(q, k, (((1,), (1,)), ((), ())), preferred_element_type=F32)


def _mla_flash_kernel(q_ref, k_ref, v_ref, o_ref, m_ref, acc_ref, *, tkc):
    ki = pl.program_id(3)

    @pl.when(ki == 0)
    def _():
        _init_softmax_state(m_ref, acc_ref)

    q = q_ref[...]
    for c in range(k_ref.shape[0] // tkc):
        s = _qk(q, k_ref[c * tkc:(c + 1) * tkc, :])
        _online_softmax_step(s, v_ref[c * tkc:(c + 1) * tkc, :], m_ref, acc_ref, None)

    @pl.when(ki == pl.num_programs(3) - 1)
    def _():
        o_ref[...] = _softmax_result(acc_ref).astype(o_ref.dtype)


def _mla_flash(q, k, v, batch, seq, tq=1024, tkb=8192, tkc=512):
    t = q.shape[0]
    tq, tkb = _tile(seq, tq), _tile(seq, tkb)
    tkc = _tile(tkb, tkc)
    nq, nk = seq // tq, seq // tkb
    return pl.pallas_call(
        functools.partial(_mla_flash_kernel, tkc=tkc),
        grid=(MLA_HEADS, batch, nq, nk),
        in_specs=[
            pl.BlockSpec((tq, MLA_QK_PAD), lambda g, b, qi, ki: (b * nq + qi, g)),
            pl.BlockSpec((tkb, MLA_QK_PAD), lambda g, b, qi, ki: (b * nk + ki, g)),
            pl.BlockSpec((tkb, MLA_V), lambda g, b, qi, ki: (b * nk + ki, g)),
        ],
        out_specs=pl.BlockSpec((tq, MLA_V), lambda g, b, qi, ki: (b * nq + qi, g)),
        out_shape=jax.ShapeDtypeStruct((t, MLA_HEADS * MLA_V), BF16),
        scratch_shapes=[pltpu.VMEM((tq, LANES), F32), pltpu.VMEM((tq, MLA_V + LANES), F32)],
        compiler_params=_params("parallel", "parallel", "parallel", "arbitrary"),
        name="mla_flash",
    )(q, k, v)


def _key_norm_kernel(k_ref, o_ref):
    @pl.when(pl.program_id(2) == 0)
    def _():
        o_ref[...] = jnp.zeros(o_ref.shape, F32)

    k = k_ref[...].astype(F32)
    top = jnp.max(jnp.sum(k * k, axis=1, keepdims=True), axis=0, keepdims=True)
    o_ref[...] = jnp.maximum(o_ref[...], top)


def _max_key_norm_sq(zb, kc, batch, seq, tk=2048):
    tk = _tile(seq, tk)
    nk = seq // tk
    return pl.pallas_call(
        _key_norm_kernel,
        grid=(batch, DIFF_HEADS, nk),
        in_specs=[pl.BlockSpec((tk, LANES), lambda b, g, i: (b * nk + i, kc + g))],
        out_specs=pl.BlockSpec((8, LANES), lambda b, g, i: (b * DIFF_HEADS + g, 0)),
        out_shape=jax.ShapeDtypeStruct((batch * DIFF_HEADS * 8, LANES), F32),
        compiler_params=_params("parallel", "parallel", "arbitrary"),
        name="key_norm",
    )(zb)


def _diff_flash_kernel(slope_ref, kn_ref, lq1_ref, lk1_ref, lq2_ref, lk2_ref, gsub_ref, q_ref, k_ref, v_ref, o_ref,
                       qs_ref, m_ref, acc_ref, bias_ref, cap_ref, *, tq, lam_init):
    g, qi, ki = pl.program_id(0), pl.program_id(2), pl.program_id(3)
    slope = slope_ref[g]
    tkb = k_ref.shape[0]
    q0 = qi * tq
    kb0 = ((q0 // tkb + ki) % pl.num_programs(3)) * tkb

    @pl.when(ki == 0)
    def _():
        _init_softmax_state(m_ref, acc_ref)
        q = q_ref[...]
        lane = lax.broadcasted_iota(jnp.int32, q.shape, 1)
        qs_ref[:tq] = jnp.where(lane < DIFF_QK_DIM, q, jnp.zeros_like(q))
        qs_ref[tq:] = jnp.where(lane >= DIFF_QK_DIM, q, jnp.zeros_like(q))
        qf = q.astype(F32)
        q_norm = jnp.sqrt(jnp.sum(qf * qf, axis=1, keepdims=True))
        cap = q_norm * jnp.sqrt(kn_ref[0:1, :]) * NORM_BOUND_SLACK + NORM_BOUND_SLACK
        cap_ref[:tq] = cap
        cap_ref[tq:] = cap
        shape = (2 * tq, tq)
        row = lax.broadcasted_iota(jnp.int32, shape, 0)
        row = jnp.where(row >= tq, row - tq, row)
        d = (row - lax.broadcasted_iota(jnp.int32, shape, 1)).astype(F32)
        bias_ref[0] = -slope * d
        bias_ref[1] = -slope * jnp.abs(d)
        bias_ref[2] = slope * d

    nch = tkb // tq
    best = jnp.max(cap_ref[...] - m_ref[...], axis=0, keepdims=True)
    cidx = lax.broadcasted_iota(jnp.int32, (8, LANES), 0)
    kc0 = kb0 + cidx * tq
    nearest = jnp.maximum(jnp.maximum(q0 - (kc0 + tq) + 1, kc0 - (q0 + tq) + 1), 0)
    live = (cidx < nch) & (best >= slope * nearest.astype(F32) - EXP2_UNDERFLOW)
    n_live = jnp.sum(jnp.where(live, 1, 0), axis=0, keepdims=True)[0, 0]
    first_live = jnp.min(jnp.where(live, cidx, nch), axis=0, keepdims=True)[0, 0]

    def chunk(c, off):
        k0 = kb0 + c * tq
        which = jnp.where(k0 < q0, 0, jnp.where(k0 == q0, 1, 2))
        gap = jnp.full((1, LANES), jnp.abs(q0 - k0), jnp.int32).astype(F32)
        s = _qk(qs_ref[...], k_ref[pl.ds(off, tq), :]) + bias_ref[which]
        _online_softmax_step(s, v_ref[pl.ds(off, tq), :], m_ref, acc_ref, -slope * gap)

    unrolled_from = min(nch, (5 * nch + 7) // 8)

    @pl.when(n_live >= unrolled_from)
    def _():
        for c in range(nch):
            chunk(c, c * tq)

    @pl.when((n_live > 0) & (n_live < unrolled_from))
    def _():
        def body(c, carry):
            chunk(c, pl.multiple_of(c * tq, tq))
            return carry

        lax.fori_loop(first_live, first_live + n_live, body, 0)

    @pl.when(ki == pl.num_programs(3) - 1)
    def _():
        o = _softmax_result(acc_ref)
        lam = (jnp.exp(jnp.sum(lq1_ref[...] * lk1_ref[...], axis=-1, keepdims=True))
               - jnp.exp(jnp.sum(lq2_ref[...] * lk2_ref[...], axis=-1, keepdims=True)) + lam_init)
        d = o[:tq] - lam * o[tq:]
        o_ref[...] = (_rms(d, gsub_ref[...]) * (1.0 - lam_init)).astype(o_ref.dtype)


def _diff_flash(zb, col0, slopes, lq1, lk1, lq2, lk2, gsub, lam_init, batch, seq, tq=512, tkb=4096):
    t = zb.shape[0]
    tq = _tile(seq, tq)
    tkb = _tile(seq, tkb)
    assert tkb % tq == 0 and tkb // tq <= 8
    nq, nk = seq // tq, seq // tkb
    qc, kc, vc = col0
    key_norm = _max_key_norm_sq(zb, kc, batch, seq)
    vec = lambda a: pl.BlockSpec(a.shape, lambda g, b, qi, ki: (0, 0))
    kblock = lambda b, qi, ki: b * nk + (qi * tq // tkb + ki) % nk
    return pl.pallas_call(
        functools.partial(_diff_flash_kernel, tq=tq, lam_init=lam_init),
        grid=(DIFF_HEADS, batch, nq, nk),
        in_specs=[
            pl.BlockSpec(memory_space=pltpu.SMEM),
            pl.BlockSpec((8, LANES), lambda g, b, qi, ki: (b * DIFF_HEADS + g, 0)),
            vec(lq1), vec(lk1), vec(lq2), vec(lk2), vec(gsub),
            pl.BlockSpec((tq, LANES), lambda g, b, qi, ki: (b * nq + qi, qc + g)),
            pl.BlockSpec((tkb, LANES), lambda g, b, qi, ki: (kblock(b, qi, ki), kc + g)),
            pl.BlockSpec((tkb, DIFF_V_DIM), lambda g, b, qi, ki: (kblock(b, qi, ki), vc + g)),
        ],
        out_specs=pl.BlockSpec((tq, DIFF_V_DIM), lambda g, b, qi, ki: (b * nq + qi, g)),
        out_shape=jax.ShapeDtypeStruct((t, DIFF_HEADS * DIFF_V_DIM), BF16),
        scratch_shapes=[pltpu.VMEM((2 * tq, LANES), BF16), pltpu.VMEM((2 * tq, LANES), F32),
                        pltpu.VMEM((2 * tq, DIFF_V_DIM + LANES), F32), pltpu.VMEM((3, 2 * tq, tq), F32),
                        pltpu.VMEM((2 * tq, LANES), F32)],
        compiler_params=_params("parallel", "parallel", "parallel", "arbitrary"),
        name="diff_flash",
    )(slopes, key_norm, lq1, lk1, lq2, lk2, gsub, zb, zb, zb)


def _na_kernel(q_ref, kp_ref, kc_ref, kn_ref, vp_ref, vc_ref, vn_ref, bias_ref, o_ref, kb_ref, vb_ref, *, rows):
    j = pl.program_id(2)
    blk = NA_ROWS_PER_STEP * GRID_W
    for n, (kr, vr) in enumerate(((kp_ref, vp_ref), (kc_ref, vc_ref), (kn_ref, vn_ref))):
        kb_ref[n * blk:(n + 1) * blk] = kr[...]
        vb_ref[n * blk:(n + 1) * blk] = vr[...]
    n_groups = rows // NA_GROUP
    gq = NA_GROUP * GRID_W
    for gi in range(NA_ROWS_PER_STEP // NA_GROUP):
        group = j * (NA_ROWS_PER_STEP // NA_GROUP) + gi
        union_start = jnp.clip(group * NA_GROUP - NA_WIN_ROWS // 2, 0, rows - NA_UNION)
        start = pl.multiple_of((union_start - (j - 1) * NA_ROWS_PER_STEP) * GRID_W, GRID_W)
        variant = jnp.where(group == 0, 0, jnp.where(group == n_groups - 1, 2, 1))
        q = q_ref[gi * gq:(gi + 1) * gq, :]
        s = _qk(q, kb_ref[pl.ds(start, NA_UNION * GRID_W), :]) + bias_ref[variant]
        p = jnp.exp2(s - jnp.max(s, axis=1, keepdims=True))
        v = vb_ref[pl.ds(start, NA_UNION * GRID_W), :]
        acc = jnp.dot(p.astype(BF16), jnp.concatenate([v, jnp.ones_like(v)], axis=1), preferred_element_type=F32)
        o_ref[gi * gq:(gi + 1) * gq, :] = (acc[:, :NA_HEAD_DIM] / acc[:, NA_HEAD_DIM:]).astype(o_ref.dtype)


def _na_attention(zb, col0, bias, batch, seq):
    t = zb.shape[0]
    rows = seq // GRID_W
    assert rows % NA_ROWS_PER_STEP == 0 and rows >= 2 * NA_ROWS_PER_STEP
    nj = rows // NA_ROWS_PER_STEP
    blk = NA_ROWS_PER_STEP * GRID_W
    qc, kc, vc = col0

    def spec(c0, dj):
        return pl.BlockSpec((blk, NA_HEAD_DIM),
                            lambda h, b, j: (b * nj + jnp.clip(j + dj, 0, nj - 1), c0 + h))

    return pl.pallas_call(
        functools.partial(_na_kernel, rows=rows),
        grid=(NA_HEADS, batch, nj),
        in_specs=[spec(qc, 0), spec(kc, -1), spec(kc, 0), spec(kc, 1), spec(vc, -1), spec(vc, 0), spec(vc, 1),
                  pl.BlockSpec((None,) + bias.shape[1:], lambda h, b, j: (h, 0, 0, 0))],
        out_specs=pl.BlockSpec((blk, NA_HEAD_DIM), lambda h, b, j: (b * nj + j, h)),
        out_shape=jax.ShapeDtypeStruct((t, NA_HEADS * NA_HEAD_DIM), BF16),
        scratch_shapes=[pltpu.VMEM((3 * blk, NA_HEAD_DIM), BF16), pltpu.VMEM((3 * blk, NA_HEAD_DIM), BF16)],
        compiler_params=_params("parallel", "parallel", "parallel"),
        name="na_attention",
    )(zb, zb, zb, zb, zb, zb, zb, bias)


def _na_bias_table(rpb):
    col = np.arange(GRID_W)
    col_start = np.clip(col - NA_WIN_COLS // 2, 0, GRID_W - NA_WIN_COLS)
    kc = np.arange(GRID_W)
    valid = (kc[None, :] >= col_start[:, None]) & (kc[None, :] < col_start[:, None] + NA_WIN_COLS)
    by_row = jnp.stack([rpb[:, NA_WIN_ROWS - 1 - t:2 * NA_WIN_ROWS - 1 - t] for t in range(NA_WIN_ROWS)], axis=1)
    pad = GRID_W - NA_WIN_COLS
    padded = jnp.pad(by_row, ((0, 0), (0, 0), (0, 0), (pad, pad)))
    tbl = jnp.stack([padded[..., GRID_W - 1 - c:2 * GRID_W - 1 - c] for c in range(GRID_W)], axis=2)
    tbl = jnp.where(valid[None, None, :, None, :], tbl * LOG2E, MASK_VALUE)
    per_row = tbl.reshape(rpb.shape[0], NA_WIN_ROWS, GRID_W, NA_WIN_ROWS * GRID_W).astype(F32)

    extra = NA_UNION - NA_WIN_ROWS

    def place(t, offset):
        return jnp.pad(per_row[:, t], ((0, 0), (0, 0), (offset * GRID_W, (extra - offset) * GRID_W)),
                       constant_values=MASK_VALUE)

    half = NA_WIN_ROWS // 2
    first = jnp.concatenate([place(i, 0) for i in range(NA_GROUP)], axis=1)
    inner = jnp.concatenate([place(half, i) for i in range(NA_GROUP)], axis=1)
    last = jnp.concatenate([place(half + i, extra) for i in range(NA_GROUP)], axis=1)
    return jnp.stack([first, inner, last], axis=1)


def _out_proj_kernel(a_ref, b_ref, c_ref, wa_ref, wb_ref, wc_ref, h_ref, o_ref):
    acc = jnp.dot(a_ref[...], wa_ref[...], preferred_element_type=F32)
    acc = acc + jnp.dot(b_ref[...], wb_ref[...], preferred_element_type=F32)
    acc = acc + jnp.dot(c_ref[...], wc_ref[...], preferred_element_type=F32)
    o_ref[...] = h_ref[...] + acc


def _out_proj(a, b, c, wa, wb, wc, h, tm=1024, tn=1024):
    t, d = h.shape
    tm, tn = _tile(t, tm), _tile(d, tn)
    act = lambda x: pl.BlockSpec((tm, x.shape[1]), lambda i, j: (i, 0))
    wgt = lambda w: pl.BlockSpec((w.shape[0], tn), lambda i, j: (0, j))
    return pl.pallas_call(
        _out_proj_kernel,
        grid=(t // tm, d // tn),
        in_specs=[act(a), act(b), act(c), wgt(wa), wgt(wb), wgt(wc),
                  pl.BlockSpec((tm, tn), lambda i, j: (i, j))],
        out_specs=pl.BlockSpec((tm, tn), lambda i, j: (i, j)),
        out_shape=jax.ShapeDtypeStruct((t, d), F32),
        compiler_params=_params("parallel", "arbitrary"),
        name="out_proj",
    )(a, b, c, wa, wb, wc, h)


def _ffn_kernel(h_ref, hp_ref, hn_ref, g_ref, wg_ref, wu_ref, cw_ref, cb_ref, wd_ref, o_ref, xn_ref, *,
                tm, tiles_per_seq):
    i, j = pl.program_id(0), pl.program_id(1)

    @pl.when(j == 0)
    def _():
        x = h_ref[...]
        g = g_ref[...]
        xn_ref[:tm] = _rms(x, g).astype(BF16)
        pos = i % tiles_per_seq
        before = jnp.where(pos == 0, 0.0, _rms(hp_ref[...], g)[BF16_ROWS - 1:BF16_ROWS])
        after = jnp.where(pos == tiles_per_seq - 1, 0.0, _rms(hn_ref[...], g)[0:1])
        rid = lax.broadcasted_iota(jnp.int32, (BF16_ROWS, x.shape[1]), 0)
        halo = jnp.where(rid == 0, before, jnp.where(rid == 1, after, 0.0))
        xn_ref[tm:] = halo.astype(BF16)
        o_ref[...] = x

    ga = jnp.dot(xn_ref[...], wg_ref[...], preferred_element_type=F32)
    up = jnp.dot(xn_ref[:tm], wu_ref[...], preferred_element_type=F32)
    gate = ga[:tm]
    rid = lax.broadcasted_iota(jnp.int32, gate.shape, 0)
    g_prev = jnp.where(rid == 0, ga[tm:tm + 1], pltpu.roll(gate, 1, 0))
    g_next = jnp.where(rid == tm - 1, ga[tm + 1:tm + 2], pltpu.roll(gate, tm - 1, 0))
    cw = cw_ref[...]
    conv = g_prev * cw[0:1] + gate * cw[1:2] + g_next * cw[2:3] + cb_ref[...]
    act = (jax.nn.gelu(conv) * up).astype(BF16)
    o_ref[...] += jnp.dot(act, wd_ref[...], preferred_element_type=F32)


def _ffn(h, g, w_gu, conv_w, conv_b, w_down, seq, tm=512, tf=512):
    t, d = h.shape
    f = w_down.shape[0]
    tm, tf = _tile(seq, tm), _tile(f, tf)
    nf = f // tf
    hb = tm // BF16_ROWS
    nhb = t // BF16_ROWS
    return pl.pallas_call(
        functools.partial(_ffn_kernel, tm=tm, tiles_per_seq=seq // tm),
        grid=(t // tm, nf),
        in_specs=[
            pl.BlockSpec((tm, d), lambda i, j: (i, 0)),
            pl.BlockSpec((BF16_ROWS, d), lambda i, j: (jnp.maximum(i * hb - 1, 0), 0)),
            pl.BlockSpec((BF16_ROWS, d), lambda i, j: (jnp.minimum((i + 1) * hb, nhb - 1), 0)),
            pl.BlockSpec((1, d), lambda i, j: (0, 0)),
            pl.BlockSpec((d, tf), lambda i, j: (0, j)),
            pl.BlockSpec((d, tf), lambda i, j: (0, nf + j)),
            pl.BlockSpec((conv_w.shape[0], tf), lambda i, j: (0, j)),
            pl.BlockSpec((1, tf), lambda i, j: (0, j)),
            pl.BlockSpec((tf, d), lambda i, j: (j, 0)),
        ],
        out_specs=pl.BlockSpec((tm, d), lambda i, j: (i, 0)),
        out_shape=jax.ShapeDtypeStruct((t, d), F32),
        scratch_shapes=[pltpu.VMEM((tm + BF16_ROWS, d), BF16)],
        compiler_params=_params("parallel", "arbitrary"),
        name="ffn",
    )(h, h, h, g, w_gu, w_gu, conv_w, conv_b, w_down)


def _ple_kernel(hrow_ref, h_ref, p_ref, g_ref, wg_ref, wp_ref, o_ref, xn_ref):
    @pl.when(pl.program_id(1) == 0)
    def _():
        xn_ref[...] = _rms(hrow_ref[...], g_ref[...]).astype(BF16)

    gate = jax.nn.sigmoid(jnp.dot(xn_ref[...], wg_ref[...], preferred_element_type=F32))
    emb = jnp.dot(p_ref[...].astype(BF16), wp_ref[...], preferred_element_type=F32)
    o_ref[...] = h_ref[...] + emb * gate


def _ple(h, p, g, w_plg, w_ple, tm=1024, tn=1024):
    t, d = h.shape
    tm, tn = _tile(t, tm), _tile(d, tn)
    return pl.pallas_call(
        _ple_kernel,
        grid=(t // tm, d // tn),
        in_specs=[
            pl.BlockSpec((tm, d), lambda i, j: (i, 0)),
            pl.BlockSpec((tm, tn), lambda i, j: (i, j)),
            pl.BlockSpec((tm, p.shape[1]), lambda i, j: (i, 0)),
            pl.BlockSpec((1, d), lambda i, j: (0, 0)),
            pl.BlockSpec((d, tn), lambda i, j: (0, j)),
            pl.BlockSpec((p.shape[1], tn), lambda i, j: (0, j)),
        ],
        out_specs=pl.BlockSpec((tm, tn), lambda i, j: (i, j)),
        out_shape=jax.ShapeDtypeStruct((t, d), F32),
        scratch_shapes=[pltpu.VMEM((tm, d), BF16)],
        compiler_params=_params("parallel", "arbitrary"),
        name="ple",
    )(h, h, p, g, w_plg, w_ple)


def _final_norm_kernel(x_ref, g_ref, o_ref):
    o_ref[...] = _rms(x_ref[...], g_ref[...])


def _final_norm(h, g, tm=1024):
    t, d = h.shape
    tm = _tile(t, tm)
    return pl.pallas_call(
        _final_norm_kernel,
        grid=(t // tm,),
        in_specs=[pl.BlockSpec((tm, d), lambda i: (i, 0)), pl.BlockSpec((1, d), lambda i: (0, 0))],
        out_specs=pl.BlockSpec((tm, d), lambda i: (i, 0)),
        out_shape=jax.ShapeDtypeStruct((t, d), F32),
        compiler_params=_params("parallel"),
        name="final_norm",
    )(h, g)


def _swap_halves(x):
    half = x.shape[-1] // 2
    return jnp.concatenate([x[..., half:], x[..., :half]], axis=-1)


def _pad_lanes(x):
    return jnp.concatenate([x, jnp.zeros(x.shape[:-1] + (LANES - x.shape[-1],), x.dtype)], axis=-1)


def _prepare_layer(i, w_in, g_qa, w_qb, g_kva, w_kvb, rpb, w_out, w_gu, w_down, w_plg, w_ple):
    d = w_in.shape[0]
    q_rank, kv_rank = g_qa.shape[0], g_kva.shape[0]
    o = q_rank + kv_rank
    kpe = w_in[:, o:o + MLA_ROPE]
    w_f = jnp.concatenate([w_in[:, :o], _pad_lanes(kpe), _pad_lanes(_swap_halves(kpe))], axis=1)
    w_b = w_in[:, o + MLA_ROPE:]
    na_w = NA_HEADS * NA_HEAD_DIM
    diff_w = DIFF_HEADS * 2 * DIFF_QK_DIM
    scale_b = np.ones((1, w_b.shape[1]), np.float32)
    scale_b[:, :na_w] = NA_HEAD_DIM ** -0.5 * LOG2E
    scale_b[:, 3 * na_w:3 * na_w + diff_w] = DIFF_QK_DIM ** -0.5 * LOG2E
    wq = w_qb.reshape(q_rank, MLA_HEADS, MLA_NOPE + MLA_ROPE)
    pe = wq[..., MLA_NOPE:]
    wq = jnp.concatenate([wq[..., :MLA_NOPE].reshape(q_rank, -1), _pad_lanes(pe).reshape(q_rank, -1),
                          _pad_lanes(_swap_halves(pe)).reshape(q_rank, -1)], axis=1)
    mla_w = MLA_HEADS * MLA_V
    return dict(
        w_in=jnp.concatenate([w_b, w_f], axis=1).astype(BF16), n_bf16=w_b.shape[1], scale_b=jnp.asarray(scale_b),
        wq=wq.astype(BF16), wkv=w_kvb.astype(BF16),
        na_bias=_na_bias_table(rpb),
        wo_mla=w_out[:mla_w].astype(BF16), wo_na=w_out[mla_w:mla_w + na_w].astype(BF16),
        wo_diff=w_out[mla_w + na_w:].astype(BF16),
        w_gu=w_gu.astype(BF16), w_down=w_down.astype(BF16),
        w_plg=w_plg.astype(BF16), w_ple=w_ple.astype(BF16),
        lam_init=0.8 - 0.6 * math.exp(-0.3 * i),
    )


def _rope_tables(seq):
    half = MLA_ROPE // 2
    freqs = jnp.power(ROPE_THETA, -jnp.arange(half, dtype=F32) / half)
    ang = jnp.arange(seq, dtype=jnp.int32).astype(F32)[:, None] * freqs[None, :]
    cos, sin = jnp.cos(ang), jnp.sin(ang)
    return _pad_lanes(jnp.concatenate([cos, cos], axis=1)), _pad_lanes(jnp.concatenate([-sin, sin], axis=1))


def _trunk(x, p, layers, vecs, g_final):
    batch, seq, d = x.shape
    t = batch * seq
    h = x.reshape(t, d)
    rope_c, rope_s = _rope_tables(seq)
    slopes = jnp.exp2(-8.0 * jnp.arange(1, DIFF_HEADS + 1, dtype=F32) / DIFF_HEADS) * LOG2E
    nb = NA_HEADS * NA_HEAD_DIM // LANES
    for i, (lw, lv) in enumerate(zip(layers, vecs)):
        zb, zf = _in_proj(h, lv["g_attn"], lw["w_in"], lw["scale_b"], lw["n_bf16"])
        q, k, v = _mla_proj(zf, lv["g_qa"], lv["g_kva"], lw["wq"], lw["wkv"], rope_c, rope_s, seq)
        o_mla = _mla_flash(q, k, v, batch, seq)
        o_na = _na_attention(zb, (0, nb, 2 * nb), lw["na_bias"], batch, seq)
        o_diff = _diff_flash(zb, (3 * nb, 4 * nb, 5 * nb), slopes, lv["lam_q1"], lv["lam_k1"], lv["lam_q2"],
                             lv["lam_k2"], lv["g_subln"], lw["lam_init"], batch, seq)
        h = _out_proj(o_mla, o_na, o_diff, lw["wo_mla"], lw["wo_na"], lw["wo_diff"], h)
        h = _ffn(h, lv["g_ffn"], lw["w_gu"], lv["conv_w"], lv["conv_b"], lw["w_down"], seq)
        h = _ple(h, p[i].reshape(t, -1), lv["g_ple"], lw["w_plg"], lw["w_ple"])
    return _final_norm(h, g_final.reshape(1, d)).reshape(batch, seq, d)


def kernel(x_prompt, x_sample, p_prompt, p_sample, g_attn, w_in, g_qa, w_qb, g_kva, w_kvb, rpb, lam_q1, lam_k1,
           lam_q2, lam_k2, g_subln, w_out, g_ffn, w_gu, conv_w, conv_b, w_down, g_ple, w_plg, w_ple, g_final):
    depth = w_in.shape[0]
    layers = [_prepare_layer(i, w_in[i], g_qa[i], w_qb[i], g_kva[i], w_kvb[i], rpb[i], w_out[i], w_gu[i],
                             w_down[i], w_plg[i], w_ple[i]) for i in range(depth)]
    row = lambda a: a.reshape(1, -1)
    vecs = [dict(g_attn=row(g_attn[i]), g_qa=row(g_qa[i]), g_kva=row(g_kva[i]), lam_q1=row(lam_q1[i]),
                 lam_k1=row(lam_k1[i]), lam_q2=row(lam_q2[i]), lam_k2=row(lam_k2[i]), g_subln=row(g_subln[i]),
                 g_ffn=row(g_ffn[i]), conv_w=conv_w[i], conv_b=row(conv_b[i]), g_ple=row(g_ple[i]))
            for i in range(depth)]
    return (_trunk(x_prompt, p_prompt, layers, vecs, g_final), _trunk(x_sample, p_sample, layers, vecs, g_final))
```

```python
import functools
import math

import jax
import jax.numpy as jnp
import numpy as np
from jax import lax
from jax.experimental import pallas as pl
from jax.experimental.pallas import tpu as pltpu

F32 = jnp.float32
BF16 = jnp.bfloat16

GRID_W = 64
MLA_HEADS = 8
MLA_NOPE = 128
MLA_ROPE = 64
MLA_V = 128
MLA_QK_PAD = 256
ROPE_THETA = 10000.0
NA_HEADS = 4
NA_HEAD_DIM = 128
NA_WIN_ROWS = 8
NA_WIN_COLS = 16
NA_ROWS_PER_STEP = 8
NA_GROUP = 4
NA_UNION = NA_WIN_ROWS + NA_GROUP - 1
DIFF_HEADS = 4
DIFF_QK_DIM = 64
DIFF_V_DIM = 128
RMS_EPS = 1e-6
LOG2E = 1.4426950408889634
MASK_VALUE = -1e30
EXP2_UNDERFLOW = 160.0
NORM_BOUND_SLACK = 1.001
LANES = 128
BF16_ROWS = 16
VMEM_LIMIT = 56 * 1024 * 1024


def _tile(n, pref):
    t = min(n, pref)
    assert n % t == 0, (n, pref)
    return t


def _rms(x, g):
    ms = jnp.mean(x * x, axis=-1, keepdims=True)
    return x * lax.rsqrt(ms + RMS_EPS) * g


def _lane_tile(x, n):
    return jnp.concatenate([x] * n, axis=1)


def _params(*sem):
    return pltpu.CompilerParams(dimension_semantics=sem, vmem_limit_bytes=VMEM_LIMIT)


def _in_proj_kernel(x_ref, g_ref, w_ref, cs_ref, zb_ref, zf_ref, xn_ref, *, nb):
    j = pl.program_id(1)

    @pl.when(j == 0)
    def _():
        xn_ref[...] = _rms(x_ref[...], g_ref[...]).astype(BF16)

    acc = jnp.dot(xn_ref[...], w_ref[...], preferred_element_type=F32)

    @pl.when(j < nb)
    def _():
        zb_ref[...] = (acc * cs_ref[...]).astype(BF16)

    @pl.when(j >= nb)
    def _():
        zf_ref[...] = acc


def _in_proj(x, g, w, col_scale, n_bf16, tm=1024, tn=1024):
    t, d = x.shape
    n = w.shape[1]
    tm, tn = _tile(t, tm), _tile(math.gcd(n_bf16, n - n_bf16), tn)
    nb, nf = n_bf16 // tn, (n - n_bf16) // tn
    return pl.pallas_call(
        functools.partial(_in_proj_kernel, nb=nb),
        grid=(t // tm, nb + nf),
        in_specs=[
            pl.BlockSpec((tm, d), lambda i, j: (i, 0)),
            pl.BlockSpec((1, d), lambda i, j: (0, 0)),
            pl.BlockSpec((d, tn), lambda i, j: (0, j)),
            pl.BlockSpec((1, tn), lambda i, j: (0, jnp.minimum(j, nb - 1))),
        ],
        out_specs=[pl.BlockSpec((tm, tn), lambda i, j: (i, jnp.minimum(j, nb - 1))),
                   pl.BlockSpec((tm, tn), lambda i, j: (i, jnp.maximum(j - nb, 0)))],
        out_shape=[jax.ShapeDtypeStruct((t, n_bf16), BF16), jax.ShapeDtypeStruct((t, n - n_bf16), F32)],
        scratch_shapes=[pltpu.VMEM((tm, d), BF16)],
        compiler_params=_params("parallel", "arbitrary"),
        name="in_proj",
    )(x, g, w, col_scale)


def _mla_proj_kernel(zf_ref, gq_ref, gkv_ref, wq_ref, wkv_ref, c_ref, s_ref, q_ref, k_ref, v_ref, *,
                     q_rank, kv_rank, q_scale):
    zf = zf_ref[...]
    cos, sin = c_ref[...], s_ref[...]
    qn = _rms(zf[:, :q_rank], gq_ref[...]).astype(BF16)
    qa = jnp.dot(qn, wq_ref[...], preferred_element_type=F32)
    kvn = _rms(zf[:, q_rank:q_rank + kv_rank], gkv_ref[...]).astype(BF16)
    kva = jnp.dot(kvn, wkv_ref[...], preferred_element_type=F32)
    o = q_rank + kv_rank
    kpe = (zf[:, o:o + LANES] * cos + zf[:, o + LANES:o + 2 * LANES] * sin).astype(BF16)
    hp = MLA_HEADS * LANES
    for h in range(MLA_HEADS):
        lo = h * LANES
        rope = qa[:, hp + lo:hp + lo + LANES] * cos + qa[:, 2 * hp + lo:2 * hp + lo + LANES] * sin
        q_ref[:, 2 * lo:2 * lo + LANES] = (qa[:, lo:lo + LANES] * q_scale).astype(BF16)
        q_ref[:, 2 * lo + LANES:2 * lo + 2 * LANES] = (rope * q_scale).astype(BF16)
        k_ref[:, 2 * lo:2 * lo + LANES] = kva[:, 2 * lo:2 * lo + LANES].astype(BF16)
        k_ref[:, 2 * lo + LANES:2 * lo + 2 * LANES] = kpe
        v_ref[:, lo:lo + LANES] = kva[:, 2 * lo + LANES:2 * lo + 2 * LANES].astype(BF16)


def _mla_proj(zf, gq, gkv, wq, wkv, rope_c, rope_s, seq, tm=512):
    t, zw = zf.shape
    tm = _tile(seq, tm)
    nl = seq // tm
    q_rank, kv_rank = gq.shape[1], gkv.shape[1]
    kern = functools.partial(_mla_proj_kernel, q_rank=q_rank, kv_rank=kv_rank,
                             q_scale=(MLA_NOPE + MLA_ROPE) ** -0.5 * LOG2E)
    qk_w = MLA_HEADS * MLA_QK_PAD
    return pl.pallas_call(
        kern,
        grid=(t // tm,),
        in_specs=[
            pl.BlockSpec((tm, zw), lambda i: (i, 0)),
            pl.BlockSpec((1, q_rank), lambda i: (0, 0)),
            pl.BlockSpec((1, kv_rank), lambda i: (0, 0)),
            pl.BlockSpec(wq.shape, lambda i: (0, 0)),
            pl.BlockSpec(wkv.shape, lambda i: (0, 0)),
            pl.BlockSpec((tm, LANES), lambda i: (i % nl, 0)),
            pl.BlockSpec((tm, LANES), lambda i: (i % nl, 0)),
        ],
        out_specs=[
            pl.BlockSpec((tm, qk_w), lambda i: (i, 0)),
            pl.BlockSpec((tm, qk_w), lambda i: (i, 0)),
            pl.BlockSpec((tm, MLA_HEADS * MLA_V), lambda i: (i, 0)),
        ],
        out_shape=[
            jax.ShapeDtypeStruct((t, qk_w), BF16),
            jax.ShapeDtypeStruct((t, qk_w), BF16),
            jax.ShapeDtypeStruct((t, MLA_HEADS * MLA_V), BF16),
        ],
        compiler_params=_params("parallel"),
        name="mla_proj",
    )(zf, gq, gkv, wq, wkv, rope_c, rope_s)


def _online_softmax_step(s, v, m_ref, acc_ref, shift):
    m_prev = m_ref[...]
    m_cur = jnp.max(s, axis=1, keepdims=True)
    if shift is not None:
        m_cur = m_cur + shift
    m_next = jnp.maximum(m_prev, m_cur)
    alpha = jnp.exp2(m_prev - m_next)
    m_sub = m_next if shift is None else m_next - shift
    p = jnp.exp2(s - _lane_tile(m_sub, s.shape[1] // LANES))
    v1 = jnp.concatenate([v, jnp.ones_like(v)], axis=1)
    acc_ref[...] = _lane_tile(alpha, 2) * acc_ref[...] + jnp.dot(p.astype(BF16), v1, preferred_element_type=F32)
    m_ref[...] = m_next


def _init_softmax_state(m_ref, acc_ref):
    m_ref[...] = jnp.full(m_ref.shape, MASK_VALUE, F32)
    acc_ref[...] = jnp.zeros(acc_ref.shape, F32)


def _softmax_result(acc_ref):
    acc = acc_ref[...]
    return acc[:, :LANES] / acc[:, LANES:]


def _qk(q, k):
    return lax.dot_general(q, k, (((1,), (1,)), ((), ())), preferred_element_type=F32)


def _mla_flash_kernel(q_ref, k_ref, v_ref, o_ref, m_ref, acc_ref, *, tkc, unroll):
    _init_softmax_state(m_ref, acc_ref)
    q = q_ref[...]

    def block(bi, carry):
        for c in range(unroll):
            off = pl.multiple_of((bi * unroll + c) * tkc, tkc)
            s = _qk(q, k_ref[pl.ds(off, tkc), :])
            _online_softmax_step(s, v_ref[pl.ds(off, tkc), :], m_ref, acc_ref, None)
        return carry

    lax.fori_loop(0, k_ref.shape[0] // (tkc * unroll), block, 0)
    o_ref[...] = _softmax_result(acc_ref).astype(o_ref.dtype)


def _mla_flash(q, k, v, batch, seq, tq=1024, tkc=512, unroll=16):
    t = q.shape[0]
    tq, tkc = _tile(seq, tq), _tile(seq, tkc)
    unroll = _tile(seq // tkc, unroll)
    nq = seq // tq
    return pl.pallas_call(
        functools.partial(_mla_flash_kernel, tkc=tkc, unroll=unroll),
        grid=(MLA_HEADS, batch, nq),
        in_specs=[
            pl.BlockSpec((tq, MLA_QK_PAD), lambda g, b, qi: (b * nq + qi, g)),
            pl.BlockSpec((seq, MLA_QK_PAD), lambda g, b, qi: (b, g)),
            pl.BlockSpec((seq, MLA_V), lambda g, b, qi: (b, g)),
        ],
        out_specs=pl.BlockSpec((tq, MLA_V), lambda g, b, qi: (b * nq + qi, g)),
        out_shape=jax.ShapeDtypeStruct((t, MLA_HEADS * MLA_V), BF16),
        scratch_shapes=[pltpu.VMEM((tq, LANES), F32), pltpu.VMEM((tq, MLA_V + LANES), F32)],
        compiler_params=_params("parallel", "parallel", "parallel"),
        name="mla_flash",
    )(q, k, v)


def _key_norm_kernel(k_ref, o_ref):
    @pl.when(pl.program_id(2) == 0)
    def _():
        o_ref[...] = jnp.zeros(o_ref.shape, F32)

    k = k_ref[...].astype(F32)
    top = jnp.max(jnp.sum(k * k, axis=1, keepdims=True), axis=0, keepdims=True)
    o_ref[...] = jnp.maximum(o_ref[...], top)


def _max_key_norm_sq(zb, kc, batch, seq, tk=2048):
    tk = _tile(seq, tk)
    nk = seq // tk
    return pl.pallas_call(
        _key_norm_kernel,
        grid=(batch, DIFF_HEADS, nk),
        in_specs=[pl.BlockSpec((tk, LANES), lambda b, g, i: (b * nk + i, kc + g))],
        out_specs=pl.BlockSpec((8, LANES), lambda b, g, i: (b * DIFF_HEADS + g, 0)),
        out_shape=jax.ShapeDtypeStruct((batch * DIFF_HEADS * 8, LANES), F32),
        compiler_params=_params("parallel", "parallel", "arbitrary"),
        name="key_norm",
    )(zb)


def _diff_flash_kernel(slope_ref, kn_ref, lq1_ref, lk1_ref, lq2_ref, lk2_ref, gsub_ref, q_ref, k_ref, v_ref, o_ref,
                       qs_ref, m_ref, acc_ref, bias_ref, cap_ref, *, tq, tkb, lam_init):
    g, qi = pl.program_id(0), pl.program_id(2)
    slope = slope_ref[g]
    nk = k_ref.shape[0] // tkb
    nch = tkb // tq
    q0 = qi * tq

    @pl.when(qi == 0)
    def _():
        shape = (2 * tq, tq)
        row = lax.broadcasted_iota(jnp.int32, shape, 0)
        row = jnp.where(row >= tq, row - tq, row)
        d = (row - lax.broadcasted_iota(jnp.int32, shape, 1)).astype(F32)
        bias_ref[0] = -slope * d
        bias_ref[1] = -slope * jnp.abs(d)
        bias_ref[2] = slope * d

    _init_softmax_state(m_ref, acc_ref)
    q = q_ref[...]
    lane = lax.broadcasted_iota(jnp.int32, q.shape, 1)
    qs_ref[:tq] = jnp.where(lane < DIFF_QK_DIM, q, jnp.zeros_like(q))
    qs_ref[tq:] = jnp.where(lane >= DIFF_QK_DIM, q, jnp.zeros_like(q))
    qf = q.astype(F32)
    q_norm = jnp.sqrt(jnp.sum(qf * qf, axis=1, keepdims=True))
    cap = q_norm * jnp.sqrt(kn_ref[0:1, :]) * NORM_BOUND_SLACK + NORM_BOUND_SLACK
    cap_ref[:tq] = cap
    cap_ref[tq:] = cap

    unrolled_from = min(nch, (5 * nch + 7) // 8)

    def block(ki, carry):
        kb0 = ((q0 // tkb + ki) % nk) * tkb
        best = jnp.max(cap_ref[...] - m_ref[...], axis=0, keepdims=True)
        cidx = lax.broadcasted_iota(jnp.int32, (8, LANES), 0)
        kc0 = kb0 + cidx * tq
        nearest = jnp.maximum(jnp.maximum(q0 - (kc0 + tq) + 1, kc0 - (q0 + tq) + 1), 0)
        live = (cidx < nch) & (best >= slope * nearest.astype(F32) - EXP2_UNDERFLOW)
        n_live = jnp.sum(jnp.where(live, 1, 0), axis=0, keepdims=True)[0, 0]
        first_live = jnp.min(jnp.where(live, cidx, nch), axis=0, keepdims=True)[0, 0]

        def chunk(c):
            k0 = pl.multiple_of(kb0 + c * tq, tq)
            which = jnp.where(k0 < q0, 0, jnp.where(k0 == q0, 1, 2))
            gap = jnp.full((1, LANES), jnp.abs(q0 - k0), jnp.int32).astype(F32)
            s = _qk(qs_ref[...], k_ref[pl.ds(k0, tq), :]) + bias_ref[which]
            _online_softmax_step(s, v_ref[pl.ds(k0, tq), :], m_ref, acc_ref, -slope * gap)

        @pl.when(n_live >= unrolled_from)
        def _():
            for c in range(nch):
                chunk(c)

        @pl.when((n_live > 0) & (n_live < unrolled_from))
        def _():
            def body(c, inner):
                chunk(c)
                return inner

            lax.fori_loop(first_live, first_live + n_live, body, 0)

        return carry

    lax.fori_loop(0, nk, block, 0)

    o = _softmax_result(acc_ref)
    lam = (jnp.exp(jnp.sum(lq1_ref[...] * lk1_ref[...], axis=-1, keepdims=True))
           - jnp.exp(jnp.sum(lq2_ref[...] * lk2_ref[...], axis=-1, keepdims=True)) + lam_init)
    d = o[:tq] - lam * o[tq:]
    o_ref[...] = (_rms(d, gsub_ref[...]) * (1.0 - lam_init)).astype(o_ref.dtype)


def _diff_flash(zb, col0, slopes, lq1, lk1, lq2, lk2, gsub, lam_init, batch, seq, tq=512, tkb=4096):
    t = zb.shape[0]
    tq = _tile(seq, tq)
    tkb = _tile(seq, tkb)
    assert tkb % tq == 0 and tkb // tq <= 8
    nq = seq // tq
    qc, kc, vc = col0
    key_norm = _max_key_norm_sq(zb, kc, batch, seq)
    vec = lambda a: pl.BlockSpec(a.shape, lambda g, b, qi: (0, 0))
    return pl.pallas_call(
        functools.partial(_diff_flash_kernel, tq=tq, tkb=tkb, lam_init=lam_init),
        grid=(DIFF_HEADS, batch, nq),
        in_specs=[
            pl.BlockSpec(memory_space=pltpu.SMEM),
            pl.BlockSpec((8, LANES), lambda g, b, qi: (b * DIFF_HEADS + g, 0)),
            vec(lq1), vec(lk1), vec(lq2), vec(lk2), vec(gsub),
            pl.BlockSpec((tq, LANES), lambda g, b, qi: (b * nq + qi, qc + g)),
            pl.BlockSpec((seq, LANES), lambda g, b, qi: (b, kc + g)),
            pl.BlockSpec((seq, DIFF_V_DIM), lambda g, b, qi: (b, vc + g)),
        ],
        out_specs=pl.BlockSpec((tq, DIFF_V_DIM), lambda g, b, qi: (b * nq + qi, g)),
        out_shape=jax.ShapeDtypeStruct((t, DIFF_HEADS * DIFF_V_DIM), BF16),
        scratch_shapes=[pltpu.VMEM((2 * tq, LANES), BF16), pltpu.VMEM((2 * tq, LANES), F32),
                        pltpu.VMEM((2 * tq, DIFF_V_DIM + LANES), F32), pltpu.VMEM((3, 2 * tq, tq), F32),
                        pltpu.VMEM((2 * tq, LANES), F32)],
        compiler_params=_params("parallel", "parallel", "arbitrary"),
        name="diff_flash",
    )(slopes, key_norm, lq1, lk1, lq2, lk2, gsub, zb, zb, zb)


def _na_kernel(q_ref, kp_ref, kc_ref, kn_ref, vp_ref, vc_ref, vn_ref, bias_ref, o_ref, kb_ref, vb_ref, *, rows):
    j = pl.program_id(2)
    blk = NA_ROWS_PER_STEP * GRID_W
    for n, (kr, vr) in enumerate(((kp_ref, vp_ref), (kc_ref, vc_ref), (kn_ref, vn_ref))):
        kb_ref[n * blk:(n + 1) * blk] = kr[...]
        vb_ref[n * blk:(n + 1) * blk] = vr[...]
    n_groups = rows // NA_GROUP
    gq = NA_GROUP * GRID_W
    for gi in range(NA_ROWS_PER_STEP // NA_GROUP):
        group = j * (NA_ROWS_PER_STEP // NA_GROUP) + gi
        union_start = jnp.clip(group * NA_GROUP - NA_WIN_ROWS // 2, 0, rows - NA_UNION)
        start = pl.multiple_of((union_start - (j - 1) * NA_ROWS_PER_STEP) * GRID_W, GRID_W)
        variant = jnp.where(group == 0, 0, jnp.where(group == n_groups - 1, 2, 1))
        q = q_ref[gi * gq:(gi + 1) * gq, :]
        s = _qk(q, kb_ref[pl.ds(start, NA_UNION * GRID_W), :]) + bias_ref[variant]
        p = jnp.exp2(s - jnp.max(s, axis=1, keepdims=True))
        v = vb_ref[pl.ds(start, NA_UNION * GRID_W), :]
        acc = jnp.dot(p.astype(BF16), jnp.concatenate([v, jnp.ones_like(v)], axis=1), preferred_element_type=F32)
        o_ref[gi * gq:(gi + 1) * gq, :] = (acc[:, :NA_HEAD_DIM] / acc[:, NA_HEAD_DIM:]).astype(o_ref.dtype)


def _na_attention(zb, col0, bias, batch, seq):
    t = zb.shape[0]
    rows = seq // GRID_W
    assert rows % NA_ROWS_PER_STEP == 0 and rows >= 2 * NA_ROWS_PER_STEP
    nj = rows // NA_ROWS_PER_STEP
    blk = NA_ROWS_PER_STEP * GRID_W
    qc, kc, vc = col0

    def spec(c0, dj):
        return pl.BlockSpec((blk, NA_HEAD_DIM),
                            lambda h, b, j: (b * nj + jnp.clip(j + dj, 0, nj - 1), c0 + h))

    return pl.pallas_call(
        functools.partial(_na_kernel, rows=rows),
        grid=(NA_HEADS, batch, nj),
        in_specs=[spec(qc, 0), spec(kc, -1), spec(kc, 0), spec(kc, 1), spec(vc, -1), spec(vc, 0), spec(vc, 1),
                  pl.BlockSpec((None,) + bias.shape[1:], lambda h, b, j: (h, 0, 0, 0))],
        out_specs=pl.BlockSpec((blk, NA_HEAD_DIM), lambda h, b, j: (b * nj + j, h)),
        out_shape=jax.ShapeDtypeStruct((t, NA_HEADS * NA_HEAD_DIM), BF16),
        scratch_shapes=[pltpu.VMEM((3 * blk, NA_HEAD_DIM), BF16), pltpu.VMEM((3 * blk, NA_HEAD_DIM), BF16)],
        compiler_params=_params("parallel", "parallel", "parallel"),
        name="na_attention",
    )(zb, zb, zb, zb, zb, zb, zb, bias)


def _na_bias_table(rpb):
    col = np.arange(GRID_W)
    col_start = np.clip(col - NA_WIN_COLS // 2, 0, GRID_W - NA_WIN_COLS)
    kc = np.arange(GRID_W)
    valid = (kc[None, :] >= col_start[:, None]) & (kc[None, :] < col_start[:, None] + NA_WIN_COLS)
    by_row = jnp.stack([rpb[:, NA_WIN_ROWS - 1 - t:2 * NA_WIN_ROWS - 1 - t] for t in range(NA_WIN_ROWS)], axis=1)
    pad = GRID_W - NA_WIN_COLS
    padded = jnp.pad(by_row, ((0, 0), (0, 0), (0, 0), (pad, pad)))
    tbl = jnp.stack([padded[..., GRID_W - 1 - c:2 * GRID_W - 1 - c] for c in range(GRID_W)], axis=2)
    tbl = jnp.where(valid[None, None, :, None, :], tbl * LOG2E, MASK_VALUE)
    per_row = tbl.reshape(rpb.shape[0], NA_WIN_ROWS, GRID_W, NA_WIN_ROWS * GRID_W).astype(F32)

    extra = NA_UNION - NA_WIN_ROWS

    def place(t, offset):
        return jnp.pad(per_row[:, t], ((0, 0), (0, 0), (offset * GRID_W, (extra - offset) * GRID_W)),
                       constant_values=MASK_VALUE)

    half = NA_WIN_ROWS // 2
    first = jnp.concatenate([place(i, 0) for i in range(NA_GROUP)], axis=1)
    inner = jnp.concatenate([place(half, i) for i in range(NA_GROUP)], axis=1)
    last = jnp.concatenate([place(half + i, extra) for i in range(NA_GROUP)], axis=1)
    return jnp.stack([first, inner, last], axis=1)


def _out_proj_kernel(a_ref, b_ref, c_ref, wa_ref, wb_ref, wc_ref, h_ref, o_ref):
    acc = jnp.dot(a_ref[...], wa_ref[...], preferred_element_type=F32)
    acc = acc + jnp.dot(b_ref[...], wb_ref[...], preferred_element_type=F32)
    acc = acc + jnp.dot(c_ref[...], wc_ref[...], preferred_element_type=F32)
    o_ref[...] = h_ref[...] + acc


def _out_proj(a, b, c, wa, wb, wc, h, tm=1024, tn=1024):
    t, d = h.shape
    tm, tn = _tile(t, tm), _tile(d, tn)
    act = lambda x: pl.BlockSpec((tm, x.shape[1]), lambda i, j: (i, 0))
    wgt = lambda w: pl.BlockSpec((w.shape[0], tn), lambda i, j: (0, j))
    return pl.pallas_call(
        _out_proj_kernel,
        grid=(t // tm, d // tn),
        in_specs=[act(a), act(b), act(c), wgt(wa), wgt(wb), wgt(wc),
                  pl.BlockSpec((tm, tn), lambda i, j: (i, j))],
        out_specs=pl.BlockSpec((tm, tn), lambda i, j: (i, j)),
        out_shape=jax.ShapeDtypeStruct((t, d), F32),
        compiler_params=_params("parallel", "arbitrary"),
        name="out_proj",
    )(a, b, c, wa, wb, wc, h)


def _ffn_kernel(h_ref, hp_ref, hn_ref, g_ref, wg_ref, wu_ref, cw_ref, cb_ref, wd_ref, o_ref, xn_ref, *,
                tm, tiles_per_seq):
    i, j = pl.program_id(0), pl.program_id(1)

    @pl.when(j == 0)
    def _():
        x = h_ref[...]
        g = g_ref[...]
        xn_ref[:tm] = _rms(x, g).astype(BF16)
        pos = i % tiles_per_seq
        before = jnp.where(pos == 0, 0.0, _rms(hp_ref[...], g)[BF16_ROWS - 1:BF16_ROWS])
        after = jnp.where(pos == tiles_per_seq - 1, 0.0, _rms(hn_ref[...], g)[0:1])
        rid = lax.broadcasted_iota(jnp.int32, (BF16_ROWS, x.shape[1]), 0)
        halo = jnp.where(rid == 0, before, jnp.where(rid == 1, after, 0.0))
        xn_ref[tm:] = halo.astype(BF16)
        o_ref[...] = x

    ga = jnp.dot(xn_ref[...], wg_ref[...], preferred_element_type=F32)
    up = jnp.dot(xn_ref[:tm], wu_ref[...], preferred_element_type=F32)
    gate = ga[:tm]
    rid = lax.broadcasted_iota(jnp.int32, gate.shape, 0)
    g_prev = jnp.where(rid == 0, ga[tm:tm + 1], pltpu.roll(gate, 1, 0))
    g_next = jnp.where(rid == tm - 1, ga[tm + 1:tm + 2], pltpu.roll(gate, tm - 1, 0))
    cw = cw_ref[...]
    conv = g_prev * cw[0:1] + gate * cw[1:2] + g_next * cw[2:3] + cb_ref[...]
    act = (jax.nn.gelu(conv) * up).astype(BF16)
    o_ref[...] += jnp.dot(act, wd_ref[...], preferred_element_type=F32)


def _ffn(h, g, w_gu, conv_w, conv_b, w_down, seq, tm=512, tf=512):
    t, d = h.shape
    f = w_down.shape[0]
    tm, tf = _tile(seq, tm), _tile(f, tf)
    nf = f // tf
    hb = tm // BF16_ROWS
    nhb = t // BF16_ROWS
    return pl.pallas_call(
        functools.partial(_ffn_kernel, tm=tm, tiles_per_seq=seq // tm),
        grid=(t // tm, nf),
        in_specs=[
            pl.BlockSpec((tm, d), lambda i, j: (i, 0)),
            pl.BlockSpec((BF16_ROWS, d), lambda i, j: (jnp.maximum(i * hb - 1, 0), 0)),
            pl.BlockSpec((BF16_ROWS, d), lambda i, j: (jnp.minimum((i + 1) * hb, nhb - 1), 0)),
            pl.BlockSpec((1, d), lambda i, j: (0, 0)),
            pl.BlockSpec((d, tf), lambda i, j: (0, j)),
            pl.BlockSpec((d, tf), lambda i, j: (0, nf + j)),
            pl.BlockSpec((conv_w.shape[0], tf), lambda i, j: (0, j)),
            pl.BlockSpec((1, tf), lambda i, j: (0, j)),
            pl.BlockSpec((tf, d), lambda i, j: (j, 0)),
        ],
        out_specs=pl.BlockSpec((tm, d), lambda i, j: (i, 0)),
        out_shape=jax.ShapeDtypeStruct((t, d), F32),
        scratch_shapes=[pltpu.VMEM((tm + BF16_ROWS, d), BF16)],
        compiler_params=_params("parallel", "arbitrary"),
        name="ffn",
    )(h, h, h, g, w_gu, w_gu, conv_w, conv_b, w_down)


def _ple_kernel(h_ref, p_ref, g_ref, wg_ref, wp_ref, gf_ref, o_ref, *, tn, final_norm):
    h = h_ref[...]
    xn = _rms(h, g_ref[...]).astype(BF16)
    pb = p_ref[...].astype(BF16)
    for j in range(h.shape[1] // tn):
        cols = slice(j * tn, (j + 1) * tn)
        gate = jax.nn.sigmoid(jnp.dot(xn, wg_ref[:, cols], preferred_element_type=F32))
        emb = jnp.dot(pb, wp_ref[:, cols], preferred_element_type=F32)
        o_ref[:, cols] = h[:, cols] + emb * gate
    if final_norm:
        o_ref[...] = _rms(o_ref[...], gf_ref[...])


def _ple(h, p, g, w_plg, w_ple, g_final, final_norm, tm=512, tn=512):
    t, d = h.shape
    tm, tn = _tile(t, tm), _tile(d, tn)
    whole = lambda a: pl.BlockSpec(a.shape, lambda i: (0, 0))
    return pl.pallas_call(
        functools.partial(_ple_kernel, tn=tn, final_norm=final_norm),
        grid=(t // tm,),
        in_specs=[
            pl.BlockSpec((tm, d), lambda i: (i, 0)),
            pl.BlockSpec((tm, p.shape[1]), lambda i: (i, 0)),
            whole(g), whole(w_plg), whole(w_ple), whole(g_final),
        ],
        out_specs=pl.BlockSpec((tm, d), lambda i: (i, 0)),
        out_shape=jax.ShapeDtypeStruct((t, d), F32),
        compiler_params=_params("parallel"),
        name="ple",
    )(h, p, g, w_plg, w_ple, g_final)


def _swap_halves(x):
    half = x.shape[-1] // 2
    return jnp.concatenate([x[..., half:], x[..., :half]], axis=-1)


def _pad_lanes(x):
    return jnp.concatenate([x, jnp.zeros(x.shape[:-1] + (LANES - x.shape[-1],), x.dtype)], axis=-1)


def _prepare_layer(i, w_in, g_qa, w_qb, g_kva, w_kvb, rpb, w_out, w_gu, w_down, w_plg, w_ple):
    d = w_in.shape[0]
    q_rank, kv_rank = g_qa.shape[0], g_kva.shape[0]
    o = q_rank + kv_rank
    kpe = w_in[:, o:o + MLA_ROPE]
    w_f = jnp.concatenate([w_in[:, :o], _pad_lanes(kpe), _pad_lanes(_swap_halves(kpe))], axis=1)
    w_b = w_in[:, o + MLA_ROPE:]
    na_w = NA_HEADS * NA_HEAD_DIM
    diff_w = DIFF_HEADS * 2 * DIFF_QK_DIM
    scale_b = np.ones((1, w_b.shape[1]), np.float32)
    scale_b[:, :na_w] = NA_HEAD_DIM ** -0.5 * LOG2E
    scale_b[:, 3 * na_w:3 * na_w + diff_w] = DIFF_QK_DIM ** -0.5 * LOG2E
    wq = w_qb.reshape(q_rank, MLA_HEADS, MLA_NOPE + MLA_ROPE)
    pe = wq[..., MLA_NOPE:]
    wq = jnp.concatenate([wq[..., :MLA_NOPE].reshape(q_rank, -1), _pad_lanes(pe).reshape(q_rank, -1),
                          _pad_lanes(_swap_halves(pe)).reshape(q_rank, -1)], axis=1)
    mla_w = MLA_HEADS * MLA_V
    return dict(
        w_in=jnp.concatenate([w_b, w_f], axis=1).astype(BF16), n_bf16=w_b.shape[1], scale_b=jnp.asarray(scale_b),
        wq=wq.astype(BF16), wkv=w_kvb.astype(BF16),
        na_bias=_na_bias_table(rpb),
        wo_mla=w_out[:mla_w].astype(BF16), wo_na=w_out[mla_w:mla_w + na_w].astype(BF16),
        wo_diff=w_out[mla_w + na_w:].astype(BF16),
        w_gu=w_gu.astype(BF16), w_down=w_down.astype(BF16),
        w_plg=w_plg.astype(BF16), w_ple=w_ple.astype(BF16),
        lam_init=0.8 - 0.6 * math.exp(-0.3 * i),
    )


def _rope_tables(seq):
    half = MLA_ROPE // 2
    freqs = jnp.power(ROPE_THETA, -jnp.arange(half, dtype=F32) / half)
    ang = jnp.arange(seq, dtype=jnp.int32).astype(F32)[:, None] * freqs[None, :]
    cos, sin = jnp.cos(ang), jnp.sin(ang)
    return _pad_lanes(jnp.concatenate([cos, cos], axis=1)), _pad_lanes(jnp.concatenate([-sin, sin], axis=1))


def _trunk(x, p, layers, vecs, g_final):
    batch, seq, d = x.shape
    t = batch * seq
    h = x.reshape(t, d)
    rope_c, rope_s = _rope_tables(seq)
    slopes = jnp.exp2(-8.0 * jnp.arange(1, DIFF_HEADS + 1, dtype=F32) / DIFF_HEADS) * LOG2E
    nb = NA_HEADS * NA_HEAD_DIM // LANES
    for i, (lw, lv) in enumerate(zip(layers, vecs)):
        zb, zf = _in_proj(h, lv["g_attn"], lw["w_in"], lw["scale_b"], lw["n_bf16"])
        q, k, v = _mla_proj(zf, lv["g_qa"], lv["g_kva"], lw["wq"], lw["wkv"], rope_c, rope_s, seq)
        o_mla = _mla_flash(q, k, v, batch, seq)
        o_na = _na_attention(zb, (0, nb, 2 * nb), lw["na_bias"], batch, seq)
        o_diff = _diff_flash(zb, (3 * nb, 4 * nb, 5 * nb), slopes, lv["lam_q1"], lv["lam_k1"], lv["lam_q2"],
                             lv["lam_k2"], lv["g_subln"], lw["lam_init"], batch, seq)
        h = _out_proj(o_mla, o_na, o_diff, lw["wo_mla"], lw["wo_na"], lw["wo_diff"], h)
        h = _ffn(h, lv["g_ffn"], lw["w_gu"], lv["conv_w"], lv["conv_b"], lw["w_down"], seq)
        h = _ple(h, p[i].reshape(t, -1), lv["g_ple"], lw["w_plg"], lw["w_ple"], g_final.reshape(1, d),
                 final_norm=i == len(layers) - 1)
    return h.reshape(batch, seq, d)


def kernel(x_prompt, x_sample, p_prompt, p_sample, g_attn, w_in, g_qa, w_qb, g_kva, w_kvb, rpb, lam_q1, lam_k1,
           lam_q2, lam_k2, g_subln, w_out, g_ffn, w_gu, conv_w, conv_b, w_down, g_ple, w_plg, w_ple, g_final):
    depth = w_in.shape[0]
    layers = [_prepare_layer(i, w_in[i], g_qa[i], w_qb[i], g_kva[i], w_kvb[i], rpb[i], w_out[i], w_gu[i],
                             w_down[i], w_plg[i], w_ple[i]) for i in range(depth)]
    row = lambda a: a.reshape(1, -1)
    vecs = [dict(g_attn=row(g_attn[i]), g_qa=row(g_qa[i]), g_kva=row(g_kva[i]), lam_q1=row(lam_q1[i]),
                 lam_k1=row(lam_k1[i]), lam_q2=row(lam_q2[i]), lam_k2=row(lam_k2[i]), g_subln=row(g_subln[i]),
                 g_ffn=row(g_ffn[i]), conv_w=conv_w[i], conv_b=row(conv_b[i]), g_ple=row(g_ple[i]))
            for i in range(depth)]
    return (_trunk(x_prompt, p_prompt, layers, vecs, g_final), _trunk(x_sample, p_sample, layers, vecs, g_final))
```

```python
import functools
import math

import jax
import jax.numpy as jnp
import numpy as np
from jax import lax
from jax.experimental import pallas as pl
from jax.experimental.pallas import tpu as pltpu

F32 = jnp.float32
BF16 = jnp.bfloat16

GRID_W = 64
MLA_HEADS = 8
MLA_NOPE = 128
MLA_ROPE = 64
MLA_V = 128
MLA_QK_PAD = 256
ROPE_THETA = 10000.0
NA_HEADS = 4
NA_HEAD_DIM = 128
NA_WIN_ROWS = 8
NA_WIN_COLS = 16
NA_ROWS_PER_STEP = 8
NA_GROUP = 4
NA_UNION = NA_WIN_ROWS + NA_GROUP - 1
DIFF_HEADS = 4
DIFF_QK_DIM = 64
DIFF_V_DIM = 128
RMS_EPS = 1e-6
LOG2E = 1.4426950408889634
MASK_VALUE = -1e30
EXP2_UNDERFLOW = 160.0
NORM_BOUND_SLACK = 1.001
LANES = 128
BF16_ROWS = 16
VMEM_LIMIT = 56 * 1024 * 1024


def _tile(n, pref):
    t = min(n, pref)
    assert n % t == 0, (n, pref)
    return t


def _rms(x, g):
    ms = jnp.mean(x * x, axis=-1, keepdims=True)
    return x * lax.rsqrt(ms + RMS_EPS) * g


def _lane_tile(x, n):
    return jnp.concatenate([x] * n, axis=1)


def _params(*sem):
    return pltpu.CompilerParams(dimension_semantics=sem, vmem_limit_bytes=VMEM_LIMIT)


def _in_proj_kernel(x_ref, g_ref, w_ref, cs_ref, zb_ref, zf_ref, xn_ref, *, nb):
    j = pl.program_id(1)

    @pl.when(j == 0)
    def _():
        xn_ref[...] = _rms(x_ref[...], g_ref[...]).astype(BF16)

    acc = jnp.dot(xn_ref[...], w_ref[...], preferred_element_type=F32)

    @pl.when(j < nb)
    def _():
        zb_ref[...] = (acc * cs_ref[...]).astype(BF16)

    @pl.when(j >= nb)
    def _():
        zf_ref[...] = acc


def _in_proj(x, g, w, col_scale, n_bf16, tm=1024, tn=1024):
    t, d = x.shape
    n = w.shape[1]
    tm, tn = _tile(t, tm), _tile(math.gcd(n_bf16, n - n_bf16), tn)
    nb, nf = n_bf16 // tn, (n - n_bf16) // tn
    return pl.pallas_call(
        functools.partial(_in_proj_kernel, nb=nb),
        grid=(t // tm, nb + nf),
        in_specs=[
            pl.BlockSpec((tm, d), lambda i, j: (i, 0)),
            pl.BlockSpec((1, d), lambda i, j: (0, 0)),
            pl.BlockSpec((d, tn), lambda i, j: (0, j)),
            pl.BlockSpec((1, tn), lambda i, j: (0, jnp.minimum(j, nb - 1))),
        ],
        out_specs=[pl.BlockSpec((tm, tn), lambda i, j: (i, jnp.minimum(j, nb - 1))),
                   pl.BlockSpec((tm, tn), lambda i, j: (i, jnp.maximum(j - nb, 0)))],
        out_shape=[jax.ShapeDtypeStruct((t, n_bf16), BF16), jax.ShapeDtypeStruct((t, n - n_bf16), F32)],
        scratch_shapes=[pltpu.VMEM((tm, d), BF16)],
        compiler_params=_params("parallel", "arbitrary"),
        name="in_proj",
    )(x, g, w, col_scale)


def _mla_proj_kernel(zf_ref, gq_ref, gkv_ref, wq_ref, wkv_ref, c_ref, s_ref, q_ref, k_ref, v_ref, *,
                     q_rank, kv_rank, q_scale):
    zf = zf_ref[...]
    cos, sin = c_ref[...], s_ref[...]
    qn = _rms(zf[:, :q_rank], gq_ref[...]).astype(BF16)
    qa = jnp.dot(qn, wq_ref[...], preferred_element_type=F32)
    kvn = _rms(zf[:, q_rank:q_rank + kv_rank], gkv_ref[...]).astype(BF16)
    kva = jnp.dot(kvn, wkv_ref[...], preferred_element_type=F32)
    o = q_rank + kv_rank
    kpe = (zf[:, o:o + LANES] * cos + zf[:, o + LANES:o + 2 * LANES] * sin).astype(BF16)
    hp = MLA_HEADS * LANES
    for h in range(MLA_HEADS):
        lo = h * LANES
        rope = qa[:, hp + lo:hp + lo + LANES] * cos + qa[:, 2 * hp + lo:2 * hp + lo + LANES] * sin
        q_ref[:, 2 * lo:2 * lo + LANES] = (qa[:, lo:lo + LANES] * q_scale).astype(BF16)
        q_ref[:, 2 * lo + LANES:2 * lo + 2 * LANES] = (rope * q_scale).astype(BF16)
        k_ref[:, 2 * lo:2 * lo + LANES] = kva[:, 2 * lo:2 * lo + LANES].astype(BF16)
        k_ref[:, 2 * lo + LANES:2 * lo + 2 * LANES] = kpe
        v_ref[:, lo:lo + LANES] = kva[:, 2 * lo + LANES:2 * lo + 2 * LANES].astype(BF16)


def _mla_proj(zf, gq, gkv, wq, wkv, rope_c, rope_s, seq, tm=512):
    t, zw = zf.shape
    tm = _tile(seq, tm)
    nl = seq // tm
    q_rank, kv_rank = gq.shape[1], gkv.shape[1]
    kern = functools.partial(_mla_proj_kernel, q_rank=q_rank, kv_rank=kv_rank,
                             q_scale=(MLA_NOPE + MLA_ROPE) ** -0.5 * LOG2E)
    qk_w = MLA_HEADS * MLA_QK_PAD
    return pl.pallas_call(
        kern,
        grid=(t // tm,),
        in_specs=[
            pl.BlockSpec((tm, zw), lambda i: (i, 0)),
            pl.BlockSpec((1, q_rank), lambda i: (0, 0)),
            pl.BlockSpec((1, kv_rank), lambda i: (0, 0)),
            pl.BlockSpec(wq.shape, lambda i: (0, 0)),
            pl.BlockSpec(wkv.shape, lambda i: (0, 0)),
            pl.BlockSpec((tm, LANES), lambda i: (i % nl, 0)),
            pl.BlockSpec((tm, LANES), lambda i: (i % nl, 0)),
        ],
        out_specs=[
            pl.BlockSpec((tm, qk_w), lambda i: (i, 0)),
            pl.BlockSpec((tm, qk_w), lambda i: (i, 0)),
            pl.BlockSpec((tm, MLA_HEADS * MLA_V), lambda i: (i, 0)),
        ],
        out_shape=[
            jax.ShapeDtypeStruct((t, qk_w), BF16),
            jax.ShapeDtypeStruct((t, qk_w), BF16),
            jax.ShapeDtypeStruct((t, MLA_HEADS * MLA_V), BF16),
        ],
        compiler_params=_params("parallel"),
        name="mla_proj",
    )(zf, gq, gkv, wq, wkv, rope_c, rope_s)


def _online_softmax_step(s, v, m_ref, acc_ref, shift):
    m_prev = m_ref[...]
    m_cur = jnp.max(s, axis=1, keepdims=True)
    if shift is not None:
        m_cur = m_cur + shift
    m_next = jnp.maximum(m_prev, m_cur)
    alpha = jnp.exp2(m_prev - m_next)
    m_sub = m_next if shift is None else m_next - shift
    p = jnp.exp2(s - _lane_tile(m_sub, s.shape[1] // LANES))
    v1 = jnp.concatenate([v, jnp.ones_like(v)], axis=1)
    acc_ref[...] = _lane_tile(alpha, 2) * acc_ref[...] + jnp.dot(p.astype(BF16), v1, preferred_element_type=F32)
    m_ref[...] = m_next


def _init_softmax_state(m_ref, acc_ref):
    m_ref[...] = jnp.full(m_ref.shape, MASK_VALUE, F32)
    acc_ref[...] = jnp.zeros(acc_ref.shape, F32)


def _softmax_result(acc_ref):
    acc = acc_ref[...]
    return acc[:, :LANES] / acc[:, LANES:]


def _qk(q, k):
    return lax.dot_general(q, k, (((1,), (1,)), ((), ())), preferred_element_type=F32)


def _mla_flash_kernel(q_ref, k_ref, v_ref, o_ref, m_ref, acc_ref, *, tkc, unroll):
    _init_softmax_state(m_ref, acc_ref)
    q = q_ref[...]

    def block(bi, carry):
        for c in range(unroll):
            off = pl.multiple_of((bi * unroll + c) * tkc, tkc)
            s = _qk(q, k_ref[pl.ds(off, tkc), :])
            _online_softmax_step(s, v_ref[pl.ds(off, tkc), :], m_ref, acc_ref, None)
        return carry

    lax.fori_loop(0, k_ref.shape[0] // (tkc * unroll), block, 0)
    o_ref[...] = _softmax_result(acc_ref).astype(o_ref.dtype)


def _mla_flash(q, k, v, batch, seq, tq=1024, tkc=512, unroll=16):
    t = q.shape[0]
    tq, tkc = _tile(seq, tq), _tile(seq, tkc)
    unroll = _tile(seq // tkc, unroll)
    nq = seq // tq
    return pl.pallas_call(
        functools.partial(_mla_flash_kernel, tkc=tkc, unroll=unroll),
        grid=(MLA_HEADS, batch, nq),
        in_specs=[
            pl.BlockSpec((tq, MLA_QK_PAD), lambda g, b, qi: (b * nq + qi, g)),
            pl.BlockSpec((seq, MLA_QK_PAD), lambda g, b, qi: (b, g)),
            pl.BlockSpec((seq, MLA_V), lambda g, b, qi: (b, g)),
        ],
        out_specs=pl.BlockSpec((tq, MLA_V), lambda g, b, qi: (b * nq + qi, g)),
        out_shape=jax.ShapeDtypeStruct((t, MLA_HEADS * MLA_V), BF16),
        scratch_shapes=[pltpu.VMEM((tq, LANES), F32), pltpu.VMEM((tq, MLA_V + LANES), F32)],
        compiler_params=_params("parallel", "parallel", "parallel"),
        name="mla_flash",
    )(q, k, v)


def _key_norm_kernel(k_ref, o_ref):
    @pl.when(pl.program_id(2) == 0)
    def _():
        o_ref[...] = jnp.zeros(o_ref.shape, F32)

    k = k_ref[...].astype(F32)
    top = jnp.max(jnp.sum(k * k, axis=1, keepdims=True), axis=0, keepdims=True)
    o_ref[...] = jnp.maximum(o_ref[...], top)


def _max_key_norm_sq(zb, kc, batch, seq, tk=2048):
    tk = _tile(seq, tk)
    nk = seq // tk
    return pl.pallas_call(
        _key_norm_kernel,
        grid=(batch, DIFF_HEADS, nk),
        in_specs=[pl.BlockSpec((tk, LANES), lambda b, g, i: (b * nk + i, kc + g))],
        out_specs=pl.BlockSpec((8, LANES), lambda b, g, i: (b * DIFF_HEADS + g, 0)),
        out_shape=jax.ShapeDtypeStruct((batch * DIFF_HEADS * 8, LANES), F32),
        compiler_params=_params("parallel", "parallel", "arbitrary"),
        name="key_norm",
    )(zb)


def _diff_flash_kernel(slope_ref, kn_ref, lq1_ref, lk1_ref, lq2_ref, lk2_ref, gsub_ref, q_ref, k_ref, v_ref, o_ref,
                       qs_ref, m_ref, acc_ref, bias_ref, cap_ref, qx_ref, kx_ref, *, tq, tkb, lam_init):
    g, qi = pl.program_id(0), pl.program_id(2)
    slope = slope_ref[g]
    nk = k_ref.shape[0] // tkb
    nch = tkb // tq
    q0 = qi * tq

    @pl.when(qi == 0)
    def _():
        shape = (2 * tq, tq)
        row = lax.broadcasted_iota(jnp.int32, shape, 0)
        row = jnp.where(row >= tq, row - tq, row)
        d = (row - lax.broadcasted_iota(jnp.int32, shape, 1)).astype(F32)
        bias_ref[0] = -slope * d
        bias_ref[1] = -slope * jnp.abs(d)
        bias_ref[2] = slope * d
        whole = jnp.full((1, LANES), slope, F32)
        p1 = whole.astype(BF16).astype(F32)
        p2 = (whole - p1).astype(BF16).astype(F32)
        p3 = (whole - p1 - p2).astype(BF16).astype(F32)
        lane = lax.broadcasted_iota(jnp.int32, (tq, LANES), 1)
        pos = lax.broadcasted_iota(jnp.int32, (tq, LANES), 0)
        even = (lane & 1) == 0
        digit = jnp.where(even, pos >> 4, pos & 15).astype(F32)
        pair = jnp.where(lane >= 6, lane - 6, lane) >> 1
        piece = jnp.where(pair == 0, p1, jnp.where(pair == 1, p2, p3))
        weight = jnp.where(even, 16.0 * piece, piece)
        kx_ref[...] = jnp.where(lane < 6, weight, jnp.where(lane < 12, digit, 0.0)).astype(BF16)
        after = jnp.where(lane < 6, digit, jnp.where(lane < 12, -weight, 0.0)).astype(BF16)
        for half in range(2):
            qx_ref[0, half * tq:(half + 1) * tq, LANES:] = -after
            qx_ref[1, half * tq:(half + 1) * tq, LANES:] = after

    _init_softmax_state(m_ref, acc_ref)
    q = q_ref[...]
    lane = lax.broadcasted_iota(jnp.int32, q.shape, 1)
    qs_ref[:tq] = jnp.where(lane < DIFF_QK_DIM, q, jnp.zeros_like(q))
    qs_ref[tq:] = jnp.where(lane >= DIFF_QK_DIM, q, jnp.zeros_like(q))
    for side in range(2):
        qx_ref[side, :, :LANES] = qs_ref[...]
    qf = q.astype(F32)
    q_norm = jnp.sqrt(jnp.sum(qf * qf, axis=1, keepdims=True))
    cap = q_norm * jnp.sqrt(kn_ref[0:1, :]) * NORM_BOUND_SLACK + NORM_BOUND_SLACK
    cap_ref[:tq] = cap
    cap_ref[tq:] = cap

    unrolled_from = min(nch, (5 * nch + 7) // 8)

    def block(ki, far):
        kb0 = ((q0 // tkb + ki) % nk) * tkb
        best = jnp.max(cap_ref[...] - m_ref[...], axis=0, keepdims=True)
        cidx = lax.broadcasted_iota(jnp.int32, (8, LANES), 0)
        kc0 = kb0 + cidx * tq
        nearest = jnp.maximum(jnp.maximum(q0 - (kc0 + tq) + 1, kc0 - (q0 + tq) + 1), 0)
        live = (cidx < nch) & (best >= slope * nearest.astype(F32) - EXP2_UNDERFLOW)
        n_live = jnp.sum(jnp.where(live, 1, 0), axis=0, keepdims=True)[0, 0]
        first_live = jnp.min(jnp.where(live, cidx, nch), axis=0, keepdims=True)[0, 0]

        def chunk(c):
            k0 = pl.multiple_of(kb0 + c * tq, tq)
            gap = jnp.full((1, LANES), jnp.abs(q0 - k0), jnp.int32).astype(F32)
            if far:
                keys = jnp.concatenate([k_ref[pl.ds(k0, tq), :], kx_ref[...]], axis=1)
                s = _qk(qx_ref[jnp.where(k0 < q0, 0, 1)], keys)
            else:
                which = jnp.where(k0 < q0, 0, jnp.where(k0 == q0, 1, 2))
                s = _qk(qs_ref[...], k_ref[pl.ds(k0, tq), :]) + bias_ref[which]
            _online_softmax_step(s, v_ref[pl.ds(k0, tq), :], m_ref, acc_ref, -slope * gap)

        @pl.when(n_live >= unrolled_from)
        def _():
            for c in range(nch):
                chunk(c)

        @pl.when((n_live > 0) & (n_live < unrolled_from))
        def _():
            def body(c, inner):
                chunk(c)
                return inner

            lax.fori_loop(first_live, first_live + n_live, body, 0)

    block(0, far=False)

    def far_block(ki, carry):
        block(ki, far=True)
        return carry

    lax.fori_loop(1, nk, far_block, 0)

    o = _softmax_result(acc_ref)
    lam = (jnp.exp(jnp.sum(lq1_ref[...] * lk1_ref[...], axis=-1, keepdims=True))
           - jnp.exp(jnp.sum(lq2_ref[...] * lk2_ref[...], axis=-1, keepdims=True)) + lam_init)
    d = o[:tq] - lam * o[tq:]
    o_ref[...] = (_rms(d, gsub_ref[...]) * (1.0 - lam_init)).astype(o_ref.dtype)


def _diff_flash(zb, col0, slopes, lq1, lk1, lq2, lk2, gsub, lam_init, batch, seq, tq=512, tkb=4096):
    t = zb.shape[0]
    tq = _tile(seq, tq)
    tkb = _tile(seq, tkb)
    assert tkb % tq == 0 and tkb // tq <= 8
    nq = seq // tq
    qc, kc, vc = col0
    key_norm = _max_key_norm_sq(zb, kc, batch, seq)
    vec = lambda a: pl.BlockSpec(a.shape, lambda g, b, qi: (0, 0))
    return pl.pallas_call(
        functools.partial(_diff_flash_kernel, tq=tq, tkb=tkb, lam_init=lam_init),
        grid=(DIFF_HEADS, batch, nq),
        in_specs=[
            pl.BlockSpec(memory_space=pltpu.SMEM),
            pl.BlockSpec((8, LANES), lambda g, b, qi: (b * DIFF_HEADS + g, 0)),
            vec(lq1), vec(lk1), vec(lq2), vec(lk2), vec(gsub),
            pl.BlockSpec((tq, LANES), lambda g, b, qi: (b * nq + qi, qc + g)),
            pl.BlockSpec((seq, LANES), lambda g, b, qi: (b, kc + g)),
            pl.BlockSpec((seq, DIFF_V_DIM), lambda g, b, qi: (b, vc + g)),
        ],
        out_specs=pl.BlockSpec((tq, DIFF_V_DIM), lambda g, b, qi: (b * nq + qi, g)),
        out_shape=jax.ShapeDtypeStruct((t, DIFF_HEADS * DIFF_V_DIM), BF16),
        scratch_shapes=[pltpu.VMEM((2 * tq, LANES), BF16), pltpu.VMEM((2 * tq, LANES), F32),
                        pltpu.VMEM((2 * tq, DIFF_V_DIM + LANES), F32), pltpu.VMEM((3, 2 * tq, tq), F32),
                        pltpu.VMEM((2 * tq, LANES), F32), pltpu.VMEM((2, 2 * tq, 2 * LANES), BF16),
                        pltpu.VMEM((tq, LANES), BF16)],
        compiler_params=_params("parallel", "parallel", "arbitrary"),
        name="diff_flash",
    )(slopes, key_norm, lq1, lk1, lq2, lk2, gsub, zb, zb, zb)


def _na_kernel(q_ref, kp_ref, kc_ref, kn_ref, vp_ref, vc_ref, vn_ref, bias_ref, o_ref, kb_ref, vb_ref, *, rows):
    j = pl.program_id(2)
    blk = NA_ROWS_PER_STEP * GRID_W
    for n, (kr, vr) in enumerate(((kp_ref, vp_ref), (kc_ref, vc_ref), (kn_ref, vn_ref))):
        kb_ref[n * blk:(n + 1) * blk] = kr[...]
        vb_ref[n * blk:(n + 1) * blk] = vr[...]
    n_groups = rows // NA_GROUP
    gq = NA_GROUP * GRID_W
    for gi in range(NA_ROWS_PER_STEP // NA_GROUP):
        group = j * (NA_ROWS_PER_STEP // NA_GROUP) + gi
        union_start = jnp.clip(group * NA_GROUP - NA_WIN_ROWS // 2, 0, rows - NA_UNION)
        start = pl.multiple_of((union_start - (j - 1) * NA_ROWS_PER_STEP) * GRID_W, GRID_W)
        variant = jnp.where(group == 0, 0, jnp.where(group == n_groups - 1, 2, 1))
        q = q_ref[gi * gq:(gi + 1) * gq, :]
        s = _qk(q, kb_ref[pl.ds(start, NA_UNION * GRID_W), :]) + bias_ref[variant]
        p = jnp.exp2(s - jnp.max(s, axis=1, keepdims=True))
        v = vb_ref[pl.ds(start, NA_UNION * GRID_W), :]
        acc = jnp.dot(p.astype(BF16), jnp.concatenate([v, jnp.ones_like(v)], axis=1), preferred_element_type=F32)
        o_ref[gi * gq:(gi + 1) * gq, :] = (acc[:, :NA_HEAD_DIM] / acc[:, NA_HEAD_DIM:]).astype(o_ref.dtype)


def _na_attention(zb, col0, bias, batch, seq):
    t = zb.shape[0]
    rows = seq // GRID_W
    assert rows % NA_ROWS_PER_STEP == 0 and rows >= 2 * NA_ROWS_PER_STEP
    nj = rows // NA_ROWS_PER_STEP
    blk = NA_ROWS_PER_STEP * GRID_W
    qc, kc, vc = col0

    def spec(c0, dj):
        return pl.BlockSpec((blk, NA_HEAD_DIM),
                            lambda h, b, j: (b * nj + jnp.clip(j + dj, 0, nj - 1), c0 + h))

    return pl.pallas_call(
        functools.partial(_na_kernel, rows=rows),
        grid=(NA_HEADS, batch, nj),
        in_specs=[spec(qc, 0), spec(kc, -1), spec(kc, 0), spec(kc, 1), spec(vc, -1), spec(vc, 0), spec(vc, 1),
                  pl.BlockSpec((None,) + bias.shape[1:], lambda h, b, j: (h, 0, 0, 0))],
        out_specs=pl.BlockSpec((blk, NA_HEAD_DIM), lambda h, b, j: (b * nj + j, h)),
        out_shape=jax.ShapeDtypeStruct((t, NA_HEADS * NA_HEAD_DIM), BF16),
        scratch_shapes=[pltpu.VMEM((3 * blk, NA_HEAD_DIM), BF16), pltpu.VMEM((3 * blk, NA_HEAD_DIM), BF16)],
        compiler_params=_params("parallel", "parallel", "parallel"),
        name="na_attention",
    )(zb, zb, zb, zb, zb, zb, zb, bias)


def _na_bias_table(rpb):
    col = np.arange(GRID_W)
    col_start = np.clip(col - NA_WIN_COLS // 2, 0, GRID_W - NA_WIN_COLS)
    kc = np.arange(GRID_W)
    valid = (kc[None, :] >= col_start[:, None]) & (kc[None, :] < col_start[:, None] + NA_WIN_COLS)
    by_row = jnp.stack([rpb[:, NA_WIN_ROWS - 1 - t:2 * NA_WIN_ROWS - 1 - t] for t in range(NA_WIN_ROWS)], axis=1)
    pad = GRID_W - NA_WIN_COLS
    padded = jnp.pad(by_row, ((0, 0), (0, 0), (0, 0), (pad, pad)))
    tbl = jnp.stack([padded[..., GRID_W - 1 - c:2 * GRID_W - 1 - c] for c in range(GRID_W)], axis=2)
    tbl = jnp.where(valid[None, None, :, None, :], tbl * LOG2E, MASK_VALUE)
    per_row = tbl.reshape(rpb.shape[0], NA_WIN_ROWS, GRID_W, NA_WIN_ROWS * GRID_W).astype(F32)

    extra = NA_UNION - NA_WIN_ROWS

    def place(t, offset):
        return jnp.pad(per_row[:, t], ((0, 0), (0, 0), (offset * GRID_W, (extra - offset) * GRID_W)),
                       constant_values=MASK_VALUE)

    half = NA_WIN_ROWS // 2
    first = jnp.concatenate([place(i, 0) for i in range(NA_GROUP)], axis=1)
    inner = jnp.concatenate([place(half, i) for i in range(NA_GROUP)], axis=1)
    last = jnp.concatenate([place(half + i, extra) for i in range(NA_GROUP)], axis=1)
    return jnp.stack([first, inner, last], axis=1)


def _out_proj_kernel(a_ref, b_ref, c_ref, wa_ref, wb_ref, wc_ref, h_ref, o_ref):
    acc = jnp.dot(a_ref[...], wa_ref[...], preferred_element_type=F32)
    acc = acc + jnp.dot(b_ref[...], wb_ref[...], preferred_element_type=F32)
    acc = acc + jnp.dot(c_ref[...], wc_ref[...], preferred_element_type=F32)
    o_ref[...] = h_ref[...] + acc


def _out_proj(a, b, c, wa, wb, wc, h, tm=1024, tn=1024):
    t, d = h.shape
    tm, tn = _tile(t, tm), _tile(d, tn)
    act = lambda x: pl.BlockSpec((tm, x.shape[1]), lambda i, j: (i, 0))
    wgt = lambda w: pl.BlockSpec((w.shape[0], tn), lambda i, j: (0, j))
    return pl.pallas_call(
        _out_proj_kernel,
        grid=(t // tm, d // tn),
        in_specs=[act(a), act(b), act(c), wgt(wa), wgt(wb), wgt(wc),
                  pl.BlockSpec((tm, tn), lambda i, j: (i, j))],
        out_specs=pl.BlockSpec((tm, tn), lambda i, j: (i, j)),
        out_shape=jax.ShapeDtypeStruct((t, d), F32),
        compiler_params=_params("parallel", "arbitrary"),
        name="out_proj",
    )(a, b, c, wa, wb, wc, h)


def _ffn_kernel(h_ref, hp_ref, hn_ref, g_ref, wg_ref, wu_ref, cw_ref, cb_ref, wd_ref, o_ref, xn_ref, *,
                tm, tiles_per_seq):
    i, j = pl.program_id(0), pl.program_id(1)

    @pl.when(j == 0)
    def _():
        x = h_ref[...]
        g = g_ref[...]
        xn_ref[:tm] = _rms(x, g).astype(BF16)
        pos = i % tiles_per_seq
        before = jnp.where(pos == 0, 0.0, _rms(hp_ref[...], g)[BF16_ROWS - 1:BF16_ROWS])
        after = jnp.where(pos == tiles_per_seq - 1, 0.0, _rms(hn_ref[...], g)[0:1])
        rid = lax.broadcasted_iota(jnp.int32, (BF16_ROWS, x.shape[1]), 0)
        halo = jnp.where(rid == 0, before, jnp.where(rid == 1, after, 0.0))
        xn_ref[tm:] = halo.astype(BF16)
        o_ref[...] = x

    ga = jnp.dot(xn_ref[...], wg_ref[...], preferred_element_type=F32)
    up = jnp.dot(xn_ref[:tm], wu_ref[...], preferred_element_type=F32)
    gate = ga[:tm]
    rid = lax.broadcasted_iota(jnp.int32, gate.shape, 0)
    g_prev = jnp.where(rid == 0, ga[tm:tm + 1], pltpu.roll(gate, 1, 0))
    g_next = jnp.where(rid == tm - 1, ga[tm + 1:tm + 2], pltpu.roll(gate, tm - 1, 0))
    cw = cw_ref[...]
    conv = g_prev * cw[0:1] + gate * cw[1:2] + g_next * cw[2:3] + cb_ref[...]
    act = (jax.nn.gelu(conv) * up).astype(BF16)
    o_ref[...] += jnp.dot(act, wd_ref[...], preferred_element_type=F32)


def _ffn(h, g, w_gu, conv_w, conv_b, w_down, seq, tm=512, tf=512):
    t, d = h.shape
    f = w_down.shape[0]
    tm, tf = _tile(seq, tm), _tile(f, tf)
    nf = f // tf
    hb = tm // BF16_ROWS
    nhb = t // BF16_ROWS
    return pl.pallas_call(
        functools.partial(_ffn_kernel, tm=tm, tiles_per_seq=seq // tm),
        grid=(t // tm, nf),
        in_specs=[
            pl.BlockSpec((tm, d), lambda i, j: (i, 0)),
            pl.BlockSpec((BF16_ROWS, d), lambda i, j: (jnp.maximum(i * hb - 1, 0), 0)),
            pl.BlockSpec((BF16_ROWS, d), lambda i, j: (jnp.minimum((i + 1) * hb, nhb - 1), 0)),
            pl.BlockSpec((1, d), lambda i, j: (0, 0)),
            pl.BlockSpec((d, tf), lambda i, j: (0, j)),
            pl.BlockSpec((d, tf), lambda i, j: (0, nf + j)),
            pl.BlockSpec((conv_w.shape[0], tf), lambda i, j: (0, j)),
            pl.BlockSpec((1, tf), lambda i, j: (0, j)),
            pl.BlockSpec((tf, d), lambda i, j: (j, 0)),
        ],
        out_specs=pl.BlockSpec((tm, d), lambda i, j: (i, 0)),
        out_shape=jax.ShapeDtypeStruct((t, d), F32),
        scratch_shapes=[pltpu.VMEM((tm + BF16_ROWS, d), BF16)],
        compiler_params=_params("parallel", "arbitrary"),
        name="ffn",
    )(h, h, h, g, w_gu, w_gu, conv_w, conv_b, w_down)


def _ple_kernel(h_ref, p_ref, g_ref, wg_ref, wp_ref, gf_ref, o_ref, *, tn, final_norm):
    h = h_ref[...]
    xn = _rms(h, g_ref[...]).astype(BF16)
    pb = p_ref[...].astype(BF16)
    for j in range(h.shape[1] // tn):
        cols = slice(j * tn, (j + 1) * tn)
        gate = jax.nn.sigmoid(jnp.dot(xn, wg_ref[:, cols], preferred_element_type=F32))
        emb = jnp.dot(pb, wp_ref[:, cols], preferred_element_type=F32)
        o_ref[:, cols] = h[:, cols] + emb * gate
    if final_norm:
        o_ref[...] = _rms(o_ref[...], gf_ref[...])


def _ple(h, p, g, w_plg, w_ple, g_final, final_norm, tm=512, tn=512):
    t, d = h.shape
    tm, tn = _tile(t, tm), _tile(d, tn)
    whole = lambda a: pl.BlockSpec(a.shape, lambda i: (0, 0))
    return pl.pallas_call(
        functools.partial(_ple_kernel, tn=tn, final_norm=final_norm),
        grid=(t // tm,),
        in_specs=[
            pl.BlockSpec((tm, d), lambda i: (i, 0)),
            pl.BlockSpec((tm, p.shape[1]), lambda i: (i, 0)),
            whole(g), whole(w_plg), whole(w_ple), whole(g_final),
        ],
        out_specs=pl.BlockSpec((tm, d), lambda i: (i, 0)),
        out_shape=jax.ShapeDtypeStruct((t, d), F32),
        compiler_params=_params("parallel"),
        name="ple",
    )(h, p, g, w_plg, w_ple, g_final)


def _swap_halves(x):
    half = x.shape[-1] // 2
    return jnp.concatenate([x[..., half:], x[..., :half]], axis=-1)


def _pad_lanes(x):
    return jnp.concatenate([x, jnp.zeros(x.shape[:-1] + (LANES - x.shape[-1],), x.dtype)], axis=-1)


def _prepare_layer(i, w_in, g_qa, w_qb, g_kva, w_kvb, rpb, w_out, w_gu, w_down, w_plg, w_ple):
    d = w_in.shape[0]
    q_rank, kv_rank = g_qa.shape[0], g_kva.shape[0]
    o = q_rank + kv_rank
    kpe = w_in[:, o:o + MLA_ROPE]
    w_f = jnp.concatenate([w_in[:, :o], _pad_lanes(kpe), _pad_lanes(_swap_halves(kpe))], axis=1)
    w_b = w_in[:, o + MLA_ROPE:]
    na_w = NA_HEADS * NA_HEAD_DIM
    diff_w = DIFF_HEADS * 2 * DIFF_QK_DIM
    scale_b = np.ones((1, w_b.shape[1]), np.float32)
    scale_b[:, :na_w] = NA_HEAD_DIM ** -0.5 * LOG2E
    scale_b[:, 3 * na_w:3 * na_w + diff_w] = DIFF_QK_DIM ** -0.5 * LOG2E
    wq = w_qb.reshape(q_rank, MLA_HEADS, MLA_NOPE + MLA_ROPE)
    pe = wq[..., MLA_NOPE:]
    wq = jnp.concatenate([wq[..., :MLA_NOPE].reshape(q_rank, -1), _pad_lanes(pe).reshape(q_rank, -1),
                          _pad_lanes(_swap_halves(pe)).reshape(q_rank, -1)], axis=1)
    mla_w = MLA_HEADS * MLA_V
    return dict(
        w_in=jnp.concatenate([w_b, w_f], axis=1).astype(BF16), n_bf16=w_b.shape[1], scale_b=jnp.asarray(scale_b),
        wq=wq.astype(BF16), wkv=w_kvb.astype(BF16),
        na_bias=_na_bias_table(rpb),
        wo_mla=w_out[:mla_w].astype(BF16), wo_na=w_out[mla_w:mla_w + na_w].astype(BF16),
        wo_diff=w_out[mla_w + na_w:].astype(BF16),
        w_gu=w_gu.astype(BF16), w_down=w_down.astype(BF16),
        w_plg=w_plg.astype(BF16), w_ple=w_ple.astype(BF16),
        lam_init=0.8 - 0.6 * math.exp(-0.3 * i),
    )


def _rope_tables(seq):
    half = MLA_ROPE // 2
    freqs = jnp.power(ROPE_THETA, -jnp.arange(half, dtype=F32) / half)
    ang = jnp.arange(seq, dtype=jnp.int32).astype(F32)[:, None] * freqs[None, :]
    cos, sin = jnp.cos(ang), jnp.sin(ang)
    return _pad_lanes(jnp.concatenate([cos, cos], axis=1)), _pad_lanes(jnp.concatenate([-sin, sin], axis=1))


def _trunk(x, p, layers, vecs, g_final):
    batch, seq, d = x.shape
    t = batch * seq
    h = x.reshape(t, d)
    rope_c, rope_s = _rope_tables(seq)
    slopes = jnp.exp2(-8.0 * jnp.arange(1, DIFF_HEADS + 1, dtype=F32) / DIFF_HEADS) * LOG2E
    nb = NA_HEADS * NA_HEAD_DIM // LANES
    for i, (lw, lv) in enumerate(zip(layers, vecs)):
        zb, zf = _in_proj(h, lv["g_attn"], lw["w_in"], lw["scale_b"], lw["n_bf16"])
        q, k, v = _mla_proj(zf, lv["g_qa"], lv["g_kva"], lw["wq"], lw["wkv"], rope_c, rope_s, seq)
        o_mla = _mla_flash(q, k, v, batch, seq)
        o_na = _na_attention(zb, (0, nb, 2 * nb), lw["na_bias"], batch, seq)
        o_diff = _diff_flash(zb, (3 * nb, 4 * nb, 5 * nb), slopes, lv["lam_q1"], lv["lam_k1"], lv["lam_q2"],
                             lv["lam_k2"], lv["g_subln"], lw["lam_init"], batch, seq)
        h = _out_proj(o_mla, o_na, o_diff, lw["wo_mla"], lw["wo_na"], lw["wo_diff"], h)
        h = _ffn(h, lv["g_ffn"], lw["w_gu"], lv["conv_w"], lv["conv_b"], lw["w_down"], seq)
        h = _ple(h, p[i].reshape(t, -1), lv["g_ple"], lw["w_plg"], lw["w_ple"], g_final.reshape(1, d),
                 final_norm=i == len(layers) - 1)
    return h.reshape(batch, seq, d)


def kernel(x_prompt, x_sample, p_prompt, p_sample, g_attn, w_in, g_qa, w_qb, g_kva, w_kvb, rpb, lam_q1, lam_k1,
           lam_q2, lam_k2, g_subln, w_out, g_ffn, w_gu, conv_w, conv_b, w_down, g_ple, w_plg, w_ple, g_final):
    depth = w_in.shape[0]
    layers = [_prepare_layer(i, w_in[i], g_qa[i], w_qb[i], g_kva[i], w_kvb[i], rpb[i], w_out[i], w_gu[i],
                             w_down[i], w_plg[i], w_ple[i]) for i in range(depth)]
    row = lambda a: a.reshape(1, -1)
    vecs = [dict(g_attn=row(g_attn[i]), g_qa=row(g_qa[i]), g_kva=row(g_kva[i]), lam_q1=row(lam_q1[i]),
                 lam_k1=row(lam_k1[i]), lam_q2=row(lam_q2[i]), lam_k2=row(lam_k2[i]), g_subln=row(g_subln[i]),
                 g_ffn=row(g_ffn[i]), conv_w=conv_w[i], conv_b=row(conv_b[i]), g_ple=row(g_ple[i]))
            for i in range(depth)]
    return (_trunk(x_prompt, p_prompt, layers, vecs, g_final), _trunk(x_sample, p_sample, layers, vecs, g_final))
```

```python
import functools
import math

import jax
import jax.numpy as jnp
import numpy as np
from jax import lax
from jax.experimental import pallas as pl
from jax.experimental.pallas import tpu as pltpu

F32 = jnp.float32
BF16 = jnp.bfloat16

GRID_W = 64
MLA_HEADS = 8
MLA_NOPE = 128
MLA_ROPE = 64
MLA_V = 128
MLA_QK_PAD = 256
ROPE_THETA = 10000.0
NA_HEADS = 4
NA_HEAD_DIM = 128
NA_WIN_ROWS = 8
NA_WIN_COLS = 16
NA_ROWS_PER_STEP = 8
NA_GROUP = 4
NA_UNION = NA_WIN_ROWS + NA_GROUP - 1
DIFF_HEADS = 4
DIFF_QK_DIM = 64
DIFF_V_DIM = 128
RMS_EPS = 1e-6
LOG2E = 1.4426950408889634
MASK_VALUE = -1e30
EXP2_UNDERFLOW = 160.0
NORM_BOUND_SLACK = 1.001
LANES = 128
BF16_ROWS = 16
VMEM_LIMIT = 56 * 1024 * 1024


def _tile(n, pref):
    t = min(n, pref)
    assert n % t == 0, (n, pref)
    return t


def _rms(x, g):
    ms = jnp.mean(x * x, axis=-1, keepdims=True)
    return x * lax.rsqrt(ms + RMS_EPS) * g


def _lane_tile(x, n):
    return jnp.concatenate([x] * n, axis=1)


def _params(*sem):
    return pltpu.CompilerParams(dimension_semantics=sem, vmem_limit_bytes=VMEM_LIMIT)


def _norm_kernel(x_ref, g_ref, o_ref):
    o_ref[...] = _rms(x_ref[...], g_ref[...]).astype(o_ref.dtype)


def _norm(x, g, tm=1024):
    t, d = x.shape
    tm = _tile(t, tm)
    return pl.pallas_call(
        _norm_kernel,
        grid=(t // tm,),
        in_specs=[pl.BlockSpec((tm, d), lambda i: (i, 0)), pl.BlockSpec((1, d), lambda i: (0, 0))],
        out_specs=pl.BlockSpec((tm, d), lambda i: (i, 0)),
        out_shape=jax.ShapeDtypeStruct((t, d), BF16),
        compiler_params=_params("parallel"),
        name="norm",
    )(x, g)


def _in_proj_kernel(x_ref, w_ref, cs_ref, zb_ref, zf_ref, *, nb):
    j = pl.program_id(1)
    acc = jnp.dot(x_ref[...], w_ref[...], preferred_element_type=F32)

    @pl.when(j < nb)
    def _():
        zb_ref[...] = (acc * cs_ref[...]).astype(BF16)

    @pl.when(j >= nb)
    def _():
        zf_ref[...] = acc


def _in_proj(xn, w, col_scale, n_bf16, tm=1024, tn=1024):
    t, d = xn.shape
    n = w.shape[1]
    tm, tn = _tile(t, tm), _tile(math.gcd(n_bf16, n - n_bf16), tn)
    nb, nf = n_bf16 // tn, (n - n_bf16) // tn
    return pl.pallas_call(
        functools.partial(_in_proj_kernel, nb=nb),
        grid=(t // tm, nb + nf),
        in_specs=[
            pl.BlockSpec((tm, d), lambda i, j: (i, 0)),
            pl.BlockSpec((d, tn), lambda i, j: (0, j)),
            pl.BlockSpec((1, tn), lambda i, j: (0, jnp.minimum(j, nb - 1))),
        ],
        out_specs=[pl.BlockSpec((tm, tn), lambda i, j: (i, jnp.minimum(j, nb - 1))),
                   pl.BlockSpec((tm, tn), lambda i, j: (i, jnp.maximum(j - nb, 0)))],
        out_shape=[jax.ShapeDtypeStruct((t, n_bf16), BF16), jax.ShapeDtypeStruct((t, n - n_bf16), F32)],
        compiler_params=_params("parallel", "arbitrary"),
        name="in_proj",
    )(xn, w, col_scale)


def _mla_proj_kernel(zf_ref, gq_ref, gkv_ref, wq_ref, wkv_ref, c_ref, s_ref, q_ref, k_ref, v_ref, *,
                     q_rank, kv_rank, q_scale):
    zf = zf_ref[...]
    cos, sin = c_ref[...], s_ref[...]
    qn = _rms(zf[:, :q_rank], gq_ref[...]).astype(BF16)
    qa = jnp.dot(qn, wq_ref[...], preferred_element_type=F32)
    kvn = _rms(zf[:, q_rank:q_rank + kv_rank], gkv_ref[...]).astype(BF16)
    kva = jnp.dot(kvn, wkv_ref[...], preferred_element_type=F32)
    o = q_rank + kv_rank
    kpe = (zf[:, o:o + LANES] * cos + zf[:, o + LANES:o + 2 * LANES] * sin).astype(BF16)
    hp = MLA_HEADS * LANES
    for h in range(MLA_HEADS):
        lo = h * LANES
        rope = qa[:, hp + lo:hp + lo + LANES] * cos + qa[:, 2 * hp + lo:2 * hp + lo + LANES] * sin
        q_ref[:, 2 * lo:2 * lo + LANES] = (qa[:, lo:lo + LANES] * q_scale).astype(BF16)
        q_ref[:, 2 * lo + LANES:2 * lo + 2 * LANES] = (rope * q_scale).astype(BF16)
        k_ref[:, 2 * lo:2 * lo + LANES] = kva[:, 2 * lo:2 * lo + LANES].astype(BF16)
        k_ref[:, 2 * lo + LANES:2 * lo + 2 * LANES] = kpe
        v_ref[:, lo:lo + LANES] = kva[:, 2 * lo + LANES:2 * lo + 2 * LANES].astype(BF16)


def _mla_proj(zf, gq, gkv, wq, wkv, rope_c, rope_s, seq, tm=512):
    t, zw = zf.shape
    tm = _tile(seq, tm)
    nl = seq // tm
    q_rank, kv_rank = gq.shape[1], gkv.shape[1]
    kern = functools.partial(_mla_proj_kernel, q_rank=q_rank, kv_rank=kv_rank,
                             q_scale=(MLA_NOPE + MLA_ROPE) ** -0.5 * LOG2E)
    qk_w = MLA_HEADS * MLA_QK_PAD
    return pl.pallas_call(
        kern,
        grid=(t // tm,),
        in_specs=[
            pl.BlockSpec((tm, zw), lambda i: (i, 0)),
            pl.BlockSpec((1, q_rank), lambda i: (0, 0)),
            pl.BlockSpec((1, kv_rank), lambda i: (0, 0)),
            pl.BlockSpec(wq.shape, lambda i: (0, 0)),
            pl.BlockSpec(wkv.shape, lambda i: (0, 0)),
            pl.BlockSpec((tm, LANES), lambda i: (i % nl, 0)),
            pl.BlockSpec((tm, LANES), lambda i: (i % nl, 0)),
        ],
        out_specs=[
            pl.BlockSpec((tm, qk_w), lambda i: (i, 0)),
            pl.BlockSpec((tm, qk_w), lambda i: (i, 0)),
            pl.BlockSpec((tm, MLA_HEADS * MLA_V), lambda i: (i, 0)),
        ],
        out_shape=[
            jax.ShapeDtypeStruct((t, qk_w), BF16),
            jax.ShapeDtypeStruct((t, qk_w), BF16),
            jax.ShapeDtypeStruct((t, MLA_HEADS * MLA_V), BF16),
        ],
        compiler_params=_params("parallel"),
        name="mla_proj",
    )(zf, gq, gkv, wq, wkv, rope_c, rope_s)


def _online_softmax_step(s, v, m_ref, acc_ref, shift):
    m_prev = m_ref[...]
    m_cur = jnp.max(s, axis=1, keepdims=True)
    if shift is not None:
        m_cur = m_cur + shift
    m_next = jnp.maximum(m_prev, m_cur)
    alpha = jnp.exp2(m_prev - m_next)
    m_sub = m_next if shift is None else m_next - shift
    p = jnp.exp2(s - _lane_tile(m_sub, s.shape[1] // LANES))
    v1 = jnp.concatenate([v, jnp.ones_like(v)], axis=1)
    acc_ref[...] = _lane_tile(alpha, 2) * acc_ref[...] + jnp.dot(p.astype(BF16), v1, preferred_element_type=F32)
    m_ref[...] = m_next


def _init_softmax_state(m_ref, acc_ref):
    m_ref[...] = jnp.full(m_ref.shape, MASK_VALUE, F32)
    acc_ref[...] = jnp.zeros(acc_ref.shape, F32)


def _softmax_result(acc_ref):
    acc = acc_ref[...]
    return acc[:, :LANES] / acc[:, LANES:]


def _qk(q, k):
    return lax.dot_general(q, k, (((1,), (1,)), ((), ())), preferred_element_type=F32)


def _mla_flash_kernel(q_ref, k_ref, v_ref, o_ref, m_ref, acc_ref, *, tkc, unroll):
    _init_softmax_state(m_ref, acc_ref)
    q = q_ref[...]

    def block(bi, carry):
        for c in range(unroll):
            off = pl.multiple_of((bi * unroll + c) * tkc, tkc)
            s = _qk(q, k_ref[pl.ds(off, tkc), :])
            _online_softmax_step(s, v_ref[pl.ds(off, tkc), :], m_ref, acc_ref, None)
        return carry

    lax.fori_loop(0, k_ref.shape[0] // (tkc * unroll), block, 0)
    o_ref[...] = _softmax_result(acc_ref).astype(o_ref.dtype)


def _mla_flash(q, k, v, batch, seq, tq=1024, tkc=512, unroll=16):
    t = q.shape[0]
    tq, tkc = _tile(seq, tq), _tile(seq, tkc)
    unroll = _tile(seq // tkc, unroll)
    nq = seq // tq
    return pl.pallas_call(
        functools.partial(_mla_flash_kernel, tkc=tkc, unroll=unroll),
        grid=(MLA_HEADS, batch, nq),
        in_specs=[
            pl.BlockSpec((tq, MLA_QK_PAD), lambda g, b, qi: (b * nq + qi, g)),
            pl.BlockSpec((seq, MLA_QK_PAD), lambda g, b, qi: (b, g)),
            pl.BlockSpec((seq, MLA_V), lambda g, b, qi: (b, g)),
        ],
        out_specs=pl.BlockSpec((tq, MLA_V), lambda g, b, qi: (b * nq + qi, g)),
        out_shape=jax.ShapeDtypeStruct((t, MLA_HEADS * MLA_V), BF16),
        scratch_shapes=[pltpu.VMEM((tq, LANES), F32), pltpu.VMEM((tq, MLA_V + LANES), F32)],
        compiler_params=_params("parallel", "parallel", "parallel"),
        name="mla_flash",
    )(q, k, v)


def _key_norm_kernel(k_ref, o_ref):
    @pl.when(pl.program_id(2) == 0)
    def _():
        o_ref[...] = jnp.zeros(o_ref.shape, F32)

    k = k_ref[...].astype(F32)
    top = jnp.max(jnp.sum(k * k, axis=1, keepdims=True), axis=0, keepdims=True)
    o_ref[...] = jnp.maximum(o_ref[...], top)


def _max_key_norm_sq(zb, kc, batch, seq, tk=2048):
    tk = _tile(seq, tk)
    nk = seq // tk
    return pl.pallas_call(
        _key_norm_kernel,
        grid=(batch, DIFF_HEADS, nk),
        in_specs=[pl.BlockSpec((tk, LANES), lambda b, g, i: (b * nk + i, kc + g))],
        out_specs=pl.BlockSpec((8, LANES), lambda b, g, i: (b * DIFF_HEADS + g, 0)),
        out_shape=jax.ShapeDtypeStruct((batch * DIFF_HEADS * 8, LANES), F32),
        compiler_params=_params("parallel", "parallel", "arbitrary"),
        name="key_norm",
    )(zb)


def _diff_flash_kernel(slope_ref, kn_ref, lq1_ref, lk1_ref, lq2_ref, lk2_ref, gsub_ref, q_ref, k_ref, v_ref, o_ref,
                       qs_ref, m_ref, acc_ref, bias_ref, cap_ref, qx_ref, kx_ref, *, tq, tkb, lam_init):
    g, qi = pl.program_id(0), pl.program_id(2)
    slope = slope_ref[g]
    nk = k_ref.shape[0] // tkb
    nch = tkb // tq
    q0 = qi * tq

    @pl.when(qi == 0)
    def _():
        shape = (2 * tq, tq)
        row = lax.broadcasted_iota(jnp.int32, shape, 0)
        row = jnp.where(row >= tq, row - tq, row)
        d = (row - lax.broadcasted_iota(jnp.int32, shape, 1)).astype(F32)
        bias_ref[0] = -slope * d
        bias_ref[1] = -slope * jnp.abs(d)
        bias_ref[2] = slope * d
        whole = jnp.full((1, LANES), slope, F32)
        p1 = whole.astype(BF16).astype(F32)
        p2 = (whole - p1).astype(BF16).astype(F32)
        p3 = (whole - p1 - p2).astype(BF16).astype(F32)
        lane = lax.broadcasted_iota(jnp.int32, (tq, LANES), 1)
        pos = lax.broadcasted_iota(jnp.int32, (tq, LANES), 0)
        even = (lane & 1) == 0
        digit = jnp.where(even, pos >> 4, pos & 15).astype(F32)
        pair = jnp.where(lane >= 6, lane - 6, lane) >> 1
        piece = jnp.where(pair == 0, p1, jnp.where(pair == 1, p2, p3))
        weight = jnp.where(even, 16.0 * piece, piece)
        kx_ref[...] = jnp.where(lane < 6, weight, jnp.where(lane < 12, digit, 0.0)).astype(BF16)
        after = jnp.where(lane < 6, digit, jnp.where(lane < 12, -weight, 0.0)).astype(BF16)
        for half in range(2):
            qx_ref[0, half * tq:(half + 1) * tq, LANES:] = -after
            qx_ref[1, half * tq:(half + 1) * tq, LANES:] = after

    _init_softmax_state(m_ref, acc_ref)
    q = q_ref[...]
    lane = lax.broadcasted_iota(jnp.int32, q.shape, 1)
    qs_ref[:tq] = jnp.where(lane < DIFF_QK_DIM, q, jnp.zeros_like(q))
    qs_ref[tq:] = jnp.where(lane >= DIFF_QK_DIM, q, jnp.zeros_like(q))
    for side in range(2):
        qx_ref[side, :, :LANES] = qs_ref[...]
    qf = q.astype(F32)
    q_norm = jnp.sqrt(jnp.sum(qf * qf, axis=1, keepdims=True))
    cap = q_norm * jnp.sqrt(kn_ref[0:1, :]) * NORM_BOUND_SLACK + NORM_BOUND_SLACK
    cap_ref[:tq] = cap
    cap_ref[tq:] = cap

    unrolled_from = min(nch, (5 * nch + 7) // 8)

    def block(ki, far):
        kb0 = ((q0 // tkb + ki) % nk) * tkb
        best = jnp.max(cap_ref[...] - m_ref[...], axis=0, keepdims=True)
        cidx = lax.broadcasted_iota(jnp.int32, (8, LANES), 0)
        kc0 = kb0 + cidx * tq
        nearest = jnp.maximum(jnp.maximum(q0 - (kc0 + tq) + 1, kc0 - (q0 + tq) + 1), 0)
        live = (cidx < nch) & (best >= slope * nearest.astype(F32) - EXP2_UNDERFLOW)
        n_live = jnp.sum(jnp.where(live, 1, 0), axis=0, keepdims=True)[0, 0]
        first_live = jnp.min(jnp.where(live, cidx, nch), axis=0, keepdims=True)[0, 0]

        def chunk(c):
            k0 = pl.multiple_of(kb0 + c * tq, tq)
            gap = jnp.full((1, LANES), jnp.abs(q0 - k0), jnp.int32).astype(F32)
            if far:
                keys = jnp.concatenate([k_ref[pl.ds(k0, tq), :], kx_ref[...]], axis=1)
                s = _qk(qx_ref[jnp.where(k0 < q0, 0, 1)], keys)
            else:
                which = jnp.where(k0 < q0, 0, jnp.where(k0 == q0, 1, 2))
                s = _qk(qs_ref[...], k_ref[pl.ds(k0, tq), :]) + bias_ref[which]
            _online_softmax_step(s, v_ref[pl.ds(k0, tq), :], m_ref, acc_ref, -slope * gap)

        @pl.when(n_live >= unrolled_from)
        def _():
            for c in range(nch):
                chunk(c)

        @pl.when((n_live > 0) & (n_live < unrolled_from))
        def _():
            def body(c, inner):
                chunk(c)
                return inner

            lax.fori_loop(first_live, first_live + n_live, body, 0)

    block(0, far=False)

    def far_block(ki, carry):
        block(ki, far=True)
        return carry

    lax.fori_loop(1, nk, far_block, 0)

    o = _softmax_result(acc_ref)
    lam = (jnp.exp(jnp.sum(lq1_ref[...] * lk1_ref[...], axis=-1, keepdims=True))
           - jnp.exp(jnp.sum(lq2_ref[...] * lk2_ref[...], axis=-1, keepdims=True)) + lam_init)
    d = o[:tq] - lam * o[tq:]
    o_ref[...] = (_rms(d, gsub_ref[...]) * (1.0 - lam_init)).astype(o_ref.dtype)


def _diff_flash(zb, col0, slopes, lq1, lk1, lq2, lk2, gsub, lam_init, batch, seq, tq=512, tkb=4096):
    t = zb.shape[0]
    tq = _tile(seq, tq)
    tkb = _tile(seq, tkb)
    assert tkb % tq == 0 and tkb // tq <= 8
    nq = seq // tq
    qc, kc, vc = col0
    key_norm = _max_key_norm_sq(zb, kc, batch, seq)
    vec = lambda a: pl.BlockSpec(a.shape, lambda g, b, qi: (0, 0))
    return pl.pallas_call(
        functools.partial(_diff_flash_kernel, tq=tq, tkb=tkb, lam_init=lam_init),
        grid=(DIFF_HEADS, batch, nq),
        in_specs=[
            pl.BlockSpec(memory_space=pltpu.SMEM),
            pl.BlockSpec((8, LANES), lambda g, b, qi: (b * DIFF_HEADS + g, 0)),
            vec(lq1), vec(lk1), vec(lq2), vec(lk2), vec(gsub),
            pl.BlockSpec((tq, LANES), lambda g, b, qi: (b * nq + qi, qc + g)),
            pl.BlockSpec((seq, LANES), lambda g, b, qi: (b, kc + g)),
            pl.BlockSpec((seq, DIFF_V_DIM), lambda g, b, qi: (b, vc + g)),
        ],
        out_specs=pl.BlockSpec((tq, DIFF_V_DIM), lambda g, b, qi: (b * nq + qi, g)),
        out_shape=jax.ShapeDtypeStruct((t, DIFF_HEADS * DIFF_V_DIM), BF16),
        scratch_shapes=[pltpu.VMEM((2 * tq, LANES), BF16), pltpu.VMEM((2 * tq, LANES), F32),
                        pltpu.VMEM((2 * tq, DIFF_V_DIM + LANES), F32), pltpu.VMEM((3, 2 * tq, tq), F32),
                        pltpu.VMEM((2 * tq, LANES), F32), pltpu.VMEM((2, 2 * tq, 2 * LANES), BF16),
                        pltpu.VMEM((tq, LANES), BF16)],
        compiler_params=_params("parallel", "parallel", "arbitrary"),
        name="diff_flash",
    )(slopes, key_norm, lq1, lk1, lq2, lk2, gsub, zb, zb, zb)


def _na_kernel(q_ref, kp_ref, kc_ref, kn_ref, vp_ref, vc_ref, vn_ref, bias_ref, o_ref, kb_ref, vb_ref, *, rows):
    j = pl.program_id(2)
    blk = NA_ROWS_PER_STEP * GRID_W
    for n, (kr, vr) in enumerate(((kp_ref, vp_ref), (kc_ref, vc_ref), (kn_ref, vn_ref))):
        kb_ref[n * blk:(n + 1) * blk] = kr[...]
        vb_ref[n * blk:(n + 1) * blk] = vr[...]
    n_groups = rows // NA_GROUP
    gq = NA_GROUP * GRID_W
    for gi in range(NA_ROWS_PER_STEP // NA_GROUP):
        group = j * (NA_ROWS_PER_STEP // NA_GROUP) + gi
        union_start = jnp.clip(group * NA_GROUP - NA_WIN_ROWS // 2, 0, rows - NA_UNION)
        start = pl.multiple_of((union_start - (j - 1) * NA_ROWS_PER_STEP) * GRID_W, GRID_W)
        variant = jnp.where(group == 0, 0, jnp.where(group == n_groups - 1, 2, 1))
        q = q_ref[gi * gq:(gi + 1) * gq, :]
        s = _qk(q, kb_ref[pl.ds(start, NA_UNION * GRID_W), :]) + bias_ref[variant]
        p = jnp.exp2(s - jnp.max(s, axis=1, keepdims=True))
        v = vb_ref[pl.ds(start, NA_UNION * GRID_W), :]
        acc = jnp.dot(p.astype(BF16), jnp.concatenate([v, jnp.ones_like(v)], axis=1), preferred_element_type=F32)
        o_ref[gi * gq:(gi + 1) * gq, :] = (acc[:, :NA_HEAD_DIM] / acc[:, NA_HEAD_DIM:]).astype(o_ref.dtype)


def _na_attention(zb, col0, bias, batch, seq):
    t = zb.shape[0]
    rows = seq // GRID_W
    assert rows % NA_ROWS_PER_STEP == 0 and rows >= 2 * NA_ROWS_PER_STEP
    nj = rows // NA_ROWS_PER_STEP
    blk = NA_ROWS_PER_STEP * GRID_W
    qc, kc, vc = col0

    def spec(c0, dj):
        return pl.BlockSpec((blk, NA_HEAD_DIM),
                            lambda h, b, j: (b * nj + jnp.clip(j + dj, 0, nj - 1), c0 + h))

    return pl.pallas_call(
        functools.partial(_na_kernel, rows=rows),
        grid=(NA_HEADS, batch, nj),
        in_specs=[spec(qc, 0), spec(kc, -1), spec(kc, 0), spec(kc, 1), spec(vc, -1), spec(vc, 0), spec(vc, 1),
                  pl.BlockSpec((None,) + bias.shape[1:], lambda h, b, j: (h, 0, 0, 0))],
        out_specs=pl.BlockSpec((blk, NA_HEAD_DIM), lambda h, b, j: (b * nj + j, h)),
        out_shape=jax.ShapeDtypeStruct((t, NA_HEADS * NA_HEAD_DIM), BF16),
        scratch_shapes=[pltpu.VMEM((3 * blk, NA_HEAD_DIM), BF16), pltpu.VMEM((3 * blk, NA_HEAD_DIM), BF16)],
        compiler_params=_params("parallel", "parallel", "parallel"),
        name="na_attention",
    )(zb, zb, zb, zb, zb, zb, zb, bias)


def _na_bias_table(rpb):
    col = np.arange(GRID_W)
    col_start = np.clip(col - NA_WIN_COLS // 2, 0, GRID_W - NA_WIN_COLS)
    kc = np.arange(GRID_W)
    valid = (kc[None, :] >= col_start[:, None]) & (kc[None, :] < col_start[:, None] + NA_WIN_COLS)
    by_row = jnp.stack([rpb[:, NA_WIN_ROWS - 1 - t:2 * NA_WIN_ROWS - 1 - t] for t in range(NA_WIN_ROWS)], axis=1)
    pad = GRID_W - NA_WIN_COLS
    padded = jnp.pad(by_row, ((0, 0), (0, 0), (0, 0), (pad, pad)))
    tbl = jnp.stack([padded[..., GRID_W - 1 - c:2 * GRID_W - 1 - c] for c in range(GRID_W)], axis=2)
    tbl = jnp.where(valid[None, None, :, None, :], tbl * LOG2E, MASK_VALUE)
    per_row = tbl.reshape(rpb.shape[0], NA_WIN_ROWS, GRID_W, NA_WIN_ROWS * GRID_W).astype(F32)

    extra = NA_UNION - NA_WIN_ROWS

    def place(t, offset):
        return jnp.pad(per_row[:, t], ((0, 0), (0, 0), (offset * GRID_W, (extra - offset) * GRID_W)),
                       constant_values=MASK_VALUE)

    half = NA_WIN_ROWS // 2
    first = jnp.concatenate([place(i, 0) for i in range(NA_GROUP)], axis=1)
    inner = jnp.concatenate([place(half, i) for i in range(NA_GROUP)], axis=1)
    last = jnp.concatenate([place(half + i, extra) for i in range(NA_GROUP)], axis=1)
    return jnp.stack([first, inner, last], axis=1)


def _out_proj_kernel(a_ref, b_ref, c_ref, wa_ref, wb_ref, wc_ref, h_ref, g_ref, o_ref, xn_ref, *, tn):
    a, b, c = a_ref[...], b_ref[...], c_ref[...]
    for j in range(o_ref.shape[1] // tn):
        cols = slice(j * tn, (j + 1) * tn)
        acc = jnp.dot(a, wa_ref[:, cols], preferred_element_type=F32)
        acc = acc + jnp.dot(b, wb_ref[:, cols], preferred_element_type=F32)
        acc = acc + jnp.dot(c, wc_ref[:, cols], preferred_element_type=F32)
        o_ref[:, cols] = h_ref[:, cols] + acc
    xn_ref[...] = _rms(o_ref[...], g_ref[...]).astype(BF16)


def _out_proj(a, b, c, wa, wb, wc, h, g_next, tm=512, tn=512):
    t, d = h.shape
    tm, tn = _tile(t, tm), _tile(d, tn)
    rows = lambda x: pl.BlockSpec((tm, x.shape[1]), lambda i: (i, 0))
    whole = lambda w: pl.BlockSpec(w.shape, lambda i: (0, 0))
    return pl.pallas_call(
        functools.partial(_out_proj_kernel, tn=tn),
        grid=(t // tm,),
        in_specs=[rows(a), rows(b), rows(c), whole(wa), whole(wb), whole(wc), rows(h), whole(g_next)],
        out_specs=[rows(h), rows(h)],
        out_shape=[jax.ShapeDtypeStruct((t, d), F32), jax.ShapeDtypeStruct((t, d), BF16)],
        compiler_params=_params("parallel"),
        name="out_proj",
    )(a, b, c, wa, wb, wc, h, g_next)


def _ffn_kernel(h_ref, x_ref, xp_ref, xa_ref, wg_ref, wu_ref, cw_ref, cb_ref, wd_ref, o_ref, xn_ref, *,
                tm, tiles_per_seq):
    i, j = pl.program_id(0), pl.program_id(1)

    @pl.when(j == 0)
    def _():
        xn_ref[:tm] = x_ref[...]
        pos = i % tiles_per_seq
        before = xp_ref[...].astype(F32)[BF16_ROWS - 1:BF16_ROWS]
        after = xa_ref[...].astype(F32)[0:1]
        before = jnp.where(pos == 0, 0.0, before)
        after = jnp.where(pos == tiles_per_seq - 1, 0.0, after)
        rid = lax.broadcasted_iota(jnp.int32, (BF16_ROWS, x_ref.shape[1]), 0)
        halo = jnp.where(rid == 0, before, jnp.where(rid == 1, after, 0.0))
        xn_ref[tm:] = halo.astype(BF16)
        o_ref[...] = h_ref[...]

    ga = jnp.dot(xn_ref[...], wg_ref[...], preferred_element_type=F32)
    up = jnp.dot(xn_ref[:tm], wu_ref[...], preferred_element_type=F32)
    gate = ga[:tm]
    rid = lax.broadcasted_iota(jnp.int32, gate.shape, 0)
    g_prev = jnp.where(rid == 0, ga[tm:tm + 1], pltpu.roll(gate, 1, 0))
    g_next = jnp.where(rid == tm - 1, ga[tm + 1:tm + 2], pltpu.roll(gate, tm - 1, 0))
    cw = cw_ref[...]
    conv = g_prev * cw[0:1] + gate * cw[1:2] + g_next * cw[2:3] + cb_ref[...]
    act = (jax.nn.gelu(conv) * up).astype(BF16)
    o_ref[...] += jnp.dot(act, wd_ref[...], preferred_element_type=F32)


def _ffn(h, xn, w_gu, conv_w, conv_b, w_down, seq, tm=512, tf=512):
    t, d = h.shape
    f = w_down.shape[0]
    tm, tf = _tile(seq, tm), _tile(f, tf)
    nf = f // tf
    hb = tm // BF16_ROWS
    nhb = t // BF16_ROWS
    return pl.pallas_call(
        functools.partial(_ffn_kernel, tm=tm, tiles_per_seq=seq // tm),
        grid=(t // tm, nf),
        in_specs=[
            pl.BlockSpec((tm, d), lambda i, j: (i, 0)),
            pl.BlockSpec((tm, d), lambda i, j: (i, 0)),
            pl.BlockSpec((BF16_ROWS, d), lambda i, j: (jnp.maximum(i * hb - 1, 0), 0)),
            pl.BlockSpec((BF16_ROWS, d), lambda i, j: (jnp.minimum((i + 1) * hb, nhb - 1), 0)),
            pl.BlockSpec((d, tf), lambda i, j: (0, j)),
            pl.BlockSpec((d, tf), lambda i, j: (0, nf + j)),
            pl.BlockSpec((conv_w.shape[0], tf), lambda i, j: (0, j)),
            pl.BlockSpec((1, tf), lambda i, j: (0, j)),
            pl.BlockSpec((tf, d), lambda i, j: (j, 0)),
        ],
        out_specs=pl.BlockSpec((tm, d), lambda i, j: (i, 0)),
        out_shape=jax.ShapeDtypeStruct((t, d), F32),
        scratch_shapes=[pltpu.VMEM((tm + BF16_ROWS, d), BF16)],
        compiler_params=_params("parallel", "arbitrary"),
        name="ffn",
    )(h, xn, xn, xn, w_gu, w_gu, conv_w, conv_b, w_down)


def _ple_kernel(h_ref, p_ref, g_ref, wg_ref, wp_ref, gn_ref, o_ref, *maybe_xn_ref, tn):
    h = h_ref[...]
    xn = _rms(h, g_ref[...]).astype(BF16)
    pb = p_ref[...].astype(BF16)
    for j in range(h.shape[1] // tn):
        cols = slice(j * tn, (j + 1) * tn)
        gate = jax.nn.sigmoid(jnp.dot(xn, wg_ref[:, cols], preferred_element_type=F32))
        emb = jnp.dot(pb, wp_ref[:, cols], preferred_element_type=F32)
        o_ref[:, cols] = h[:, cols] + emb * gate
    normed = _rms(o_ref[...], gn_ref[...])
    if maybe_xn_ref:
        maybe_xn_ref[0][...] = normed.astype(BF16)
    else:
        o_ref[...] = normed


def _ple(h, p, g, w_plg, w_ple, g_next, last, tm=512, tn=512):
    t, d = h.shape
    tm, tn = _tile(t, tm), _tile(d, tn)
    whole = lambda a: pl.BlockSpec(a.shape, lambda i: (0, 0))
    rows = pl.BlockSpec((tm, d), lambda i: (i, 0))
    return pl.pallas_call(
        functools.partial(_ple_kernel, tn=tn),
        grid=(t // tm,),
        in_specs=[rows, pl.BlockSpec((tm, p.shape[1]), lambda i: (i, 0)),
                  whole(g), whole(w_plg), whole(w_ple), whole(g_next)],
        out_specs=[rows] if last else [rows, rows],
        out_shape=[jax.ShapeDtypeStruct((t, d), F32)] + ([] if last else [jax.ShapeDtypeStruct((t, d), BF16)]),
        compiler_params=_params("parallel"),
        name="ple",
    )(h, p, g, w_plg, w_ple, g_next)


def _swap_halves(x):
    half = x.shape[-1] // 2
    return jnp.concatenate([x[..., half:], x[..., :half]], axis=-1)


def _pad_lanes(x):
    return jnp.concatenate([x, jnp.zeros(x.shape[:-1] + (LANES - x.shape[-1],), x.dtype)], axis=-1)


def _prepare_layer(i, w_in, g_qa, w_qb, g_kva, w_kvb, rpb, w_out, w_gu, w_down, w_plg, w_ple):
    d = w_in.shape[0]
    q_rank, kv_rank = g_qa.shape[0], g_kva.shape[0]
    o = q_rank + kv_rank
    kpe = w_in[:, o:o + MLA_ROPE]
    w_f = jnp.concatenate([w_in[:, :o], _pad_lanes(kpe), _pad_lanes(_swap_halves(kpe))], axis=1)
    w_b = w_in[:, o + MLA_ROPE:]
    na_w = NA_HEADS * NA_HEAD_DIM
    diff_w = DIFF_HEADS * 2 * DIFF_QK_DIM
    scale_b = np.ones((1, w_b.shape[1]), np.float32)
    scale_b[:, :na_w] = NA_HEAD_DIM ** -0.5 * LOG2E
    scale_b[:, 3 * na_w:3 * na_w + diff_w] = DIFF_QK_DIM ** -0.5 * LOG2E
    wq = w_qb.reshape(q_rank, MLA_HEADS, MLA_NOPE + MLA_ROPE)
    pe = wq[..., MLA_NOPE:]
    wq = jnp.concatenate([wq[..., :MLA_NOPE].reshape(q_rank, -1), _pad_lanes(pe).reshape(q_rank, -1),
                          _pad_lanes(_swap_halves(pe)).reshape(q_rank, -1)], axis=1)
    mla_w = MLA_HEADS * MLA_V
    return dict(
        w_in=jnp.concatenate([w_b, w_f], axis=1).astype(BF16), n_bf16=w_b.shape[1], scale_b=jnp.asarray(scale_b),
        wq=wq.astype(BF16), wkv=w_kvb.astype(BF16),
        na_bias=_na_bias_table(rpb),
        wo_mla=w_out[:mla_w].astype(BF16), wo_na=w_out[mla_w:mla_w + na_w].astype(BF16),
        wo_diff=w_out[mla_w + na_w:].astype(BF16),
        w_gu=w_gu.astype(BF16), w_down=w_down.astype(BF16),
        w_plg=w_plg.astype(BF16), w_ple=w_ple.astype(BF16),
        lam_init=0.8 - 0.6 * math.exp(-0.3 * i),
    )


def _rope_tables(seq):
    half = MLA_ROPE // 2
    freqs = jnp.power(ROPE_THETA, -jnp.arange(half, dtype=F32) / half)
    ang = jnp.arange(seq, dtype=jnp.int32).astype(F32)[:, None] * freqs[None, :]
    cos, sin = jnp.cos(ang), jnp.sin(ang)
    return _pad_lanes(jnp.concatenate([cos, cos], axis=1)), _pad_lanes(jnp.concatenate([-sin, sin], axis=1))


def _trunk(x, p, layers, vecs, g_final):
    batch, seq, d = x.shape
    t = batch * seq
    h = x.reshape(t, d)
    rope_c, rope_s = _rope_tables(seq)
    slopes = jnp.exp2(-8.0 * jnp.arange(1, DIFF_HEADS + 1, dtype=F32) / DIFF_HEADS) * LOG2E
    nb = NA_HEADS * NA_HEAD_DIM // LANES
    xn = _norm(h, vecs[0]["g_attn"])
    for i, (lw, lv) in enumerate(zip(layers, vecs)):
        zb, zf = _in_proj(xn, lw["w_in"], lw["scale_b"], lw["n_bf16"])
        q, k, v = _mla_proj(zf, lv["g_qa"], lv["g_kva"], lw["wq"], lw["wkv"], rope_c, rope_s, seq)
        o_mla = _mla_flash(q, k, v, batch, seq)
        o_na = _na_attention(zb, (0, nb, 2 * nb), lw["na_bias"], batch, seq)
        o_diff = _diff_flash(zb, (3 * nb, 4 * nb, 5 * nb), slopes, lv["lam_q1"], lv["lam_k1"], lv["lam_q2"],
                             lv["lam_k2"], lv["g_subln"], lw["lam_init"], batch, seq)
        h, xn_ffn = _out_proj(o_mla, o_na, o_diff, lw["wo_mla"], lw["wo_na"], lw["wo_diff"], h, lv["g_ffn"])
        h = _ffn(h, xn_ffn, lw["w_gu"], lv["conv_w"], lv["conv_b"], lw["w_down"], seq)
        last = i == len(layers) - 1
        g_next = g_final.reshape(1, d) if last else vecs[i + 1]["g_attn"]
        outs = _ple(h, p[i].reshape(t, -1), lv["g_ple"], lw["w_plg"], lw["w_ple"], g_next, last)
        h, xn = outs if not last else (outs[0], None)
    return h.reshape(batch, seq, d)


def kernel(x_prompt, x_sample, p_prompt, p_sample, g_attn, w_in, g_qa, w_qb, g_kva, w_kvb, rpb, lam_q1, lam_k1,
           lam_q2, lam_k2, g_subln, w_out, g_ffn, w_gu, conv_w, conv_b, w_down, g_ple, w_plg, w_ple, g_final):
    depth = w_in.shape[0]
    layers = [_prepare_layer(i, w_in[i], g_qa[i], w_qb[i], g_kva[i], w_kvb[i], rpb[i], w_out[i], w_gu[i],
                             w_down[i], w_plg[i], w_ple[i]) for i in range(depth)]
    row = lambda a: a.reshape(1, -1)
    vecs = [dict(g_attn=row(g_attn[i]), g_qa=row(g_qa[i]), g_kva=row(g_kva[i]), lam_q1=row(lam_q1[i]),
                 lam_k1=row(lam_k1[i]), lam_q2=row(lam_q2[i]), lam_k2=row(lam_k2[i]), g_subln=row(g_subln[i]),
                 g_ffn=row(g_ffn[i]), conv_w=conv_w[i], conv_b=row(conv_b[i]), g_ple=row(g_ple[i]))
            for i in range(depth)]
    return (_trunk(x_prompt, p_prompt, layers, vecs, g_final), _trunk(x_sample, p_sample, layers, vecs, g_final))
```

```python
import functools
import math

import jax
import jax.numpy as jnp
import numpy as np
from jax import lax
from jax.experimental import pallas as pl
from jax.experimental.pallas import tpu as pltpu

F32 = jnp.float32
BF16 = jnp.bfloat16

GRID_W = 64
MLA_HEADS = 8
MLA_NOPE = 128
MLA_ROPE = 64
MLA_V = 128
MLA_QK_PAD = 256
ROPE_THETA = 10000.0
NA_HEADS = 4
NA_HEAD_DIM = 128
NA_WIN_ROWS = 8
NA_WIN_COLS = 16
NA_ROWS_PER_STEP = 16
NA_GROUP = 4
NA_UNION = NA_WIN_ROWS + NA_GROUP - 1
DIFF_HEADS = 4
DIFF_QK_DIM = 64
DIFF_V_DIM = 128
RMS_EPS = 1e-6
LOG2E = 1.4426950408889634
MASK_VALUE = -1e30
EXP2_UNDERFLOW = 160.0
NORM_BOUND_SLACK = 1.001
LANES = 128
BF16_ROWS = 16
VMEM_LIMIT = 56 * 1024 * 1024


def _tile(n, pref):
    t = min(n, pref)
    assert n % t == 0, (n, pref)
    return t


def _rms(x, g):
    ms = jnp.mean(x * x, axis=-1, keepdims=True)
    return x * lax.rsqrt(ms + RMS_EPS) * g


def _lane_tile(x, n):
    return jnp.concatenate([x] * n, axis=1)


def _params(*sem):
    return pltpu.CompilerParams(dimension_semantics=sem, vmem_limit_bytes=VMEM_LIMIT)


def _in_proj_kernel(x_ref, g_ref, w_ref, cs_ref, zb_ref, zf_ref, xn_ref, *, nb):
    j = pl.program_id(1)

    @pl.when(j == 0)
    def _():
        xn_ref[...] = _rms(x_ref[...], g_ref[...]).astype(BF16)

    acc = jnp.dot(xn_ref[...], w_ref[...], preferred_element_type=F32)

    @pl.when(j < nb)
    def _():
        zb_ref[...] = (acc * cs_ref[...]).astype(BF16)

    @pl.when(j >= nb)
    def _():
        zf_ref[...] = acc


def _in_proj(x, g, w, col_scale, n_bf16, tm=1024, tn=1024):
    t, d = x.shape
    n = w.shape[1]
    tm, tn = _tile(t, tm), _tile(math.gcd(n_bf16, n - n_bf16), tn)
    nb, nf = n_bf16 // tn, (n - n_bf16) // tn
    return pl.pallas_call(
        functools.partial(_in_proj_kernel, nb=nb),
        grid=(t // tm, nb + nf),
        in_specs=[
            pl.BlockSpec((tm, d), lambda i, j: (i, 0)),
            pl.BlockSpec((1, d), lambda i, j: (0, 0)),
            pl.BlockSpec((d, tn), lambda i, j: (0, j)),
            pl.BlockSpec((1, tn), lambda i, j: (0, jnp.minimum(j, nb - 1))),
        ],
        out_specs=[pl.BlockSpec((tm, tn), lambda i, j: (i, jnp.minimum(j, nb - 1))),
                   pl.BlockSpec((tm, tn), lambda i, j: (i, jnp.maximum(j - nb, 0)))],
        out_shape=[jax.ShapeDtypeStruct((t, n_bf16), BF16), jax.ShapeDtypeStruct((t, n - n_bf16), F32)],
        scratch_shapes=[pltpu.VMEM((tm, d), BF16)],
        compiler_params=_params("parallel", "arbitrary"),
        name="in_proj",
    )(x, g, w, col_scale)


def _mla_proj_kernel(zf_ref, gq_ref, gkv_ref, wq_ref, wkv_ref, c_ref, s_ref, q_ref, k_ref, v_ref, *,
                     q_rank, kv_rank, q_scale):
    zf = zf_ref[...]
    cos, sin = c_ref[...], s_ref[...]
    qn = _rms(zf[:, :q_rank], gq_ref[...]).astype(BF16)
    qa = jnp.dot(qn, wq_ref[...], preferred_element_type=F32)
    kvn = _rms(zf[:, q_rank:q_rank + kv_rank], gkv_ref[...]).astype(BF16)
    kva = jnp.dot(kvn, wkv_ref[...], preferred_element_type=F32)
    o = q_rank + kv_rank
    kpe = (zf[:, o:o + LANES] * cos + zf[:, o + LANES:o + 2 * LANES] * sin).astype(BF16)
    hp = MLA_HEADS * LANES
    for h in range(MLA_HEADS):
        lo = h * LANES
        rope = qa[:, hp + lo:hp + lo + LANES] * cos + qa[:, 2 * hp + lo:2 * hp + lo + LANES] * sin
        q_ref[:, 2 * lo:2 * lo + LANES] = (qa[:, lo:lo + LANES] * q_scale).astype(BF16)
        q_ref[:, 2 * lo + LANES:2 * lo + 2 * LANES] = (rope * q_scale).astype(BF16)
        k_ref[:, 2 * lo:2 * lo + LANES] = kva[:, 2 * lo:2 * lo + LANES].astype(BF16)
        k_ref[:, 2 * lo + LANES:2 * lo + 2 * LANES] = kpe
        v_ref[:, lo:lo + LANES] = kva[:, 2 * lo + LANES:2 * lo + 2 * LANES].astype(BF16)


def _mla_proj(zf, gq, gkv, wq, wkv, rope_c, rope_s, seq, tm=512):
    t, zw = zf.shape
    tm = _tile(seq, tm)
    nl = seq // tm
    q_rank, kv_rank = gq.shape[1], gkv.shape[1]
    kern = functools.partial(_mla_proj_kernel, q_rank=q_rank, kv_rank=kv_rank,
                             q_scale=(MLA_NOPE + MLA_ROPE) ** -0.5 * LOG2E)
    qk_w = MLA_HEADS * MLA_QK_PAD
    return pl.pallas_call(
        kern,
        grid=(t // tm,),
        in_specs=[
            pl.BlockSpec((tm, zw), lambda i: (i, 0)),
            pl.BlockSpec((1, q_rank), lambda i: (0, 0)),
            pl.BlockSpec((1, kv_rank), lambda i: (0, 0)),
            pl.BlockSpec(wq.shape, lambda i: (0, 0)),
            pl.BlockSpec(wkv.shape, lambda i: (0, 0)),
            pl.BlockSpec((tm, LANES), lambda i: (i % nl, 0)),
            pl.BlockSpec((tm, LANES), lambda i: (i % nl, 0)),
        ],
        out_specs=[
            pl.BlockSpec((tm, qk_w), lambda i: (i, 0)),
            pl.BlockSpec((tm, qk_w), lambda i: (i, 0)),
            pl.BlockSpec((tm, MLA_HEADS * MLA_V), lambda i: (i, 0)),
        ],
        out_shape=[
            jax.ShapeDtypeStruct((t, qk_w), BF16),
            jax.ShapeDtypeStruct((t, qk_w), BF16),
            jax.ShapeDtypeStruct((t, MLA_HEADS * MLA_V), BF16),
        ],
        compiler_params=_params("parallel"),
        name="mla_proj",
    )(zf, gq, gkv, wq, wkv, rope_c, rope_s)


def _online_softmax_step(s, v, m_ref, acc_ref, shift):
    m_prev = m_ref[...]
    m_cur = jnp.max(s, axis=1, keepdims=True)
    if shift is not None:
        m_cur = m_cur + shift
    m_next = jnp.maximum(m_prev, m_cur)
    alpha = jnp.exp2(m_prev - m_next)
    m_sub = m_next if shift is None else m_next - shift
    p = jnp.exp2(s - _lane_tile(m_sub, s.shape[1] // LANES))
    v1 = jnp.concatenate([v, jnp.ones_like(v)], axis=1)
    acc_ref[...] = _lane_tile(alpha, 2) * acc_ref[...] + jnp.dot(p.astype(BF16), v1, preferred_element_type=F32)
    m_ref[...] = m_next


def _init_softmax_state(m_ref, acc_ref):
    m_ref[...] = jnp.full(m_ref.shape, MASK_VALUE, F32)
    acc_ref[...] = jnp.zeros(acc_ref.shape, F32)


def _softmax_result(acc_ref):
    acc = acc_ref[...]
    return acc[:, :LANES] / acc[:, LANES:]


def _qk(q, k):
    return lax.dot_general(q, k, (((1,), (1,)), ((), ())), preferred_element_type=F32)


def _mla_flash_kernel(q_ref, k_ref, v_ref, o_ref, m_ref, acc_ref, *, tkc, unroll):
    _init_softmax_state(m_ref, acc_ref)
    q = q_ref[...]

    def block(bi, carry):
        for c in range(unroll):
            off = pl.multiple_of((bi * unroll + c) * tkc, tkc)
            s = _qk(q, k_ref[pl.ds(off, tkc), :])
            _online_softmax_step(s, v_ref[pl.ds(off, tkc), :], m_ref, acc_ref, None)
        return carry

    lax.fori_loop(0, k_ref.shape[0] // (tkc * unroll), block, 0)
    o_ref[...] = _softmax_result(acc_ref).astype(o_ref.dtype)


def _mla_flash(q, k, v, batch, seq, tq=1024, tkc=512, unroll=16):
    t = q.shape[0]
    tq, tkc = _tile(seq, tq), _tile(seq, tkc)
    unroll = _tile(seq // tkc, unroll)
    nq = seq // tq
    return pl.pallas_call(
        functools.partial(_mla_flash_kernel, tkc=tkc, unroll=unroll),
        grid=(MLA_HEADS, batch, nq),
        in_specs=[
            pl.BlockSpec((tq, MLA_QK_PAD), lambda g, b, qi: (b * nq + qi, g)),
            pl.BlockSpec((seq, MLA_QK_PAD), lambda g, b, qi: (b, g)),
            pl.BlockSpec((seq, MLA_V), lambda g, b, qi: (b, g)),
        ],
        out_specs=pl.BlockSpec((tq, MLA_V), lambda g, b, qi: (b * nq + qi, g)),
        out_shape=jax.ShapeDtypeStruct((t, MLA_HEADS * MLA_V), BF16),
        scratch_shapes=[pltpu.VMEM((tq, LANES), F32), pltpu.VMEM((tq, MLA_V + LANES), F32)],
        compiler_params=_params("parallel", "parallel", "parallel"),
        name="mla_flash",
    )(q, k, v)


def _key_norm_kernel(k_ref, o_ref):
    @pl.when(pl.program_id(2) == 0)
    def _():
        o_ref[...] = jnp.zeros(o_ref.shape, F32)

    k = k_ref[...].astype(F32)
    top = jnp.max(jnp.sum(k * k, axis=1, keepdims=True), axis=0, keepdims=True)
    o_ref[...] = jnp.maximum(o_ref[...], top)


def _max_key_norm_sq(zb, kc, batch, seq, tk=2048):
    tk = _tile(seq, tk)
    nk = seq // tk
    return pl.pallas_call(
        _key_norm_kernel,
        grid=(batch, DIFF_HEADS, nk),
        in_specs=[pl.BlockSpec((tk, LANES), lambda b, g, i: (b * nk + i, kc + g))],
        out_specs=pl.BlockSpec((8, LANES), lambda b, g, i: (b * DIFF_HEADS + g, 0)),
        out_shape=jax.ShapeDtypeStruct((batch * DIFF_HEADS * 8, LANES), F32),
        compiler_params=_params("parallel", "parallel", "arbitrary"),
        name="key_norm",
    )(zb)


def _diff_flash_kernel(slope_ref, kn_ref, lq1_ref, lk1_ref, lq2_ref, lk2_ref, gsub_ref, q_ref, k_ref, v_ref, o_ref,
                       qs_ref, m_ref, acc_ref, bias_ref, cap_ref, qx_ref, kx_ref, *, tq, tkb, lam_init):
    g, qi = pl.program_id(0), pl.program_id(2)
    slope = slope_ref[g]
    nk = k_ref.shape[0] // tkb
    nch = tkb // tq
    q0 = qi * tq

    @pl.when(qi == 0)
    def _():
        shape = (2 * tq, tq)
        row = lax.broadcasted_iota(jnp.int32, shape, 0)
        row = jnp.where(row >= tq, row - tq, row)
        d = (row - lax.broadcasted_iota(jnp.int32, shape, 1)).astype(F32)
        bias_ref[0] = -slope * d
        bias_ref[1] = -slope * jnp.abs(d)
        bias_ref[2] = slope * d
        whole = jnp.full((1, LANES), slope, F32)
        p1 = whole.astype(BF16).astype(F32)
        p2 = (whole - p1).astype(BF16).astype(F32)
        p3 = (whole - p1 - p2).astype(BF16).astype(F32)
        lane = lax.broadcasted_iota(jnp.int32, (tq, LANES), 1)
        pos = lax.broadcasted_iota(jnp.int32, (tq, LANES), 0)
        even = (lane & 1) == 0
        digit = jnp.where(even, pos >> 4, pos & 15).astype(F32)
        pair = jnp.where(lane >= 6, lane - 6, lane) >> 1
        piece = jnp.where(pair == 0, p1, jnp.where(pair == 1, p2, p3))
        weight = jnp.where(even, 16.0 * piece, piece)
        kx_ref[...] = jnp.where(lane < 6, weight, jnp.where(lane < 12, digit, 0.0)).astype(BF16)
        after = jnp.where(lane < 6, digit, jnp.where(lane < 12, -weight, 0.0)).astype(BF16)
        for half in range(2):
            qx_ref[0, half * tq:(half + 1) * tq, LANES:] = -after
            qx_ref[1, half * tq:(half + 1) * tq, LANES:] = after

    _init_softmax_state(m_ref, acc_ref)
    q = q_ref[...]
    lane = lax.broadcasted_iota(jnp.int32, q.shape, 1)
    qs_ref[:tq] = jnp.where(lane < DIFF_QK_DIM, q, jnp.zeros_like(q))
    qs_ref[tq:] = jnp.where(lane >= DIFF_QK_DIM, q, jnp.zeros_like(q))
    for side in range(2):
        qx_ref[side, :, :LANES] = qs_ref[...]
    qf = q.astype(F32)
    q_norm = jnp.sqrt(jnp.sum(qf * qf, axis=1, keepdims=True))
    cap = q_norm * jnp.sqrt(kn_ref[0:1, :]) * NORM_BOUND_SLACK + NORM_BOUND_SLACK
    cap_ref[:tq] = cap
    cap_ref[tq:] = cap

    unrolled_from = min(nch, (5 * nch + 7) // 8)

    def block(ki, far):
        kb0 = ((q0 // tkb + ki) % nk) * tkb
        best = jnp.max(cap_ref[...] - m_ref[...], axis=0, keepdims=True)
        cidx = lax.broadcasted_iota(jnp.int32, (8, LANES), 0)
        kc0 = kb0 + cidx * tq
        nearest = jnp.maximum(jnp.maximum(q0 - (kc0 + tq) + 1, kc0 - (q0 + tq) + 1), 0)
        live = (cidx < nch) & (best >= slope * nearest.astype(F32) - EXP2_UNDERFLOW)
        n_live = jnp.sum(jnp.where(live, 1, 0), axis=0, keepdims=True)[0, 0]
        first_live = jnp.min(jnp.where(live, cidx, nch), axis=0, keepdims=True)[0, 0]

        def chunk(c):
            k0 = pl.multiple_of(kb0 + c * tq, tq)
            gap = jnp.full((1, LANES), jnp.abs(q0 - k0), jnp.int32).astype(F32)
            if far:
                keys = jnp.concatenate([k_ref[pl.ds(k0, tq), :], kx_ref[...]], axis=1)
                s = _qk(qx_ref[jnp.where(k0 < q0, 0, 1)], keys)
            else:
                which = jnp.where(k0 < q0, 0, jnp.where(k0 == q0, 1, 2))
                s = _qk(qs_ref[...], k_ref[pl.ds(k0, tq), :]) + bias_ref[which]
            _online_softmax_step(s, v_ref[pl.ds(k0, tq), :], m_ref, acc_ref, -slope * gap)

        @pl.when(n_live >= unrolled_from)
        def _():
            for c in range(nch):
                chunk(c)

        @pl.when((n_live > 0) & (n_live < unrolled_from))
        def _():
            def body(c, inner):
                chunk(c)
                return inner

            lax.fori_loop(first_live, first_live + n_live, body, 0)

    block(0, far=False)

    def far_block(ki, carry):
        block(ki, far=True)
        return carry

    lax.fori_loop(1, nk, far_block, 0)

    o = _softmax_result(acc_ref)
    lam = (jnp.exp(jnp.sum(lq1_ref[...] * lk1_ref[...], axis=-1, keepdims=True))
           - jnp.exp(jnp.sum(lq2_ref[...] * lk2_ref[...], axis=-1, keepdims=True)) + lam_init)
    d = o[:tq] - lam * o[tq:]
    o_ref[...] = (_rms(d, gsub_ref[...]) * (1.0 - lam_init)).astype(o_ref.dtype)


def _diff_flash(zb, col0, slopes, lq1, lk1, lq2, lk2, gsub, lam_init, batch, seq, tq=512, tkb=4096):
    t = zb.shape[0]
    tq = _tile(seq, tq)
    tkb = _tile(seq, tkb)
    assert tkb % tq == 0 and tkb // tq <= 8
    nq = seq // tq
    qc, kc, vc = col0
    key_norm = _max_key_norm_sq(zb, kc, batch, seq)
    vec = lambda a: pl.BlockSpec(a.shape, lambda g, b, qi: (0, 0))
    return pl.pallas_call(
        functools.partial(_diff_flash_kernel, tq=tq, tkb=tkb, lam_init=lam_init),
        grid=(DIFF_HEADS, batch, nq),
        in_specs=[
            pl.BlockSpec(memory_space=pltpu.SMEM),
            pl.BlockSpec((8, LANES), lambda g, b, qi: (b * DIFF_HEADS + g, 0)),
            vec(lq1), vec(lk1), vec(lq2), vec(lk2), vec(gsub),
            pl.BlockSpec((tq, LANES), lambda g, b, qi: (b * nq + qi, qc + g)),
            pl.BlockSpec((seq, LANES), lambda g, b, qi: (b, kc + g)),
            pl.BlockSpec((seq, DIFF_V_DIM), lambda g, b, qi: (b, vc + g)),
        ],
        out_specs=pl.BlockSpec((tq, DIFF_V_DIM), lambda g, b, qi: (b * nq + qi, g)),
        out_shape=jax.ShapeDtypeStruct((t, DIFF_HEADS * DIFF_V_DIM), BF16),
        scratch_shapes=[pltpu.VMEM((2 * tq, LANES), BF16), pltpu.VMEM((2 * tq, LANES), F32),
                        pltpu.VMEM((2 * tq, DIFF_V_DIM + LANES), F32), pltpu.VMEM((3, 2 * tq, tq), F32),
                        pltpu.VMEM((2 * tq, LANES), F32), pltpu.VMEM((2, 2 * tq, 2 * LANES), BF16),
                        pltpu.VMEM((tq, LANES), BF16)],
        compiler_params=_params("parallel", "parallel", "arbitrary"),
        name="diff_flash",
    )(slopes, key_norm, lq1, lk1, lq2, lk2, gsub, zb, zb, zb)


def _na_kernel(q_ref, kp_ref, kc_ref, kn_ref, vp_ref, vc_ref, vn_ref, bias_ref, o_ref, kb_ref, vb_ref, *, rows):
    j = pl.program_id(2)
    blk = NA_ROWS_PER_STEP * GRID_W
    for n, (kr, vr) in enumerate(((kp_ref, vp_ref), (kc_ref, vc_ref), (kn_ref, vn_ref))):
        kb_ref[n * blk:(n + 1) * blk] = kr[...]
        vb_ref[n * blk:(n + 1) * blk] = vr[...]
    n_groups = rows // NA_GROUP
    gq = NA_GROUP * GRID_W
    for gi in range(NA_ROWS_PER_STEP // NA_GROUP):
        group = j * (NA_ROWS_PER_STEP // NA_GROUP) + gi
        union_start = jnp.clip(group * NA_GROUP - NA_WIN_ROWS // 2, 0, rows - NA_UNION)
        start = pl.multiple_of((union_start - (j - 1) * NA_ROWS_PER_STEP) * GRID_W, GRID_W)
        variant = jnp.where(group == 0, 0, jnp.where(group == n_groups - 1, 2, 1))
        q = q_ref[gi * gq:(gi + 1) * gq, :]
        s = _qk(q, kb_ref[pl.ds(start, NA_UNION * GRID_W), :]) + bias_ref[variant]
        p = jnp.exp2(s - jnp.max(s, axis=1, keepdims=True))
        v = vb_ref[pl.ds(start, NA_UNION * GRID_W), :]
        acc = jnp.dot(p.astype(BF16), jnp.concatenate([v, jnp.ones_like(v)], axis=1), preferred_element_type=F32)
        o_ref[gi * gq:(gi + 1) * gq, :] = (acc[:, :NA_HEAD_DIM] / acc[:, NA_HEAD_DIM:]).astype(o_ref.dtype)


def _na_attention(zb, col0, bias, batch, seq):
    t = zb.shape[0]
    rows = seq // GRID_W
    assert rows % NA_ROWS_PER_STEP == 0 and rows >= 2 * NA_ROWS_PER_STEP
    nj = rows // NA_ROWS_PER_STEP
    blk = NA_ROWS_PER_STEP * GRID_W
    qc, kc, vc = col0

    def spec(c0, dj):
        return pl.BlockSpec((blk, NA_HEAD_DIM),
                            lambda h, b, j: (b * nj + jnp.clip(j + dj, 0, nj - 1), c0 + h))

    return pl.pallas_call(
        functools.partial(_na_kernel, rows=rows),
        grid=(NA_HEADS, batch, nj),
        in_specs=[spec(qc, 0), spec(kc, -1), spec(kc, 0), spec(kc, 1), spec(vc, -1), spec(vc, 0), spec(vc, 1),
                  pl.BlockSpec((None,) + bias.shape[1:], lambda h, b, j: (h, 0, 0, 0))],
        out_specs=pl.BlockSpec((blk, NA_HEAD_DIM), lambda h, b, j: (b * nj + j, h)),
        out_shape=jax.ShapeDtypeStruct((t, NA_HEADS * NA_HEAD_DIM), BF16),
        scratch_shapes=[pltpu.VMEM((3 * blk, NA_HEAD_DIM), BF16), pltpu.VMEM((3 * blk, NA_HEAD_DIM), BF16)],
        compiler_params=_params("parallel", "parallel", "parallel"),
        name="na_attention",
    )(zb, zb, zb, zb, zb, zb, zb, bias)


def _na_bias_table(rpb):
    col = np.arange(GRID_W)
    col_start = np.clip(col - NA_WIN_COLS // 2, 0, GRID_W - NA_WIN_COLS)
    kc = np.arange(GRID_W)
    valid = (kc[None, :] >= col_start[:, None]) & (kc[None, :] < col_start[:, None] + NA_WIN_COLS)
    by_row = jnp.stack([rpb[:, NA_WIN_ROWS - 1 - t:2 * NA_WIN_ROWS - 1 - t] for t in range(NA_WIN_ROWS)], axis=1)
    pad = GRID_W - NA_WIN_COLS
    padded = jnp.pad(by_row, ((0, 0), (0, 0), (0, 0), (pad, pad)))
    tbl = jnp.stack([padded[..., GRID_W - 1 - c:2 * GRID_W - 1 - c] for c in range(GRID_W)], axis=2)
    tbl = jnp.where(valid[None, None, :, None, :], tbl * LOG2E, MASK_VALUE)
    per_row = tbl.reshape(rpb.shape[0], NA_WIN_ROWS, GRID_W, NA_WIN_ROWS * GRID_W).astype(F32)

    extra = NA_UNION - NA_WIN_ROWS

    def place(t, offset):
        return jnp.pad(per_row[:, t], ((0, 0), (0, 0), (offset * GRID_W, (extra - offset) * GRID_W)),
                       constant_values=MASK_VALUE)

    half = NA_WIN_ROWS // 2
    first = jnp.concatenate([place(i, 0) for i in range(NA_GROUP)], axis=1)
    inner = jnp.concatenate([place(half, i) for i in range(NA_GROUP)], axis=1)
    last = jnp.concatenate([place(half + i, extra) for i in range(NA_GROUP)], axis=1)
    return jnp.stack([first, inner, last], axis=1)


def _out_proj_kernel(a_ref, b_ref, c_ref, wa_ref, wb_ref, wc_ref, h_ref, o_ref, *, tn):
    a, b, c = a_ref[...], b_ref[...], c_ref[...]
    for j in range(o_ref.shape[1] // tn):
        cols = slice(j * tn, (j + 1) * tn)
        acc = jnp.dot(a, wa_ref[:, cols], preferred_element_type=F32)
        acc = acc + jnp.dot(b, wb_ref[:, cols], preferred_element_type=F32)
        acc = acc + jnp.dot(c, wc_ref[:, cols], preferred_element_type=F32)
        o_ref[:, cols] = h_ref[:, cols] + acc


def _out_proj(a, b, c, wa, wb, wc, h, tm=512, tn=512):
    t, d = h.shape
    tm, tn = _tile(t, tm), _tile(d, tn)
    rows = lambda x: pl.BlockSpec((tm, x.shape[1]), lambda i: (i, 0))
    whole = lambda w: pl.BlockSpec(w.shape, lambda i: (0, 0))
    return pl.pallas_call(
        functools.partial(_out_proj_kernel, tn=tn),
        grid=(t // tm,),
        in_specs=[rows(a), rows(b), rows(c), whole(wa), whole(wb), whole(wc), rows(h)],
        out_specs=rows(h),
        out_shape=jax.ShapeDtypeStruct((t, d), F32),
        compiler_params=_params("parallel"),
        name="out_proj",
    )(a, b, c, wa, wb, wc, h)


def _ffn_kernel(h_ref, hp_ref, hn_ref, g_ref, wg_ref, wu_ref, cw_ref, cb_ref, wd_ref, o_ref, xn_ref, *,
                tm, tiles_per_seq):
    i, j = pl.program_id(0), pl.program_id(1)

    @pl.when(j == 0)
    def _():
        x = h_ref[...]
        g = g_ref[...]
        xn_ref[:tm] = _rms(x, g).astype(BF16)
        pos = i % tiles_per_seq
        before = jnp.where(pos == 0, 0.0, _rms(hp_ref[...], g)[BF16_ROWS - 1:BF16_ROWS])
        after = jnp.where(pos == tiles_per_seq - 1, 0.0, _rms(hn_ref[...], g)[0:1])
        rid = lax.broadcasted_iota(jnp.int32, (BF16_ROWS, x.shape[1]), 0)
        halo = jnp.where(rid == 0, before, jnp.where(rid == 1, after, 0.0))
        xn_ref[tm:] = halo.astype(BF16)
        o_ref[...] = x

    ga = jnp.dot(xn_ref[...], wg_ref[...], preferred_element_type=F32)
    up = jnp.dot(xn_ref[:tm], wu_ref[...], preferred_element_type=F32)
    gate = ga[:tm]
    rid = lax.broadcasted_iota(jnp.int32, gate.shape, 0)
    g_prev = jnp.where(rid == 0, ga[tm:tm + 1], pltpu.roll(gate, 1, 0))
    g_next = jnp.where(rid == tm - 1, ga[tm + 1:tm + 2], pltpu.roll(gate, tm - 1, 0))
    cw = cw_ref[...]
    conv = g_prev * cw[0:1] + gate * cw[1:2] + g_next * cw[2:3] + cb_ref[...]
    act = (jax.nn.gelu(conv) * up).astype(BF16)
    o_ref[...] += jnp.dot(act, wd_ref[...], preferred_element_type=F32)


def _ffn(h, g, w_gu, conv_w, conv_b, w_down, seq, tm=512, tf=512):
    t, d = h.shape
    f = w_down.shape[0]
    tm, tf = _tile(seq, tm), _tile(f, tf)
    nf = f // tf
    hb = tm // BF16_ROWS
    nhb = t // BF16_ROWS
    return pl.pallas_call(
        functools.partial(_ffn_kernel, tm=tm, tiles_per_seq=seq // tm),
        grid=(t // tm, nf),
        in_specs=[
            pl.BlockSpec((tm, d), lambda i, j: (i, 0)),
            pl.BlockSpec((BF16_ROWS, d), lambda i, j: (jnp.maximum(i * hb - 1, 0), 0)),
            pl.BlockSpec((BF16_ROWS, d), lambda i, j: (jnp.minimum((i + 1) * hb, nhb - 1), 0)),
            pl.BlockSpec((1, d), lambda i, j: (0, 0)),
            pl.BlockSpec((d, tf), lambda i, j: (0, j)),
            pl.BlockSpec((d, tf), lambda i, j: (0, nf + j)),
            pl.BlockSpec((conv_w.shape[0], tf), lambda i, j: (0, j)),
            pl.BlockSpec((1, tf), lambda i, j: (0, j)),
            pl.BlockSpec((tf, d), lambda i, j: (j, 0)),
        ],
        out_specs=pl.BlockSpec((tm, d), lambda i, j: (i, 0)),
        out_shape=jax.ShapeDtypeStruct((t, d), F32),
        scratch_shapes=[pltpu.VMEM((tm + BF16_ROWS, d), BF16)],
        compiler_params=_params("parallel", "arbitrary"),
        name="ffn",
    )(h, h, h, g, w_gu, w_gu, conv_w, conv_b, w_down)


def _ple_kernel(h_ref, p_ref, g_ref, wg_ref, wp_ref, gf_ref, o_ref, *, tn, final_norm):
    h = h_ref[...]
    xn = _rms(h, g_ref[...]).astype(BF16)
    pb = p_ref[...].astype(BF16)
    for j in range(h.shape[1] // tn):
        cols = slice(j * tn, (j + 1) * tn)
        gate = jax.nn.sigmoid(jnp.dot(xn, wg_ref[:, cols], preferred_element_type=F32))
        emb = jnp.dot(pb, wp_ref[:, cols], preferred_element_type=F32)
        o_ref[:, cols] = h[:, cols] + emb * gate
    if final_norm:
        o_ref[...] = _rms(o_ref[...], gf_ref[...])


def _ple(h, p, g, w_plg, w_ple, g_final, final_norm, tm=512, tn=512):
    t, d = h.shape
    tm, tn = _tile(t, tm), _tile(d, tn)
    whole = lambda a: pl.BlockSpec(a.shape, lambda i: (0, 0))
    return pl.pallas_call(
        functools.partial(_ple_kernel, tn=tn, final_norm=final_norm),
        grid=(t // tm,),
        in_specs=[
            pl.BlockSpec((tm, d), lambda i: (i, 0)),
            pl.BlockSpec((tm, p.shape[1]), lambda i: (i, 0)),
            whole(g), whole(w_plg), whole(w_ple), whole(g_final),
        ],
        out_specs=pl.BlockSpec((tm, d), lambda i: (i, 0)),
        out_shape=jax.ShapeDtypeStruct((t, d), F32),
        compiler_params=_params("parallel"),
        name="ple",
    )(h, p, g, w_plg, w_ple, g_final)


def _swap_halves(x):
    half = x.shape[-1] // 2
    return jnp.concatenate([x[..., half:], x[..., :half]], axis=-1)


def _pad_lanes(x):
    return jnp.concatenate([x, jnp.zeros(x.shape[:-1] + (LANES - x.shape[-1],), x.dtype)], axis=-1)


def _prepare_layer(i, w_in, g_qa, w_qb, g_kva, w_kvb, rpb, w_out, w_gu, w_down, w_plg, w_ple):
    d = w_in.shape[0]
    q_rank, kv_rank = g_qa.shape[0], g_kva.shape[0]
    o = q_rank + kv_rank
    kpe = w_in[:, o:o + MLA_ROPE]
    w_f = jnp.concatenate([w_in[:, :o], _pad_lanes(kpe), _pad_lanes(_swap_halves(kpe))], axis=1)
    w_b = w_in[:, o + MLA_ROPE:]
    na_w = NA_HEADS * NA_HEAD_DIM
    diff_w = DIFF_HEADS * 2 * DIFF_QK_DIM
    scale_b = np.ones((1, w_b.shape[1]), np.float32)
    scale_b[:, :na_w] = NA_HEAD_DIM ** -0.5 * LOG2E
    scale_b[:, 3 * na_w:3 * na_w + diff_w] = DIFF_QK_DIM ** -0.5 * LOG2E
    wq = w_qb.reshape(q_rank, MLA_HEADS, MLA_NOPE + MLA_ROPE)
    pe = wq[..., MLA_NOPE:]
    wq = jnp.concatenate([wq[..., :MLA_NOPE].reshape(q_rank, -1), _pad_lanes(pe).reshape(q_rank, -1),
                          _pad_lanes(_swap_halves(pe)).reshape(q_rank, -1)], axis=1)
    mla_w = MLA_HEADS * MLA_V
    return dict(
        w_in=jnp.concatenate([w_b, w_f], axis=1).astype(BF16), n_bf16=w_b.shape[1], scale_b=jnp.asarray(scale_b),
        wq=wq.astype(BF16), wkv=w_kvb.astype(BF16),
        na_bias=_na_bias_table(rpb),
        wo_mla=w_out[:mla_w].astype(BF16), wo_na=w_out[mla_w:mla_w + na_w].astype(BF16),
        wo_diff=w_out[mla_w + na_w:].astype(BF16),
        w_gu=w_gu.astype(BF16), w_down=w_down.astype(BF16),
        w_plg=w_plg.astype(BF16), w_ple=w_ple.astype(BF16),
        lam_init=0.8 - 0.6 * math.exp(-0.3 * i),
    )


def _rope_tables(seq):
    half = MLA_ROPE // 2
    freqs = jnp.power(ROPE_THETA, -jnp.arange(half, dtype=F32) / half)
    ang = jnp.arange(seq, dtype=jnp.int32).astype(F32)[:, None] * freqs[None, :]
    cos, sin = jnp.cos(ang), jnp.sin(ang)
    return _pad_lanes(jnp.concatenate([cos, cos], axis=1)), _pad_lanes(jnp.concatenate([-sin, sin], axis=1))


def _trunk(x, p, layers, vecs, g_final):
    batch, seq, d = x.shape
    t = batch * seq
    h = x.reshape(t, d)
    rope_c, rope_s = _rope_tables(seq)
    slopes = jnp.exp2(-8.0 * jnp.arange(1, DIFF_HEADS + 1, dtype=F32) / DIFF_HEADS) * LOG2E
    nb = NA_HEADS * NA_HEAD_DIM // LANES
    for i, (lw, lv) in enumerate(zip(layers, vecs)):
        zb, zf = _in_proj(h, lv["g_attn"], lw["w_in"], lw["scale_b"], lw["n_bf16"])
        q, k, v = _mla_proj(zf, lv["g_qa"], lv["g_kva"], lw["wq"], lw["wkv"], rope_c, rope_s, seq)
        o_mla = _mla_flash(q, k, v, batch, seq)
        o_na = _na_attention(zb, (0, nb, 2 * nb), lw["na_bias"], batch, seq)
        o_diff = _diff_flash(zb, (3 * nb, 4 * nb, 5 * nb), slopes, lv["lam_q1"], lv["lam_k1"], lv["lam_q2"],
                             lv["lam_k2"], lv["g_subln"], lw["lam_init"], batch, seq)
        h = _out_proj(o_mla, o_na, o_diff, lw["wo_mla"], lw["wo_na"], lw["wo_diff"], h)
        h = _ffn(h, lv["g_ffn"], lw["w_gu"], lv["conv_w"], lv["conv_b"], lw["w_down"], seq)
        h = _ple(h, p[i].reshape(t, -1), lv["g_ple"], lw["w_plg"], lw["w_ple"], g_final.reshape(1, d),
                 final_norm=i == len(layers) - 1)
    return h.reshape(batch, seq, d)


def kernel(x_prompt, x_sample, p_prompt, p_sample, g_attn, w_in, g_qa, w_qb, g_kva, w_kvb, rpb, lam_q1, lam_k1,
           lam_q2, lam_k2, g_subln, w_out, g_ffn, w_gu, conv_w, conv_b, w_down, g_ple, w_plg, w_ple, g_final):
    depth = w_in.shape[0]
    layers = [_prepare_layer(i, w_in[i], g_qa[i], w_qb[i], g_kva[i], w_kvb[i], rpb[i], w_out[i], w_gu[i],
                             w_down[i], w_plg[i], w_ple[i]) for i in range(depth)]
    row = lambda a: a.reshape(1, -1)
    vecs = [dict(g_attn=row(g_attn[i]), g_qa=row(g_qa[i]), g_kva=row(g_kva[i]), lam_q1=row(lam_q1[i]),
                 lam_k1=row(lam_k1[i]), lam_q2=row(lam_q2[i]), lam_k2=row(lam_k2[i]), g_subln=row(g_subln[i]),
                 g_ffn=row(g_ffn[i]), conv_w=conv_w[i], conv_b=row(conv_b[i]), g_ple=row(g_ple[i]))
            for i in range(depth)]
    return (_trunk(x_prompt, p_prompt, layers, vecs, g_final), _trunk(x_sample, p_sample, layers, vecs, g_final))
```

```python
import functools
import math

import jax
import jax.numpy as jnp
import numpy as np
from jax import lax
from jax.experimental import pallas as pl
from jax.experimental.pallas import tpu as pltpu

F32 = jnp.float32
BF16 = jnp.bfloat16

GRID_W = 64
MLA_HEADS = 8
MLA_NOPE = 128
MLA_ROPE = 64
MLA_V = 128
MLA_QK_PAD = 256
ROPE_THETA = 10000.0
NA_HEADS = 4
NA_HEAD_DIM = 128
NA_WIN_ROWS = 8
NA_WIN_COLS = 16
NA_ROWS_PER_STEP = 32
NA_GROUP = 4
NA_UNION = NA_WIN_ROWS + NA_GROUP - 1
DIFF_HEADS = 4
DIFF_QK_DIM = 64
DIFF_V_DIM = 128
RMS_EPS = 1e-6
LOG2E = 1.4426950408889634
MASK_VALUE = -1e30
EXP2_UNDERFLOW = 160.0
NORM_BOUND_SLACK = 1.001
LANES = 128
BF16_ROWS = 16
VMEM_LIMIT = 56 * 1024 * 1024


def _tile(n, pref):
    t = min(n, pref)
    assert n % t == 0, (n, pref)
    return t


def _rms(x, g):
    ms = jnp.mean(x * x, axis=-1, keepdims=True)
    return x * lax.rsqrt(ms + RMS_EPS) * g


def _lane_tile(x, n):
    return jnp.concatenate([x] * n, axis=1)


def _params(*sem):
    return pltpu.CompilerParams(dimension_semantics=sem, vmem_limit_bytes=VMEM_LIMIT)


def _in_proj_kernel(x_ref, g_ref, w_ref, cs_ref, zb_ref, zf_ref, xn_ref, *, nb):
    j = pl.program_id(1)

    @pl.when(j == 0)
    def _():
        xn_ref[...] = _rms(x_ref[...], g_ref[...]).astype(BF16)

    acc = jnp.dot(xn_ref[...], w_ref[...], preferred_element_type=F32)

    @pl.when(j < nb)
    def _():
        zb_ref[...] = (acc * cs_ref[...]).astype(BF16)

    @pl.when(j >= nb)
    def _():
        zf_ref[...] = acc


def _in_proj(x, g, w, col_scale, n_bf16, tm=1024, tn=1024):
    t, d = x.shape
    n = w.shape[1]
    tm, tn = _tile(t, tm), _tile(math.gcd(n_bf16, n - n_bf16), tn)
    nb, nf = n_bf16 // tn, (n - n_bf16) // tn
    return pl.pallas_call(
        functools.partial(_in_proj_kernel, nb=nb),
        grid=(t // tm, nb + nf),
        in_specs=[
            pl.BlockSpec((tm, d), lambda i, j: (i, 0)),
            pl.BlockSpec((1, d), lambda i, j: (0, 0)),
            pl.BlockSpec((d, tn), lambda i, j: (0, j)),
            pl.BlockSpec((1, tn), lambda i, j: (0, jnp.minimum(j, nb - 1))),
        ],
        out_specs=[pl.BlockSpec((tm, tn), lambda i, j: (i, jnp.minimum(j, nb - 1))),
                   pl.BlockSpec((tm, tn), lambda i, j: (i, jnp.maximum(j - nb, 0)))],
        out_shape=[jax.ShapeDtypeStruct((t, n_bf16), BF16), jax.ShapeDtypeStruct((t, n - n_bf16), F32)],
        scratch_shapes=[pltpu.VMEM((tm, d), BF16)],
        compiler_params=_params("parallel", "arbitrary"),
        name="in_proj",
    )(x, g, w, col_scale)


def _mla_proj_kernel(zf_ref, gq_ref, gkv_ref, wq_ref, wkv_ref, c_ref, s_ref, q_ref, k_ref, v_ref, *,
                     q_rank, kv_rank, q_scale):
    zf = zf_ref[...]
    cos, sin = c_ref[...], s_ref[...]
    qn = _rms(zf[:, :q_rank], gq_ref[...]).astype(BF16)
    qa = jnp.dot(qn, wq_ref[...], preferred_element_type=F32)
    kvn = _rms(zf[:, q_rank:q_rank + kv_rank], gkv_ref[...]).astype(BF16)
    kva = jnp.dot(kvn, wkv_ref[...], preferred_element_type=F32)
    o = q_rank + kv_rank
    kpe = (zf[:, o:o + LANES] * cos + zf[:, o + LANES:o + 2 * LANES] * sin).astype(BF16)
    hp = MLA_HEADS * LANES
    for h in range(MLA_HEADS):
        lo = h * LANES
        rope = qa[:, hp + lo:hp + lo + LANES] * cos + qa[:, 2 * hp + lo:2 * hp + lo + LANES] * sin
        q_ref[:, 2 * lo:2 * lo + LANES] = (qa[:, lo:lo + LANES] * q_scale).astype(BF16)
        q_ref[:, 2 * lo + LANES:2 * lo + 2 * LANES] = (rope * q_scale).astype(BF16)
        k_ref[:, 2 * lo:2 * lo + LANES] = kva[:, 2 * lo:2 * lo + LANES].astype(BF16)
        k_ref[:, 2 * lo + LANES:2 * lo + 2 * LANES] = kpe
        v_ref[:, lo:lo + LANES] = kva[:, 2 * lo + LANES:2 * lo + 2 * LANES].astype(BF16)


def _mla_proj(zf, gq, gkv, wq, wkv, rope_c, rope_s, seq, tm=512):
    t, zw = zf.shape
    tm = _tile(seq, tm)
    nl = seq // tm
    q_rank, kv_rank = gq.shape[1], gkv.shape[1]
    kern = functools.partial(_mla_proj_kernel, q_rank=q_rank, kv_rank=kv_rank,
                             q_scale=(MLA_NOPE + MLA_ROPE) ** -0.5 * LOG2E)
    qk_w = MLA_HEADS * MLA_QK_PAD
    return pl.pallas_call(
        kern,
        grid=(t // tm,),
        in_specs=[
            pl.BlockSpec((tm, zw), lambda i: (i, 0)),
            pl.BlockSpec((1, q_rank), lambda i: (0, 0)),
            pl.BlockSpec((1, kv_rank), lambda i: (0, 0)),
            pl.BlockSpec(wq.shape, lambda i: (0, 0)),
            pl.BlockSpec(wkv.shape, lambda i: (0, 0)),
            pl.BlockSpec((tm, LANES), lambda i: (i % nl, 0)),
            pl.BlockSpec((tm, LANES), lambda i: (i % nl, 0)),
        ],
        out_specs=[
            pl.BlockSpec((tm, qk_w), lambda i: (i, 0)),
            pl.BlockSpec((tm, qk_w), lambda i: (i, 0)),
            pl.BlockSpec((tm, MLA_HEADS * MLA_V), lambda i: (i, 0)),
        ],
        out_shape=[
            jax.ShapeDtypeStruct((t, qk_w), BF16),
            jax.ShapeDtypeStruct((t, qk_w), BF16),
            jax.ShapeDtypeStruct((t, MLA_HEADS * MLA_V), BF16),
        ],
        compiler_params=_params("parallel"),
        name="mla_proj",
    )(zf, gq, gkv, wq, wkv, rope_c, rope_s)


def _online_softmax_step(s, v, m_ref, acc_ref, shift):
    m_prev = m_ref[...]
    m_cur = jnp.max(s, axis=1, keepdims=True)
    if shift is not None:
        m_cur = m_cur + shift
    m_next = jnp.maximum(m_prev, m_cur)
    alpha = jnp.exp2(m_prev - m_next)
    m_sub = m_next if shift is None else m_next - shift
    p = jnp.exp2(s - _lane_tile(m_sub, s.shape[1] // LANES))
    v1 = jnp.concatenate([v, jnp.ones_like(v)], axis=1)
    acc_ref[...] = _lane_tile(alpha, 2) * acc_ref[...] + jnp.dot(p.astype(BF16), v1, preferred_element_type=F32)
    m_ref[...] = m_next


def _init_softmax_state(m_ref, acc_ref):
    m_ref[...] = jnp.full(m_ref.shape, MASK_VALUE, F32)
    acc_ref[...] = jnp.zeros(acc_ref.shape, F32)


def _softmax_result(acc_ref):
    acc = acc_ref[...]
    return acc[:, :LANES] / acc[:, LANES:]


def _qk(q, k):
    return lax.dot_general(q, k, (((1,), (1,)), ((), ())), preferred_element_type=F32)


def _mla_flash_kernel(q_ref, k_ref, v_ref, o_ref, m_ref, acc_ref, *, tkc, unroll):
    _init_softmax_state(m_ref, acc_ref)
    q = q_ref[...]

    def block(bi, carry):
        for c in range(unroll):
            off = pl.multiple_of((bi * unroll + c) * tkc, tkc)
            s = _qk(q, k_ref[pl.ds(off, tkc), :])
            _online_softmax_step(s, v_ref[pl.ds(off, tkc), :], m_ref, acc_ref, None)
        return carry

    lax.fori_loop(0, k_ref.shape[0] // (tkc * unroll), block, 0)
    o_ref[...] = _softmax_result(acc_ref).astype(o_ref.dtype)


def _mla_flash(q, k, v, batch, seq, tq=1024, tkc=512, unroll=16):
    t = q.shape[0]
    tq, tkc = _tile(seq, tq), _tile(seq, tkc)
    unroll = _tile(seq // tkc, unroll)
    nq = seq // tq
    return pl.pallas_call(
        functools.partial(_mla_flash_kernel, tkc=tkc, unroll=unroll),
        grid=(MLA_HEADS, batch, nq),
        in_specs=[
            pl.BlockSpec((tq, MLA_QK_PAD), lambda g, b, qi: (b * nq + qi, g)),
            pl.BlockSpec((seq, MLA_QK_PAD), lambda g, b, qi: (b, g)),
            pl.BlockSpec((seq, MLA_V), lambda g, b, qi: (b, g)),
        ],
        out_specs=pl.BlockSpec((tq, MLA_V), lambda g, b, qi: (b * nq + qi, g)),
        out_shape=jax.ShapeDtypeStruct((t, MLA_HEADS * MLA_V), BF16),
        scratch_shapes=[pltpu.VMEM((tq, LANES), F32), pltpu.VMEM((tq, MLA_V + LANES), F32)],
        compiler_params=_params("parallel", "parallel", "parallel"),
        name="mla_flash",
    )(q, k, v)


def _key_norm_kernel(k_ref, o_ref):
    @pl.when(pl.program_id(1) == 0)
    def _():
        o_ref[...] = jnp.zeros(o_ref.shape, F32)

    k = k_ref[...].astype(F32)
    sq = k * k
    for g in range(DIFF_HEADS):
        top = jnp.max(jnp.sum(sq[:, g * LANES:(g + 1) * LANES], axis=1, keepdims=True), axis=0, keepdims=True)
        o_ref[g * 8:(g + 1) * 8, :] = jnp.maximum(o_ref[g * 8:(g + 1) * 8, :], top)


def _max_key_norm_sq(zb, kc, batch, seq, tk=2048):
    tk = _tile(seq, tk)
    nk = seq // tk
    assert kc % DIFF_HEADS == 0
    return pl.pallas_call(
        _key_norm_kernel,
        grid=(batch, nk),
        in_specs=[pl.BlockSpec((tk, DIFF_HEADS * LANES), lambda b, i: (b * nk + i, kc // DIFF_HEADS))],
        out_specs=pl.BlockSpec((DIFF_HEADS * 8, LANES), lambda b, i: (b, 0)),
        out_shape=jax.ShapeDtypeStruct((batch * DIFF_HEADS * 8, LANES), F32),
        compiler_params=_params("parallel", "arbitrary"),
        name="key_norm",
    )(zb)


def _diff_flash_kernel(slope_ref, kn_ref, lq1_ref, lk1_ref, lq2_ref, lk2_ref, gsub_ref, q_ref, k_ref, v_ref, o_ref,
                       qs_ref, m_ref, acc_ref, bias_ref, cap_ref, qx_ref, kx_ref, *, tq, tkb, lam_init):
    g, qi = pl.program_id(0), pl.program_id(2)
    slope = slope_ref[g]
    nk = k_ref.shape[0] // tkb
    nch = tkb // tq
    q0 = qi * tq

    @pl.when((qi == 0) & (pl.program_id(1) == 0))
    def _():
        shape = (2 * tq, tq)
        row = lax.broadcasted_iota(jnp.int32, shape, 0)
        row = jnp.where(row >= tq, row - tq, row)
        d = (row - lax.broadcasted_iota(jnp.int32, shape, 1)).astype(F32)
        bias_ref[0] = -slope * d
        bias_ref[1] = -slope * jnp.abs(d)
        bias_ref[2] = slope * d
        whole = jnp.full((1, LANES), slope, F32)
        p1 = whole.astype(BF16).astype(F32)
        p2 = (whole - p1).astype(BF16).astype(F32)
        p3 = (whole - p1 - p2).astype(BF16).astype(F32)
        lane = lax.broadcasted_iota(jnp.int32, (tq, LANES), 1)
        pos = lax.broadcasted_iota(jnp.int32, (tq, LANES), 0)
        even = (lane & 1) == 0
        digit = jnp.where(even, pos >> 4, pos & 15).astype(F32)
        pair = jnp.where(lane >= 6, lane - 6, lane) >> 1
        piece = jnp.where(pair == 0, p1, jnp.where(pair == 1, p2, p3))
        weight = jnp.where(even, 16.0 * piece, piece)
        kx_ref[...] = jnp.where(lane < 6, weight, jnp.where(lane < 12, digit, 0.0)).astype(BF16)
        after = jnp.where(lane < 6, digit, jnp.where(lane < 12, -weight, 0.0)).astype(BF16)
        for half in range(2):
            qx_ref[0, half * tq:(half + 1) * tq, LANES:] = -after
            qx_ref[1, half * tq:(half + 1) * tq, LANES:] = after

    _init_softmax_state(m_ref, acc_ref)
    q = q_ref[...]
    lane = lax.broadcasted_iota(jnp.int32, q.shape, 1)
    qs_ref[:tq] = jnp.where(lane < DIFF_QK_DIM, q, jnp.zeros_like(q))
    qs_ref[tq:] = jnp.where(lane >= DIFF_QK_DIM, q, jnp.zeros_like(q))
    for side in range(2):
        qx_ref[side, :, :LANES] = qs_ref[...]
    qf = q.astype(F32)
    q_norm = jnp.sqrt(jnp.sum(qf * qf, axis=1, keepdims=True))
    cap = q_norm * jnp.sqrt(kn_ref[0:1, :]) * NORM_BOUND_SLACK + NORM_BOUND_SLACK
    cap_ref[:tq] = cap
    cap_ref[tq:] = cap

    unrolled_from = min(nch, (5 * nch + 7) // 8)

    def block(ki, far):
        kb0 = ((q0 // tkb + ki) % nk) * tkb
        best = jnp.max(cap_ref[...] - m_ref[...], axis=0, keepdims=True)
        cidx = lax.broadcasted_iota(jnp.int32, (8, LANES), 0)
        kc0 = kb0 + cidx * tq
        nearest = jnp.maximum(jnp.maximum(q0 - (kc0 + tq) + 1, kc0 - (q0 + tq) + 1), 0)
        live = (cidx < nch) & (best >= slope * nearest.astype(F32) - EXP2_UNDERFLOW)
        n_live = jnp.sum(jnp.where(live, 1, 0), axis=0, keepdims=True)[0, 0]
        first_live = jnp.min(jnp.where(live, cidx, nch), axis=0, keepdims=True)[0, 0]

        def chunk(c):
            k0 = pl.multiple_of(kb0 + c * tq, tq)
            gap = jnp.full((1, LANES), jnp.abs(q0 - k0), jnp.int32).astype(F32)
            if far:
                keys = jnp.concatenate([k_ref[pl.ds(k0, tq), :], kx_ref[...]], axis=1)
                s = _qk(qx_ref[jnp.where(k0 < q0, 0, 1)], keys)
            else:
                which = jnp.where(k0 < q0, 0, jnp.where(k0 == q0, 1, 2))
                s = _qk(qs_ref[...], k_ref[pl.ds(k0, tq), :]) + bias_ref[which]
            _online_softmax_step(s, v_ref[pl.ds(k0, tq), :], m_ref, acc_ref, -slope * gap)

        @pl.when(n_live >= unrolled_from)
        def _():
            for c in range(nch):
                chunk(c)

        @pl.when((n_live > 0) & (n_live < unrolled_from))
        def _():
            def body(c, inner):
                chunk(c)
                return inner

            lax.fori_loop(first_live, first_live + n_live, body, 0)

    block(0, far=False)

    def far_block(ki, carry):
        block(ki, far=True)
        return carry

    lax.fori_loop(1, nk, far_block, 0)

    o = _softmax_result(acc_ref)
    lam = (jnp.exp(jnp.sum(lq1_ref[...] * lk1_ref[...], axis=-1, keepdims=True))
           - jnp.exp(jnp.sum(lq2_ref[...] * lk2_ref[...], axis=-1, keepdims=True)) + lam_init)
    d = o[:tq] - lam * o[tq:]
    o_ref[...] = (_rms(d, gsub_ref[...]) * (1.0 - lam_init)).astype(o_ref.dtype)


def _diff_flash(zb, col0, slopes, lq1, lk1, lq2, lk2, gsub, lam_init, batch, seq, tq=512, tkb=4096):
    t = zb.shape[0]
    tq = _tile(seq, tq)
    tkb = _tile(seq, tkb)
    assert tkb % tq == 0 and tkb // tq <= 8
    nq = seq // tq
    qc, kc, vc = col0
    key_norm = _max_key_norm_sq(zb, kc, batch, seq)
    vec = lambda a: pl.BlockSpec(a.shape, lambda g, b, qi: (0, 0))
    return pl.pallas_call(
        functools.partial(_diff_flash_kernel, tq=tq, tkb=tkb, lam_init=lam_init),
        grid=(DIFF_HEADS, batch, nq),
        in_specs=[
            pl.BlockSpec(memory_space=pltpu.SMEM),
            pl.BlockSpec((8, LANES), lambda g, b, qi: (b * DIFF_HEADS + g, 0)),
            vec(lq1), vec(lk1), vec(lq2), vec(lk2), vec(gsub),
            pl.BlockSpec((tq, LANES), lambda g, b, qi: (b * nq + qi, qc + g)),
            pl.BlockSpec((seq, LANES), lambda g, b, qi: (b, kc + g)),
            pl.BlockSpec((seq, DIFF_V_DIM), lambda g, b, qi: (b, vc + g)),
        ],
        out_specs=pl.BlockSpec((tq, DIFF_V_DIM), lambda g, b, qi: (b * nq + qi, g)),
        out_shape=jax.ShapeDtypeStruct((t, DIFF_HEADS * DIFF_V_DIM), BF16),
        scratch_shapes=[pltpu.VMEM((2 * tq, LANES), BF16), pltpu.VMEM((2 * tq, LANES), F32),
                        pltpu.VMEM((2 * tq, DIFF_V_DIM + LANES), F32), pltpu.VMEM((3, 2 * tq, tq), F32),
                        pltpu.VMEM((2 * tq, LANES), F32), pltpu.VMEM((2, 2 * tq, 2 * LANES), BF16),
                        pltpu.VMEM((tq, LANES), BF16)],
        compiler_params=_params("parallel", "arbitrary", "arbitrary"),
        name="diff_flash",
    )(slopes, key_norm, lq1, lk1, lq2, lk2, gsub, zb, zb, zb)


def _na_kernel(q_ref, kp_ref, kc_ref, kn_ref, vp_ref, vc_ref, vn_ref, bias_ref, o_ref, kb_ref, vb_ref, *, rows):
    j = pl.program_id(2)
    blk = NA_ROWS_PER_STEP * GRID_W
    for n, (kr, vr) in enumerate(((kp_ref, vp_ref), (kc_ref, vc_ref), (kn_ref, vn_ref))):
        kb_ref[n * blk:(n + 1) * blk] = kr[...]
        vb_ref[n * blk:(n + 1) * blk] = vr[...]
    n_groups = rows // NA_GROUP
    gq = NA_GROUP * GRID_W
    for gi in range(NA_ROWS_PER_STEP // NA_GROUP):
        group = j * (NA_ROWS_PER_STEP // NA_GROUP) + gi
        union_start = jnp.clip(group * NA_GROUP - NA_WIN_ROWS // 2, 0, rows - NA_UNION)
        start = pl.multiple_of((union_start - (j - 1) * NA_ROWS_PER_STEP) * GRID_W, GRID_W)
        variant = jnp.where(group == 0, 0, jnp.where(group == n_groups - 1, 2, 1))
        q = q_ref[gi * gq:(gi + 1) * gq, :]
        s = _qk(q, kb_ref[pl.ds(start, NA_UNION * GRID_W), :]) + bias_ref[variant]
        p = jnp.exp2(s - jnp.max(s, axis=1, keepdims=True))
        v = vb_ref[pl.ds(start, NA_UNION * GRID_W), :]
        acc = jnp.dot(p.astype(BF16), jnp.concatenate([v, jnp.ones_like(v)], axis=1), preferred_element_type=F32)
        o_ref[gi * gq:(gi + 1) * gq, :] = (acc[:, :NA_HEAD_DIM] / acc[:, NA_HEAD_DIM:]).astype(o_ref.dtype)


def _na_attention(zb, col0, bias, batch, seq):
    t = zb.shape[0]
    rows = seq // GRID_W
    assert rows % NA_ROWS_PER_STEP == 0
    nj = rows // NA_ROWS_PER_STEP
    blk = NA_ROWS_PER_STEP * GRID_W
    qc, kc, vc = col0

    def spec(c0, dj):
        return pl.BlockSpec((blk, NA_HEAD_DIM),
                            lambda h, b, j: (b * nj + jnp.clip(j + dj, 0, nj - 1), c0 + h))

    return pl.pallas_call(
        functools.partial(_na_kernel, rows=rows),
        grid=(NA_HEADS, batch, nj),
        in_specs=[spec(qc, 0), spec(kc, -1), spec(kc, 0), spec(kc, 1), spec(vc, -1), spec(vc, 0), spec(vc, 1),
                  pl.BlockSpec((None,) + bias.shape[1:], lambda h, b, j: (h, 0, 0, 0))],
        out_specs=pl.BlockSpec((blk, NA_HEAD_DIM), lambda h, b, j: (b * nj + j, h)),
        out_shape=jax.ShapeDtypeStruct((t, NA_HEADS * NA_HEAD_DIM), BF16),
        scratch_shapes=[pltpu.VMEM((3 * blk, NA_HEAD_DIM), BF16), pltpu.VMEM((3 * blk, NA_HEAD_DIM), BF16)],
        compiler_params=_params("parallel", "parallel", "parallel"),
        name="na_attention",
    )(zb, zb, zb, zb, zb, zb, zb, bias)


def _na_bias_table(rpb):
    col = np.arange(GRID_W)
    col_start = np.clip(col - NA_WIN_COLS // 2, 0, GRID_W - NA_WIN_COLS)
    kc = np.arange(GRID_W)
    valid = (kc[None, :] >= col_start[:, None]) & (kc[None, :] < col_start[:, None] + NA_WIN_COLS)
    by_row = jnp.stack([rpb[:, NA_WIN_ROWS - 1 - t:2 * NA_WIN_ROWS - 1 - t] for t in range(NA_WIN_ROWS)], axis=1)
    pad = GRID_W - NA_WIN_COLS
    padded = jnp.pad(by_row, ((0, 0), (0, 0), (0, 0), (pad, pad)))
    tbl = jnp.stack([padded[..., GRID_W - 1 - c:2 * GRID_W - 1 - c] for c in range(GRID_W)], axis=2)
    tbl = jnp.where(valid[None, None, :, None, :], tbl * LOG2E, MASK_VALUE)
    per_row = tbl.reshape(rpb.shape[0], NA_WIN_ROWS, GRID_W, NA_WIN_ROWS * GRID_W).astype(F32)

    extra = NA_UNION - NA_WIN_ROWS

    def place(t, offset):
        return jnp.pad(per_row[:, t], ((0, 0), (0, 0), (offset * GRID_W, (extra - offset) * GRID_W)),
                       constant_values=MASK_VALUE)

    half = NA_WIN_ROWS // 2
    first = jnp.concatenate([place(i, 0) for i in range(NA_GROUP)], axis=1)
    inner = jnp.concatenate([place(half, i) for i in range(NA_GROUP)], axis=1)
    last = jnp.concatenate([place(half + i, extra) for i in range(NA_GROUP)], axis=1)
    return jnp.stack([first, inner, last], axis=1)


def _out_proj_kernel(a_ref, b_ref, c_ref, wa_ref, wb_ref, wc_ref, h_ref, o_ref, *, tn):
    a, b, c = a_ref[...], b_ref[...], c_ref[...]
    for j in range(o_ref.shape[1] // tn):
        cols = slice(j * tn, (j + 1) * tn)
        acc = jnp.dot(a, wa_ref[:, cols], preferred_element_type=F32)
        acc = acc + jnp.dot(b, wb_ref[:, cols], preferred_element_type=F32)
        acc = acc + jnp.dot(c, wc_ref[:, cols], preferred_element_type=F32)
        o_ref[:, cols] = h_ref[:, cols] + acc


def _out_proj(a, b, c, wa, wb, wc, h, tm=512, tn=512):
    t, d = h.shape
    tm, tn = _tile(t, tm), _tile(d, tn)
    rows = lambda x: pl.BlockSpec((tm, x.shape[1]), lambda i: (i, 0))
    whole = lambda w: pl.BlockSpec(w.shape, lambda i: (0, 0))
    return pl.pallas_call(
        functools.partial(_out_proj_kernel, tn=tn),
        grid=(t // tm,),
        in_specs=[rows(a), rows(b), rows(c), whole(wa), whole(wb), whole(wc), rows(h)],
        out_specs=rows(h),
        out_shape=jax.ShapeDtypeStruct((t, d), F32),
        compiler_params=_params("parallel"),
        name="out_proj",
    )(a, b, c, wa, wb, wc, h)


def _ffn_kernel(h_ref, hp_ref, hn_ref, g_ref, wg_ref, wu_ref, cw_ref, cb_ref, wd_ref, o_ref, xn_ref, *,
                tm, tiles_per_seq):
    i, j = pl.program_id(0), pl.program_id(1)

    @pl.when(j == 0)
    def _():
        x = h_ref[...]
        g = g_ref[...]
        xn_ref[:tm] = _rms(x, g).astype(BF16)
        pos = i % tiles_per_seq
        before = jnp.where(pos == 0, 0.0, _rms(hp_ref[...], g)[BF16_ROWS - 1:BF16_ROWS])
        after = jnp.where(pos == tiles_per_seq - 1, 0.0, _rms(hn_ref[...], g)[0:1])
        rid = lax.broadcasted_iota(jnp.int32, (BF16_ROWS, x.shape[1]), 0)
        halo = jnp.where(rid == 0, before, jnp.where(rid == 1, after, 0.0))
        xn_ref[tm:] = halo.astype(BF16)
        o_ref[...] = x

    ga = jnp.dot(xn_ref[...], wg_ref[...], preferred_element_type=F32)
    up = jnp.dot(xn_ref[:tm], wu_ref[...], preferred_element_type=F32)
    gate = ga[:tm]
    rid = lax.broadcasted_iota(jnp.int32, gate.shape, 0)
    g_prev = jnp.where(rid == 0, ga[tm:tm + 1], pltpu.roll(gate, 1, 0))
    g_next = jnp.where(rid == tm - 1, ga[tm + 1:tm + 2], pltpu.roll(gate, tm - 1, 0))
    cw = cw_ref[...]
    conv = g_prev * cw[0:1] + gate * cw[1:2] + g_next * cw[2:3] + cb_ref[...]
    act = (jax.nn.gelu(conv) * up).astype(BF16)
    o_ref[...] += jnp.dot(act, wd_ref[...], preferred_element_type=F32)


def _ffn(h, g, w_gu, conv_w, conv_b, w_down, seq, tm=512, tf=512):
    t, d = h.shape
    f = w_down.shape[0]
    tm, tf = _tile(seq, tm), _tile(f, tf)
    nf = f // tf
    hb = tm // BF16_ROWS
    nhb = t // BF16_ROWS
    return pl.pallas_call(
        functools.partial(_ffn_kernel, tm=tm, tiles_per_seq=seq // tm),
        grid=(t // tm, nf),
        in_specs=[
            pl.BlockSpec((tm, d), lambda i, j: (i, 0)),
            pl.BlockSpec((BF16_ROWS, d), lambda i, j: (jnp.maximum(i * hb - 1, 0), 0)),
            pl.BlockSpec((BF16_ROWS, d), lambda i, j: (jnp.minimum((i + 1) * hb, nhb - 1), 0)),
            pl.BlockSpec((1, d), lambda i, j: (0, 0)),
            pl.BlockSpec((d, tf), lambda i, j: (0, j)),
            pl.BlockSpec((d, tf), lambda i, j: (0, nf + j)),
            pl.BlockSpec((conv_w.shape[0], tf), lambda i, j: (0, j)),
            pl.BlockSpec((1, tf), lambda i, j: (0, j)),
            pl.BlockSpec((tf, d), lambda i, j: (j, 0)),
        ],
        out_specs=pl.BlockSpec((tm, d), lambda i, j: (i, 0)),
        out_shape=jax.ShapeDtypeStruct((t, d), F32),
        scratch_shapes=[pltpu.VMEM((tm + BF16_ROWS, d), BF16)],
        compiler_params=_params("parallel", "arbitrary"),
        name="ffn",
    )(h, h, h, g, w_gu, w_gu, conv_w, conv_b, w_down)


def _ple_kernel(h_ref, p_ref, g_ref, wg_ref, wp_ref, gf_ref, o_ref, *, tn, final_norm):
    h = h_ref[...]
    xn = _rms(h, g_ref[...]).astype(BF16)
    pb = p_ref[...].astype(BF16)
    for j in range(h.shape[1] // tn):
        cols = slice(j * tn, (j + 1) * tn)
        gate = jax.nn.sigmoid(jnp.dot(xn, wg_ref[:, cols], preferred_element_type=F32))
        emb = jnp.dot(pb, wp_ref[:, cols], preferred_element_type=F32)
        o_ref[:, cols] = h[:, cols] + emb * gate
    if final_norm:
        o_ref[...] = _rms(o_ref[...], gf_ref[...])


def _ple(h, p, g, w_plg, w_ple, g_final, final_norm, tm=512, tn=512):
    t, d = h.shape
    tm, tn = _tile(t, tm), _tile(d, tn)
    whole = lambda a: pl.BlockSpec(a.shape, lambda i: (0, 0))
    return pl.pallas_call(
        functools.partial(_ple_kernel, tn=tn, final_norm=final_norm),
        grid=(t // tm,),
        in_specs=[
            pl.BlockSpec((tm, d), lambda i: (i, 0)),
            pl.BlockSpec((tm, p.shape[1]), lambda i: (i, 0)),
            whole(g), whole(w_plg), whole(w_ple), whole(g_final),
        ],
        out_specs=pl.BlockSpec((tm, d), lambda i: (i, 0)),
        out_shape=jax.ShapeDtypeStruct((t, d), F32),
        compiler_params=_params("parallel"),
        name="ple",
    )(h, p, g, w_plg, w_ple, g_final)


def _swap_halves(x):
    half = x.shape[-1] // 2
    return jnp.concatenate([x[..., half:], x[..., :half]], axis=-1)


def _pad_lanes(x):
    return jnp.concatenate([x, jnp.zeros(x.shape[:-1] + (LANES - x.shape[-1],), x.dtype)], axis=-1)


def _prepare_layer(i, w_in, g_qa, w_qb, g_kva, w_kvb, rpb, w_out, w_gu, w_down, w_plg, w_ple):
    d = w_in.shape[0]
    q_rank, kv_rank = g_qa.shape[0], g_kva.shape[0]
    o = q_rank + kv_rank
    kpe = w_in[:, o:o + MLA_ROPE]
    w_f = jnp.concatenate([w_in[:, :o], _pad_lanes(kpe), _pad_lanes(_swap_halves(kpe))], axis=1)
    w_b = w_in[:, o + MLA_ROPE:]
    na_w = NA_HEADS * NA_HEAD_DIM
    diff_w = DIFF_HEADS * 2 * DIFF_QK_DIM
    scale_b = np.ones((1, w_b.shape[1]), np.float32)
    scale_b[:, :na_w] = NA_HEAD_DIM ** -0.5 * LOG2E
    scale_b[:, 3 * na_w:3 * na_w + diff_w] = DIFF_QK_DIM ** -0.5 * LOG2E
    wq = w_qb.reshape(q_rank, MLA_HEADS, MLA_NOPE + MLA_ROPE)
    pe = wq[..., MLA_NOPE:]
    wq = jnp.concatenate([wq[..., :MLA_NOPE].reshape(q_rank, -1), _pad_lanes(pe).reshape(q_rank, -1),
                          _pad_lanes(_swap_halves(pe)).reshape(q_rank, -1)], axis=1)
    mla_w = MLA_HEADS * MLA_V
    return dict(
        w_in=jnp.concatenate([w_b, w_f], axis=1).astype(BF16), n_bf16=w_b.shape[1], scale_b=jnp.asarray(scale_b),
        wq=wq.astype(BF16), wkv=w_kvb.astype(BF16),
        na_bias=_na_bias_table(rpb),
        wo_mla=w_out[:mla_w].astype(BF16), wo_na=w_out[mla_w:mla_w + na_w].astype(BF16),
        wo_diff=w_out[mla_w + na_w:].astype(BF16),
        w_gu=w_gu.astype(BF16), w_down=w_down.astype(BF16),
        w_plg=w_plg.astype(BF16), w_ple=w_ple.astype(BF16),
        lam_init=0.8 - 0.6 * math.exp(-0.3 * i),
    )


def _rope_tables(seq):
    half = MLA_ROPE // 2
    freqs = jnp.power(ROPE_THETA, -jnp.arange(half, dtype=F32) / half)
    ang = jnp.arange(seq, dtype=jnp.int32).astype(F32)[:, None] * freqs[None, :]
    cos, sin = jnp.cos(ang), jnp.sin(ang)
    return _pad_lanes(jnp.concatenate([cos, cos], axis=1)), _pad_lanes(jnp.concatenate([-sin, sin], axis=1))


def _trunk(x, p, layers, vecs, g_final):
    batch, seq, d = x.shape
    t = batch * seq
    h = x.reshape(t, d)
    rope_c, rope_s = _rope_tables(seq)
    slopes = jnp.exp2(-8.0 * jnp.arange(1, DIFF_HEADS + 1, dtype=F32) / DIFF_HEADS) * LOG2E
    nb = NA_HEADS * NA_HEAD_DIM // LANES
    for i, (lw, lv) in enumerate(zip(layers, vecs)):
        zb, zf = _in_proj(h, lv["g_attn"], lw["w_in"], lw["scale_b"], lw["n_bf16"])
        q, k, v = _mla_proj(zf, lv["g_qa"], lv["g_kva"], lw["wq"], lw["wkv"], rope_c, rope_s, seq)
        o_mla = _mla_flash(q, k, v, batch, seq)
        o_na = _na_attention(zb, (0, nb, 2 * nb), lw["na_bias"], batch, seq)
        o_diff = _diff_flash(zb, (3 * nb, 4 * nb, 5 * nb), slopes, lv["lam_q1"], lv["lam_k1"], lv["lam_q2"],
                             lv["lam_k2"], lv["g_subln"], lw["lam_init"], batch, seq)
        h = _out_proj(o_mla, o_na, o_diff, lw["wo_mla"], lw["wo_na"], lw["wo_diff"], h)
        h = _ffn(h, lv["g_ffn"], lw["w_gu"], lv["conv_w"], lv["conv_b"], lw["w_down"], seq)
        h = _ple(h, p[i].reshape(t, -1), lv["g_ple"], lw["w_plg"], lw["w_ple"], g_final.reshape(1, d),
                 final_norm=i == len(layers) - 1)
    return h.reshape(batch, seq, d)


def kernel(x_prompt, x_sample, p_prompt, p_sample, g_attn, w_in, g_qa, w_qb, g_kva, w_kvb, rpb, lam_q1, lam_k1,
           lam_q2, lam_k2, g_subln, w_out, g_ffn, w_gu, conv_w, conv_b, w_down, g_ple, w_plg, w_ple, g_final):
    depth = w_in.shape[0]
    layers = [_prepare_layer(i, w_in[i], g_qa[i], w_qb[i], g_kva[i], w_kvb[i], rpb[i], w_out[i], w_gu[i],
                             w_down[i], w_plg[i], w_ple[i]) for i in range(depth)]
    row = lambda a: a.reshape(1, -1)
    vecs = [dict(g_attn=row(g_attn[i]), g_qa=row(g_qa[i]), g_kva=row(g_kva[i]), lam_q1=row(lam_q1[i]),
                 lam_k1=row(lam_k1[i]), lam_q2=row(lam_q2[i]), lam_k2=row(lam_k2[i]), g_subln=row(g_subln[i]),
                 g_ffn=row(g_ffn[i]), conv_w=conv_w[i], conv_b=row(conv_b[i]), g_ple=row(g_ple[i]))
            for i in range(depth)]
    return (_trunk(x_prompt, p_prompt, layers, vecs, g_final), _trunk(x_sample, p_sample, layers, vecs, g_final))
```

```python
import functools
import math

import jax
import jax.numpy as jnp
import numpy as np
from jax import lax
from jax.experimental import pallas as pl
from jax.experimental.pallas import tpu as pltpu

F32 = jnp.float32
BF16 = jnp.bfloat16

GRID_W = 64
MLA_HEADS = 8
MLA_NOPE = 128
MLA_ROPE = 64
MLA_V = 128
MLA_QK_PAD = 256
ROPE_THETA = 10000.0
NA_HEADS = 4
NA_HEAD_DIM = 128
NA_WIN_ROWS = 8
NA_WIN_COLS = 16
NA_ROWS_PER_STEP = 32
NA_GROUP = 4
NA_UNION = NA_WIN_ROWS + NA_GROUP - 1
DIFF_HEADS = 4
DIFF_QK_DIM = 64
DIFF_V_DIM = 128
DIFF_STEEP_HEADS = 2
RMS_EPS = 1e-6
LOG2E = 1.4426950408889634
MASK_VALUE = -1e30
EXP2_UNDERFLOW = 160.0
NORM_BOUND_SLACK = 1.001
LANES = 128
BF16_ROWS = 16
VMEM_LIMIT = 56 * 1024 * 1024


def _tile(n, pref):
    t = min(n, pref)
    assert n % t == 0, (n, pref)
    return t


def _rms(x, g):
    ms = jnp.mean(x * x, axis=-1, keepdims=True)
    return x * lax.rsqrt(ms + RMS_EPS) * g


def _lane_tile(x, n):
    return jnp.concatenate([x] * n, axis=1)


def _params(*sem):
    return pltpu.CompilerParams(dimension_semantics=sem, vmem_limit_bytes=VMEM_LIMIT)


def _in_proj_kernel(x_ref, g_ref, w_ref, cs_ref, zb_ref, zf_ref, xn_ref, *, nb):
    j = pl.program_id(1)

    @pl.when(j == 0)
    def _():
        xn_ref[...] = _rms(x_ref[...], g_ref[...]).astype(BF16)

    acc = jnp.dot(xn_ref[...], w_ref[...], preferred_element_type=F32)

    @pl.when(j < nb)
    def _():
        zb_ref[...] = (acc * cs_ref[...]).astype(BF16)

    @pl.when(j >= nb)
    def _():
        zf_ref[...] = acc


def _in_proj(x, g, w, col_scale, n_bf16, tm=1024, tn=1024):
    t, d = x.shape
    n = w.shape[1]
    tm, tn = _tile(t, tm), _tile(math.gcd(n_bf16, n - n_bf16), tn)
    nb, nf = n_bf16 // tn, (n - n_bf16) // tn
    return pl.pallas_call(
        functools.partial(_in_proj_kernel, nb=nb),
        grid=(t // tm, nb + nf),
        in_specs=[
            pl.BlockSpec((tm, d), lambda i, j: (i, 0)),
            pl.BlockSpec((1, d), lambda i, j: (0, 0)),
            pl.BlockSpec((d, tn), lambda i, j: (0, j)),
            pl.BlockSpec((1, tn), lambda i, j: (0, jnp.minimum(j, nb - 1))),
        ],
        out_specs=[pl.BlockSpec((tm, tn), lambda i, j: (i, jnp.minimum(j, nb - 1))),
                   pl.BlockSpec((tm, tn), lambda i, j: (i, jnp.maximum(j - nb, 0)))],
        out_shape=[jax.ShapeDtypeStruct((t, n_bf16), BF16), jax.ShapeDtypeStruct((t, n - n_bf16), F32)],
        scratch_shapes=[pltpu.VMEM((tm, d), BF16)],
        compiler_params=_params("parallel", "arbitrary"),
        name="in_proj",
    )(x, g, w, col_scale)


def _mla_proj_kernel(zf_ref, gq_ref, gkv_ref, wq_ref, wkv_ref, c_ref, s_ref, q_ref, k_ref, v_ref, *,
                     q_rank, kv_rank, q_scale):
    zf = zf_ref[...]
    cos, sin = c_ref[...], s_ref[...]
    qn = _rms(zf[:, :q_rank], gq_ref[...]).astype(BF16)
    qa = jnp.dot(qn, wq_ref[...], preferred_element_type=F32)
    kvn = _rms(zf[:, q_rank:q_rank + kv_rank], gkv_ref[...]).astype(BF16)
    kva = jnp.dot(kvn, wkv_ref[...], preferred_element_type=F32)
    o = q_rank + kv_rank
    kpe = (zf[:, o:o + LANES] * cos + zf[:, o + LANES:o + 2 * LANES] * sin).astype(BF16)
    hp = MLA_HEADS * LANES
    for h in range(MLA_HEADS):
        lo = h * LANES
        rope = qa[:, hp + lo:hp + lo + LANES] * cos + qa[:, 2 * hp + lo:2 * hp + lo + LANES] * sin
        q_ref[:, 2 * lo:2 * lo + LANES] = (qa[:, lo:lo + LANES] * q_scale).astype(BF16)
        q_ref[:, 2 * lo + LANES:2 * lo + 2 * LANES] = (rope * q_scale).astype(BF16)
        k_ref[:, 2 * lo:2 * lo + LANES] = kva[:, 2 * lo:2 * lo + LANES].astype(BF16)
        k_ref[:, 2 * lo + LANES:2 * lo + 2 * LANES] = kpe
        v_ref[:, lo:lo + LANES] = kva[:, 2 * lo + LANES:2 * lo + 2 * LANES].astype(BF16)


def _mla_proj(zf, gq, gkv, wq, wkv, rope_c, rope_s, seq, tm=512):
    t, zw = zf.shape
    tm = _tile(seq, tm)
    nl = seq // tm
    q_rank, kv_rank = gq.shape[1], gkv.shape[1]
    kern = functools.partial(_mla_proj_kernel, q_rank=q_rank, kv_rank=kv_rank,
                             q_scale=(MLA_NOPE + MLA_ROPE) ** -0.5 * LOG2E)
    qk_w = MLA_HEADS * MLA_QK_PAD
    return pl.pallas_call(
        kern,
        grid=(t // tm,),
        in_specs=[
            pl.BlockSpec((tm, zw), lambda i: (i, 0)),
            pl.BlockSpec((1, q_rank), lambda i: (0, 0)),
            pl.BlockSpec((1, kv_rank), lambda i: (0, 0)),
            pl.BlockSpec(wq.shape, lambda i: (0, 0)),
            pl.BlockSpec(wkv.shape, lambda i: (0, 0)),
            pl.BlockSpec((tm, LANES), lambda i: (i % nl, 0)),
            pl.BlockSpec((tm, LANES), lambda i: (i % nl, 0)),
        ],
        out_specs=[
            pl.BlockSpec((tm, qk_w), lambda i: (i, 0)),
            pl.BlockSpec((tm, qk_w), lambda i: (i, 0)),
            pl.BlockSpec((tm, MLA_HEADS * MLA_V), lambda i: (i, 0)),
        ],
        out_shape=[
            jax.ShapeDtypeStruct((t, qk_w), BF16),
            jax.ShapeDtypeStruct((t, qk_w), BF16),
            jax.ShapeDtypeStruct((t, MLA_HEADS * MLA_V), BF16),
        ],
        compiler_params=_params("parallel"),
        name="mla_proj",
    )(zf, gq, gkv, wq, wkv, rope_c, rope_s)


def _online_softmax_step(s, v, m_ref, acc_ref, shift):
    m_prev = m_ref[...]
    m_cur = jnp.max(s, axis=1, keepdims=True)
    if shift is not None:
        m_cur = m_cur + shift
    m_next = jnp.maximum(m_prev, m_cur)
    alpha = jnp.exp2(m_prev - m_next)
    m_sub = m_next if shift is None else m_next - shift
    p = jnp.exp2(s - _lane_tile(m_sub, s.shape[1] // LANES))
    v1 = jnp.concatenate([v, jnp.ones_like(v)], axis=1)
    acc_ref[...] = _lane_tile(alpha, 2) * acc_ref[...] + jnp.dot(p.astype(BF16), v1, preferred_element_type=F32)
    m_ref[...] = m_next


def _init_softmax_state(m_ref, acc_ref):
    m_ref[...] = jnp.full(m_ref.shape, MASK_VALUE, F32)
    acc_ref[...] = jnp.zeros(acc_ref.shape, F32)


def _softmax_result(acc_ref):
    acc = acc_ref[...]
    return acc[:, :LANES] / acc[:, LANES:]


def _qk(q, k):
    return lax.dot_general(q, k, (((1,), (1,)), ((), ())), preferred_element_type=F32)


def _mla_flash_kernel(q_ref, k_ref, v_ref, o_ref, m_ref, acc_ref, *, tkc, unroll):
    _init_softmax_state(m_ref, acc_ref)
    q = q_ref[...]

    def block(bi, carry):
        for c in range(unroll):
            off = pl.multiple_of((bi * unroll + c) * tkc, tkc)
            s = _qk(q, k_ref[pl.ds(off, tkc), :])
            _online_softmax_step(s, v_ref[pl.ds(off, tkc), :], m_ref, acc_ref, None)
        return carry

    lax.fori_loop(0, k_ref.shape[0] // (tkc * unroll), block, 0)
    o_ref[...] = _softmax_result(acc_ref).astype(o_ref.dtype)


def _mla_flash(q, k, v, batch, seq, tq=1024, tkc=512, unroll=16):
    t = q.shape[0]
    tq, tkc = _tile(seq, tq), _tile(seq, tkc)
    unroll = _tile(seq // tkc, unroll)
    nq = seq // tq
    return pl.pallas_call(
        functools.partial(_mla_flash_kernel, tkc=tkc, unroll=unroll),
        grid=(MLA_HEADS, batch, nq),
        in_specs=[
            pl.BlockSpec((tq, MLA_QK_PAD), lambda g, b, qi: (b * nq + qi, g)),
            pl.BlockSpec((seq, MLA_QK_PAD), lambda g, b, qi: (b, g)),
            pl.BlockSpec((seq, MLA_V), lambda g, b, qi: (b, g)),
        ],
        out_specs=pl.BlockSpec((tq, MLA_V), lambda g, b, qi: (b * nq + qi, g)),
        out_shape=jax.ShapeDtypeStruct((t, MLA_HEADS * MLA_V), BF16),
        scratch_shapes=[pltpu.VMEM((tq, LANES), F32), pltpu.VMEM((tq, MLA_V + LANES), F32)],
        compiler_params=_params("parallel", "parallel", "parallel"),
        name="mla_flash",
    )(q, k, v)


def _key_norm_kernel(k_ref, o_ref):
    @pl.when(pl.program_id(1) == 0)
    def _():
        o_ref[...] = jnp.zeros(o_ref.shape, F32)

    k = k_ref[...].astype(F32)
    sq = k * k
    for g in range(DIFF_HEADS):
        top = jnp.max(jnp.sum(sq[:, g * LANES:(g + 1) * LANES], axis=1, keepdims=True), axis=0, keepdims=True)
        o_ref[g * 8:(g + 1) * 8, :] = jnp.maximum(o_ref[g * 8:(g + 1) * 8, :], top)


def _max_key_norm_sq(zb, kc, batch, seq, tk=2048):
    tk = _tile(seq, tk)
    nk = seq // tk
    assert kc % DIFF_HEADS == 0
    return pl.pallas_call(
        _key_norm_kernel,
        grid=(batch, nk),
        in_specs=[pl.BlockSpec((tk, DIFF_HEADS * LANES), lambda b, i: (b * nk + i, kc // DIFF_HEADS))],
        out_specs=pl.BlockSpec((DIFF_HEADS * 8, LANES), lambda b, i: (b, 0)),
        out_shape=jax.ShapeDtypeStruct((batch * DIFF_HEADS * 8, LANES), F32),
        compiler_params=_params("parallel", "arbitrary"),
        name="key_norm",
    )(zb)


def _diff_flash_kernel(slope_ref, lam0_ref, kn_ref, lq1_ref, lk1_ref, lq2_ref, lk2_ref, gsub_ref, q_ref, k_ref, v_ref,
                       o_ref, qs_ref, m_ref, acc_ref, bias_ref, cap_ref, qx_ref, kx_ref, *, tq, near_tkb, far_tkb):
    g, qi = pl.program_id(0), pl.program_id(2)
    slope = slope_ref[g]
    lam_init = jnp.full((1, LANES), lam0_ref[0], F32)
    q0 = qi * tq

    @pl.when((qi == 0) & (pl.program_id(1) == 0))
    def _():
        shape = (2 * tq, tq)
        row = lax.broadcasted_iota(jnp.int32, shape, 0)
        row = jnp.where(row >= tq, row - tq, row)
        d = (row - lax.broadcasted_iota(jnp.int32, shape, 1)).astype(F32)
        bias_ref[0] = -slope * d
        bias_ref[1] = -slope * jnp.abs(d)
        bias_ref[2] = slope * d
        whole = jnp.full((1, LANES), slope, F32)
        p1 = whole.astype(BF16).astype(F32)
        p2 = (whole - p1).astype(BF16).astype(F32)
        p3 = (whole - p1 - p2).astype(BF16).astype(F32)
        lane = lax.broadcasted_iota(jnp.int32, (tq, LANES), 1)
        pos = lax.broadcasted_iota(jnp.int32, (tq, LANES), 0)
        even = (lane & 1) == 0
        digit = jnp.where(even, pos >> 4, pos & 15).astype(F32)
        pair = jnp.where(lane >= 6, lane - 6, lane) >> 1
        piece = jnp.where(pair == 0, p1, jnp.where(pair == 1, p2, p3))
        weight = jnp.where(even, 16.0 * piece, piece)
        kx_ref[...] = jnp.where(lane < 6, weight, jnp.where(lane < 12, digit, 0.0)).astype(BF16)
        after = jnp.where(lane < 6, digit, jnp.where(lane < 12, -weight, 0.0)).astype(BF16)
        for half in range(2):
            qx_ref[0, half * tq:(half + 1) * tq, LANES:] = -after
            qx_ref[1, half * tq:(half + 1) * tq, LANES:] = after

    _init_softmax_state(m_ref, acc_ref)
    q = q_ref[...]
    lane = lax.broadcasted_iota(jnp.int32, q.shape, 1)
    qs_ref[:tq] = jnp.where(lane < DIFF_QK_DIM, q, jnp.zeros_like(q))
    qs_ref[tq:] = jnp.where(lane >= DIFF_QK_DIM, q, jnp.zeros_like(q))
    for side in range(2):
        qx_ref[side, :, :LANES] = qs_ref[...]
    qf = q.astype(F32)
    q_norm = jnp.sqrt(jnp.sum(qf * qf, axis=1, keepdims=True))
    cap = q_norm * jnp.sqrt(kn_ref[0:1, :]) * NORM_BOUND_SLACK + NORM_BOUND_SLACK
    cap_ref[:tq] = cap
    cap_ref[tq:] = cap

    def block(ki, far, tkb):
        nk = k_ref.shape[0] // tkb
        nch = tkb // tq
        unrolled_from = min(nch, (5 * nch + 7) // 8)
        kb0 = ((q0 // tkb + ki) % nk) * tkb
        best = jnp.max(cap_ref[...] - m_ref[...], axis=0, keepdims=True)
        cidx = lax.broadcasted_iota(jnp.int32, (max(8, nch), LANES), 0)
        kc0 = kb0 + cidx * tq
        nearest = jnp.maximum(jnp.maximum(q0 - (kc0 + tq) + 1, kc0 - (q0 + tq) + 1), 0)
        live = (cidx < nch) & (best >= slope * nearest.astype(F32) - EXP2_UNDERFLOW)
        n_live = jnp.sum(jnp.where(live, 1, 0), axis=0, keepdims=True)[0, 0]
        first_live = jnp.min(jnp.where(live, cidx, nch), axis=0, keepdims=True)[0, 0]

        def chunk(c):
            k0 = pl.multiple_of(kb0 + c * tq, tq)
            gap = jnp.full((1, LANES), jnp.abs(q0 - k0), jnp.int32).astype(F32)
            if far:
                keys = jnp.concatenate([k_ref[pl.ds(k0, tq), :], kx_ref[...]], axis=1)
                s = _qk(qx_ref[jnp.where(k0 < q0, 0, 1)], keys)
            else:
                which = jnp.where(k0 < q0, 0, jnp.where(k0 == q0, 1, 2))
                s = _qk(qs_ref[...], k_ref[pl.ds(k0, tq), :]) + bias_ref[which]
            _online_softmax_step(s, v_ref[pl.ds(k0, tq), :], m_ref, acc_ref, -slope * gap)

        @pl.when(n_live >= unrolled_from)
        def _():
            for c in range(nch):
                chunk(c)

        @pl.when((n_live > 0) & (n_live < unrolled_from))
        def _():
            def body(c, inner):
                chunk(c)
                return inner

            lax.fori_loop(first_live, first_live + n_live, body, 0)

    def all_blocks(tkb):
        block(0, False, tkb)

        def far_block(ki, carry):
            block(ki, True, tkb)
            return carry

        lax.fori_loop(1, k_ref.shape[0] // tkb, far_block, 0)

    if near_tkb == far_tkb:
        all_blocks(far_tkb)
    else:
        pl.when(g < DIFF_STEEP_HEADS)(lambda: all_blocks(near_tkb))
        pl.when(g >= DIFF_STEEP_HEADS)(lambda: all_blocks(far_tkb))

    o = _softmax_result(acc_ref)
    lam = (jnp.exp(jnp.sum(lq1_ref[...] * lk1_ref[...], axis=-1, keepdims=True))
           - jnp.exp(jnp.sum(lq2_ref[...] * lk2_ref[...], axis=-1, keepdims=True)) + lam_init)
    d = o[:tq] - lam * o[tq:]
    o_ref[...] = (_rms(d, gsub_ref[...]) * (1.0 - lam_init)).astype(o_ref.dtype)


def _diff_flash(zb, col0, slopes, lq1, lk1, lq2, lk2, gsub, lam_init, batch, seq, tq=512, near_tkb=2048,
                far_tkb=8192):
    t = zb.shape[0]
    tq = _tile(seq, tq)
    near_tkb, far_tkb = _tile(seq, near_tkb), _tile(seq, far_tkb)
    assert near_tkb % tq == 0 and far_tkb % tq == 0
    nq = seq // tq
    qc, kc, vc = col0
    key_norm = _max_key_norm_sq(zb, kc, batch, seq)
    vec = lambda a: pl.BlockSpec(a.shape, lambda g, b, qi: (0, 0))
    return pl.pallas_call(
        functools.partial(_diff_flash_kernel, tq=tq, near_tkb=near_tkb, far_tkb=far_tkb),
        grid=(DIFF_HEADS, batch, nq),
        in_specs=[
            pl.BlockSpec(memory_space=pltpu.SMEM),
            pl.BlockSpec(memory_space=pltpu.SMEM),
            pl.BlockSpec((8, LANES), lambda g, b, qi: (b * DIFF_HEADS + g, 0)),
            vec(lq1), vec(lk1), vec(lq2), vec(lk2), vec(gsub),
            pl.BlockSpec((tq, LANES), lambda g, b, qi: (b * nq + qi, qc + g)),
            pl.BlockSpec((seq, LANES), lambda g, b, qi: (b, kc + g)),
            pl.BlockSpec((seq, DIFF_V_DIM), lambda g, b, qi: (b, vc + g)),
        ],
        out_specs=pl.BlockSpec((tq, DIFF_V_DIM), lambda g, b, qi: (b * nq + qi, g)),
        out_shape=jax.ShapeDtypeStruct((t, DIFF_HEADS * DIFF_V_DIM), BF16),
        scratch_shapes=[pltpu.VMEM((2 * tq, LANES), BF16), pltpu.VMEM((2 * tq, LANES), F32),
                        pltpu.VMEM((2 * tq, DIFF_V_DIM + LANES), F32), pltpu.VMEM((3, 2 * tq, tq), F32),
                        pltpu.VMEM((2 * tq, LANES), F32), pltpu.VMEM((2, 2 * tq, 2 * LANES), BF16),
                        pltpu.VMEM((tq, LANES), BF16)],
        compiler_params=_params("parallel", "arbitrary", "arbitrary"),
        name="diff_flash",
    )(slopes, lam_init, key_norm, lq1, lk1, lq2, lk2, gsub, zb, zb, zb)


def _na_kernel(q_ref, kp_ref, kc_ref, kn_ref, vp_ref, vc_ref, vn_ref, bias_ref, o_ref, kb_ref, vb_ref, *, rows):
    j = pl.program_id(2)
    blk = NA_ROWS_PER_STEP * GRID_W
    for n, (kr, vr) in enumerate(((kp_ref, vp_ref), (kc_ref, vc_ref), (kn_ref, vn_ref))):
        kb_ref[n * blk:(n + 1) * blk] = kr[...]
        vb_ref[n * blk:(n + 1) * blk] = vr[...]
    n_groups = rows // NA_GROUP
    gq = NA_GROUP * GRID_W
    for gi in range(NA_ROWS_PER_STEP // NA_GROUP):
        group = j * (NA_ROWS_PER_STEP // NA_GROUP) + gi
        union_start = jnp.clip(group * NA_GROUP - NA_WIN_ROWS // 2, 0, rows - NA_UNION)
        start = pl.multiple_of((union_start - (j - 1) * NA_ROWS_PER_STEP) * GRID_W, GRID_W)
        variant = jnp.where(group == 0, 0, jnp.where(group == n_groups - 1, 2, 1))
        q = q_ref[gi * gq:(gi + 1) * gq, :]
        s = _qk(q, kb_ref[pl.ds(start, NA_UNION * GRID_W), :]) + bias_ref[variant]
        p = jnp.exp2(s - jnp.max(s, axis=1, keepdims=True))
        v = vb_ref[pl.ds(start, NA_UNION * GRID_W), :]
        acc = jnp.dot(p.astype(BF16), jnp.concatenate([v, jnp.ones_like(v)], axis=1), preferred_element_type=F32)
        o_ref[gi * gq:(gi + 1) * gq, :] = (acc[:, :NA_HEAD_DIM] / acc[:, NA_HEAD_DIM:]).astype(o_ref.dtype)


def _na_attention(zb, col0, bias, batch, seq):
    t = zb.shape[0]
    rows = seq // GRID_W
    assert rows % NA_ROWS_PER_STEP == 0
    nj = rows // NA_ROWS_PER_STEP
    blk = NA_ROWS_PER_STEP * GRID_W
    qc, kc, vc = col0

    def spec(c0, dj):
        return pl.BlockSpec((blk, NA_HEAD_DIM),
                            lambda h, b, j: (b * nj + jnp.clip(j + dj, 0, nj - 1), c0 + h))

    return pl.pallas_call(
        functools.partial(_na_kernel, rows=rows),
        grid=(NA_HEADS, batch, nj),
        in_specs=[spec(qc, 0), spec(kc, -1), spec(kc, 0), spec(kc, 1), spec(vc, -1), spec(vc, 0), spec(vc, 1),
                  pl.BlockSpec((None,) + bias.shape[1:], lambda h, b, j: (h, 0, 0, 0))],
        out_specs=pl.BlockSpec((blk, NA_HEAD_DIM), lambda h, b, j: (b * nj + j, h)),
        out_shape=jax.ShapeDtypeStruct((t, NA_HEADS * NA_HEAD_DIM), BF16),
        scratch_shapes=[pltpu.VMEM((3 * blk, NA_HEAD_DIM), BF16), pltpu.VMEM((3 * blk, NA_HEAD_DIM), BF16)],
        compiler_params=_params("parallel", "parallel", "parallel"),
        name="na_attention",
    )(zb, zb, zb, zb, zb, zb, zb, bias)


def _na_bias_table(rpb):
    col = np.arange(GRID_W)
    col_start = np.clip(col - NA_WIN_COLS // 2, 0, GRID_W - NA_WIN_COLS)
    kc = np.arange(GRID_W)
    valid = (kc[None, :] >= col_start[:, None]) & (kc[None, :] < col_start[:, None] + NA_WIN_COLS)
    by_row = jnp.stack([rpb[:, NA_WIN_ROWS - 1 - t:2 * NA_WIN_ROWS - 1 - t] for t in range(NA_WIN_ROWS)], axis=1)
    pad = GRID_W - NA_WIN_COLS
    padded = jnp.pad(by_row, ((0, 0), (0, 0), (0, 0), (pad, pad)))
    tbl = jnp.stack([padded[..., GRID_W - 1 - c:2 * GRID_W - 1 - c] for c in range(GRID_W)], axis=2)
    tbl = jnp.where(valid[None, None, :, None, :], tbl * LOG2E, MASK_VALUE)
    per_row = tbl.reshape(rpb.shape[0], NA_WIN_ROWS, GRID_W, NA_WIN_ROWS * GRID_W).astype(F32)

    extra = NA_UNION - NA_WIN_ROWS

    def place(t, offset):
        return jnp.pad(per_row[:, t], ((0, 0), (0, 0), (offset * GRID_W, (extra - offset) * GRID_W)),
                       constant_values=MASK_VALUE)

    half = NA_WIN_ROWS // 2
    first = jnp.concatenate([place(i, 0) for i in range(NA_GROUP)], axis=1)
    inner = jnp.concatenate([place(half, i) for i in range(NA_GROUP)], axis=1)
    last = jnp.concatenate([place(half + i, extra) for i in range(NA_GROUP)], axis=1)
    return jnp.stack([first, inner, last], axis=1)


def _out_proj_kernel(a_ref, b_ref, c_ref, wa_ref, wb_ref, wc_ref, h_ref, o_ref, *, tn):
    a, b, c = a_ref[...], b_ref[...], c_ref[...]
    for j in range(o_ref.shape[1] // tn):
        cols = slice(j * tn, (j + 1) * tn)
        acc = jnp.dot(a, wa_ref[:, cols], preferred_element_type=F32)
        acc = acc + jnp.dot(b, wb_ref[:, cols], preferred_element_type=F32)
        acc = acc + jnp.dot(c, wc_ref[:, cols], preferred_element_type=F32)
        o_ref[:, cols] = h_ref[:, cols] + acc


def _out_proj(a, b, c, wa, wb, wc, h, tm=512, tn=512):
    t, d = h.shape
    tm, tn = _tile(t, tm), _tile(d, tn)
    rows = lambda x: pl.BlockSpec((tm, x.shape[1]), lambda i: (i, 0))
    whole = lambda w: pl.BlockSpec(w.shape, lambda i: (0, 0))
    return pl.pallas_call(
        functools.partial(_out_proj_kernel, tn=tn),
        grid=(t // tm,),
        in_specs=[rows(a), rows(b), rows(c), whole(wa), whole(wb), whole(wc), rows(h)],
        out_specs=rows(h),
        out_shape=jax.ShapeDtypeStruct((t, d), F32),
        compiler_params=_params("parallel"),
        name="out_proj",
    )(a, b, c, wa, wb, wc, h)


def _ffn_kernel(h_ref, hp_ref, hn_ref, g_ref, wg_ref, wu_ref, cw_ref, cb_ref, wd_ref, o_ref, xn_ref, *,
                tm, tiles_per_seq):
    i, j = pl.program_id(0), pl.program_id(1)

    @pl.when(j == 0)
    def _():
        x = h_ref[...]
        g = g_ref[...]
        xn_ref[:tm] = _rms(x, g).astype(BF16)
        pos = i % tiles_per_seq
        before = jnp.where(pos == 0, 0.0, _rms(hp_ref[...], g)[BF16_ROWS - 1:BF16_ROWS])
        after = jnp.where(pos == tiles_per_seq - 1, 0.0, _rms(hn_ref[...], g)[0:1])
        rid = lax.broadcasted_iota(jnp.int32, (BF16_ROWS, x.shape[1]), 0)
        halo = jnp.where(rid == 0, before, jnp.where(rid == 1, after, 0.0))
        xn_ref[tm:] = halo.astype(BF16)
        o_ref[...] = x

    ga = jnp.dot(xn_ref[...], wg_ref[...], preferred_element_type=F32)
    up = jnp.dot(xn_ref[:tm], wu_ref[...], preferred_element_type=F32)
    gate = ga[:tm]
    rid = lax.broadcasted_iota(jnp.int32, gate.shape, 0)
    g_prev = jnp.where(rid == 0, ga[tm:tm + 1], pltpu.roll(gate, 1, 0))
    g_next = jnp.where(rid == tm - 1, ga[tm + 1:tm + 2], pltpu.roll(gate, tm - 1, 0))
    cw = cw_ref[...]
    conv = g_prev * cw[0:1] + gate * cw[1:2] + g_next * cw[2:3] + cb_ref[...]
    act = (jax.nn.gelu(conv) * up).astype(BF16)
    o_ref[...] += jnp.dot(act, wd_ref[...], preferred_element_type=F32)


def _ffn(h, g, w_gu, conv_w, conv_b, w_down, seq, tm=512, tf=512):
    t, d = h.shape
    f = w_down.shape[0]
    tm, tf = _tile(seq, tm), _tile(f, tf)
    nf = f // tf
    hb = tm // BF16_ROWS
    nhb = t // BF16_ROWS
    return pl.pallas_call(
        functools.partial(_ffn_kernel, tm=tm, tiles_per_seq=seq // tm),
        grid=(t // tm, nf),
        in_specs=[
            pl.BlockSpec((tm, d), lambda i, j: (i, 0)),
            pl.BlockSpec((BF16_ROWS, d), lambda i, j: (jnp.maximum(i * hb - 1, 0), 0)),
            pl.BlockSpec((BF16_ROWS, d), lambda i, j: (jnp.minimum((i + 1) * hb, nhb - 1), 0)),
            pl.BlockSpec((1, d), lambda i, j: (0, 0)),
            pl.BlockSpec((d, tf), lambda i, j: (0, j)),
            pl.BlockSpec((d, tf), lambda i, j: (0, nf + j)),
            pl.BlockSpec((conv_w.shape[0], tf), lambda i, j: (0, j)),
            pl.BlockSpec((1, tf), lambda i, j: (0, j)),
            pl.BlockSpec((tf, d), lambda i, j: (j, 0)),
        ],
        out_specs=pl.BlockSpec((tm, d), lambda i, j: (i, 0)),
        out_shape=jax.ShapeDtypeStruct((t, d), F32),
        scratch_shapes=[pltpu.VMEM((tm + BF16_ROWS, d), BF16)],
        compiler_params=_params("parallel", "arbitrary"),
        name="ffn",
    )(h, h, h, g, w_gu, w_gu, conv_w, conv_b, w_down)


def _ple_kernel(h_ref, p_ref, g_ref, wg_ref, wp_ref, gf_ref, o_ref, *, tn, final_norm):
    h = h_ref[...]
    xn = _rms(h, g_ref[...]).astype(BF16)
    pb = p_ref[...].astype(BF16)
    for j in range(h.shape[1] // tn):
        cols = slice(j * tn, (j + 1) * tn)
        gate = jax.nn.sigmoid(jnp.dot(xn, wg_ref[:, cols], preferred_element_type=F32))
        emb = jnp.dot(pb, wp_ref[:, cols], preferred_element_type=F32)
        o_ref[:, cols] = h[:, cols] + emb * gate
    if final_norm:
        o_ref[...] = _rms(o_ref[...], gf_ref[...])


def _ple(h, p, g, w_plg, w_ple, g_final, final_norm, tm=512, tn=512):
    t, d = h.shape
    tm, tn = _tile(t, tm), _tile(d, tn)
    whole = lambda a: pl.BlockSpec(a.shape, lambda i: (0, 0))
    return pl.pallas_call(
        functools.partial(_ple_kernel, tn=tn, final_norm=final_norm),
        grid=(t // tm,),
        in_specs=[
            pl.BlockSpec((tm, d), lambda i: (i, 0)),
            pl.BlockSpec((tm, p.shape[1]), lambda i: (i, 0)),
            whole(g), whole(w_plg), whole(w_ple), whole(g_final),
        ],
        out_specs=pl.BlockSpec((tm, d), lambda i: (i, 0)),
        out_shape=jax.ShapeDtypeStruct((t, d), F32),
        compiler_params=_params("parallel"),
        name="ple",
    )(h, p, g, w_plg, w_ple, g_final)


def _swap_halves(x):
    half = x.shape[-1] // 2
    return jnp.concatenate([x[..., half:], x[..., :half]], axis=-1)


def _pad_lanes(x):
    return jnp.concatenate([x, jnp.zeros(x.shape[:-1] + (LANES - x.shape[-1],), x.dtype)], axis=-1)


def _prepare_layer(i, w_in, g_qa, w_qb, g_kva, w_kvb, rpb, w_out, w_gu, w_down, w_plg, w_ple):
    d = w_in.shape[0]
    q_rank, kv_rank = g_qa.shape[0], g_kva.shape[0]
    o = q_rank + kv_rank
    kpe = w_in[:, o:o + MLA_ROPE]
    w_f = jnp.concatenate([w_in[:, :o], _pad_lanes(kpe), _pad_lanes(_swap_halves(kpe))], axis=1)
    w_b = w_in[:, o + MLA_ROPE:]
    na_w = NA_HEADS * NA_HEAD_DIM
    diff_w = DIFF_HEADS * 2 * DIFF_QK_DIM
    scale_b = np.ones((1, w_b.shape[1]), np.float32)
    scale_b[:, :na_w] = NA_HEAD_DIM ** -0.5 * LOG2E
    scale_b[:, 3 * na_w:3 * na_w + diff_w] = DIFF_QK_DIM ** -0.5 * LOG2E
    wq = w_qb.reshape(q_rank, MLA_HEADS, MLA_NOPE + MLA_ROPE)
    pe = wq[..., MLA_NOPE:]
    wq = jnp.concatenate([wq[..., :MLA_NOPE].reshape(q_rank, -1), _pad_lanes(pe).reshape(q_rank, -1),
                          _pad_lanes(_swap_halves(pe)).reshape(q_rank, -1)], axis=1)
    mla_w = MLA_HEADS * MLA_V
    return dict(
        w_in=jnp.concatenate([w_b, w_f], axis=1).astype(BF16), n_bf16=w_b.shape[1], scale_b=jnp.asarray(scale_b),
        wq=wq.astype(BF16), wkv=w_kvb.astype(BF16),
        na_bias=_na_bias_table(rpb),
        wo_mla=w_out[:mla_w].astype(BF16), wo_na=w_out[mla_w:mla_w + na_w].astype(BF16),
        wo_diff=w_out[mla_w + na_w:].astype(BF16),
        w_gu=w_gu.astype(BF16), w_down=w_down.astype(BF16),
        w_plg=w_plg.astype(BF16), w_ple=w_ple.astype(BF16),
        lam_init=jnp.full((1,), 0.8 - 0.6 * math.exp(-0.3 * i), F32),
    )


def _rope_tables(seq):
    half = MLA_ROPE // 2
    freqs = jnp.power(ROPE_THETA, -jnp.arange(half, dtype=F32) / half)
    ang = jnp.arange(seq, dtype=jnp.int32).astype(F32)[:, None] * freqs[None, :]
    cos, sin = jnp.cos(ang), jnp.sin(ang)
    return _pad_lanes(jnp.concatenate([cos, cos], axis=1)), _pad_lanes(jnp.concatenate([-sin, sin], axis=1))


def _trunk(x, p, layers, vecs, g_final):
    batch, seq, d = x.shape
    t = batch * seq
    h = x.reshape(t, d)
    rope_c, rope_s = _rope_tables(seq)
    slopes = jnp.exp2(-8.0 * jnp.arange(1, DIFF_HEADS + 1, dtype=F32) / DIFF_HEADS) * LOG2E
    nb = NA_HEADS * NA_HEAD_DIM // LANES
    for i, (lw, lv) in enumerate(zip(layers, vecs)):
        zb, zf = _in_proj(h, lv["g_attn"], lw["w_in"], lw["scale_b"], lw["n_bf16"])
        q, k, v = _mla_proj(zf, lv["g_qa"], lv["g_kva"], lw["wq"], lw["wkv"], rope_c, rope_s, seq)
        o_mla = _mla_flash(q, k, v, batch, seq)
        o_na = _na_attention(zb, (0, nb, 2 * nb), lw["na_bias"], batch, seq)
        o_diff = _diff_flash(zb, (3 * nb, 4 * nb, 5 * nb), slopes, lv["lam_q1"], lv["lam_k1"], lv["lam_q2"],
                             lv["lam_k2"], lv["g_subln"], lw["lam_init"], batch, seq)
        h = _out_proj(o_mla, o_na, o_diff, lw["wo_mla"], lw["wo_na"], lw["wo_diff"], h)
        h = _ffn(h, lv["g_ffn"], lw["w_gu"], lv["conv_w"], lv["conv_b"], lw["w_down"], seq)
        h = _ple(h, p[i].reshape(t, -1), lv["g_ple"], lw["w_plg"], lw["w_ple"], g_final.reshape(1, d),
                 final_norm=i == len(layers) - 1)
    return h.reshape(batch, seq, d)


def kernel(x_prompt, x_sample, p_prompt, p_sample, g_attn, w_in, g_qa, w_qb, g_kva, w_kvb, rpb, lam_q1, lam_k1,
           lam_q2, lam_k2, g_subln, w_out, g_ffn, w_gu, conv_w, conv_b, w_down, g_ple, w_plg, w_ple, g_final):
    depth = w_in.shape[0]
    layers = [_prepare_layer(i, w_in[i], g_qa[i], w_qb[i], g_kva[i], w_kvb[i], rpb[i], w_out[i], w_gu[i],
                             w_down[i], w_plg[i], w_ple[i]) for i in range(depth)]
    row = lambda a: a.reshape(1, -1)
    vecs = [dict(g_attn=row(g_attn[i]), g_qa=row(g_qa[i]), g_kva=row(g_kva[i]), lam_q1=row(lam_q1[i]),
                 lam_k1=row(lam_k1[i]), lam_q2=row(lam_q2[i]), lam_k2=row(lam_k2[i]), g_subln=row(g_subln[i]),
                 g_ffn=row(g_ffn[i]), conv_w=conv_w[i], conv_b=row(conv_b[i]), g_ple=row(g_ple[i]))
            for i in range(depth)]
    return (_trunk(x_prompt, p_prompt, layers, vecs, g_final), _trunk(x_sample, p_sample, layers, vecs, g_final))
```

```python
import functools
import math

import jax
import jax.numpy as jnp
import numpy as np
from jax import lax
from jax.experimental import pallas as pl
from jax.experimental.pallas import tpu as pltpu

F32 = jnp.float32
BF16 = jnp.bfloat16

GRID_W = 64
MLA_HEADS = 8
MLA_NOPE = 128
MLA_ROPE = 64
MLA_V = 128
MLA_QK_PAD = 256
ROPE_THETA = 10000.0
NA_HEADS = 4
NA_HEAD_DIM = 128
NA_WIN_ROWS = 8
NA_WIN_COLS = 16
NA_ROWS_PER_STEP = 32
NA_GROUP = 4
NA_UNION = NA_WIN_ROWS + NA_GROUP - 1
DIFF_HEADS = 4
DIFF_QK_DIM = 64
DIFF_V_DIM = 128
RMS_EPS = 1e-6
LOG2E = 1.4426950408889634
MASK_VALUE = -1e30
EXP2_UNDERFLOW = 160.0
NORM_BOUND_SLACK = 1.001
LANES = 128
BF16_ROWS = 16
FFN_HIDDEN_TILE = 512
VMEM_LIMIT = 56 * 1024 * 1024


def _tile(n, pref):
    t = min(n, pref)
    assert n % t == 0, (n, pref)
    return t


def _rms(x, g):
    ms = jnp.mean(x * x, axis=-1, keepdims=True)
    return x * lax.rsqrt(ms + RMS_EPS) * g


def _lane_tile(x, n):
    return jnp.concatenate([x] * n, axis=1)


def _params(*sem):
    return pltpu.CompilerParams(dimension_semantics=sem, vmem_limit_bytes=VMEM_LIMIT)


def _in_proj_kernel(x_ref, g_ref, w_ref, cs_ref, zb_ref, zf_ref, xn_ref, *, nb):
    j = pl.program_id(1)

    @pl.when(j == 0)
    def _():
        xn_ref[...] = _rms(x_ref[...], g_ref[...]).astype(BF16)

    acc = jnp.dot(xn_ref[...], w_ref[...], preferred_element_type=F32)

    @pl.when(j < nb)
    def _():
        zb_ref[...] = (acc * cs_ref[...]).astype(BF16)

    @pl.when(j >= nb)
    def _():
        zf_ref[...] = acc


def _in_proj(x, g, w, col_scale, n_bf16, tm=1024, tn=1024):
    t, d = x.shape
    n = w.shape[1]
    tm, tn = _tile(t, tm), _tile(math.gcd(n_bf16, n - n_bf16), tn)
    nb, nf = n_bf16 // tn, (n - n_bf16) // tn
    return pl.pallas_call(
        functools.partial(_in_proj_kernel, nb=nb),
        grid=(t // tm, nb + nf),
        in_specs=[
            pl.BlockSpec((tm, d), lambda i, j: (i, 0)),
            pl.BlockSpec((1, d), lambda i, j: (0, 0)),
            pl.BlockSpec((d, tn), lambda i, j: (0, j)),
            pl.BlockSpec((1, tn), lambda i, j: (0, jnp.minimum(j, nb - 1))),
        ],
        out_specs=[pl.BlockSpec((tm, tn), lambda i, j: (i, jnp.minimum(j, nb - 1))),
                   pl.BlockSpec((tm, tn), lambda i, j: (i, jnp.maximum(j - nb, 0)))],
        out_shape=[jax.ShapeDtypeStruct((t, n_bf16), BF16), jax.ShapeDtypeStruct((t, n - n_bf16), F32)],
        scratch_shapes=[pltpu.VMEM((tm, d), BF16)],
        compiler_params=_params("parallel", "arbitrary"),
        name="in_proj",
    )(x, g, w, col_scale)


def _mla_proj_kernel(zf_ref, gq_ref, gkv_ref, wq_ref, wkv_ref, c_ref, s_ref, q_ref, k_ref, v_ref, *,
                     q_rank, kv_rank, q_scale):
    zf = zf_ref[...]
    cos, sin = c_ref[...], s_ref[...]
    qn = _rms(zf[:, :q_rank], gq_ref[...]).astype(BF16)
    qa = jnp.dot(qn, wq_ref[...], preferred_element_type=F32)
    kvn = _rms(zf[:, q_rank:q_rank + kv_rank], gkv_ref[...]).astype(BF16)
    kva = jnp.dot(kvn, wkv_ref[...], preferred_element_type=F32)
    o = q_rank + kv_rank
    kpe = (zf[:, o:o + LANES] * cos + zf[:, o + LANES:o + 2 * LANES] * sin).astype(BF16)
    hp = MLA_HEADS * LANES
    for h in range(MLA_HEADS):
        lo = h * LANES
        rope = qa[:, hp + lo:hp + lo + LANES] * cos + qa[:, 2 * hp + lo:2 * hp + lo + LANES] * sin
        q_ref[:, 2 * lo:2 * lo + LANES] = (qa[:, lo:lo + LANES] * q_scale).astype(BF16)
        q_ref[:, 2 * lo + LANES:2 * lo + 2 * LANES] = (rope * q_scale).astype(BF16)
        k_ref[:, 2 * lo:2 * lo + LANES] = kva[:, 2 * lo:2 * lo + LANES].astype(BF16)
        k_ref[:, 2 * lo + LANES:2 * lo + 2 * LANES] = kpe
        v_ref[:, lo:lo + LANES] = kva[:, 2 * lo + LANES:2 * lo + 2 * LANES].astype(BF16)


def _mla_proj(zf, gq, gkv, wq, wkv, rope_c, rope_s, seq, tm=512):
    t, zw = zf.shape
    tm = _tile(seq, tm)
    nl = seq // tm
    q_rank, kv_rank = gq.shape[1], gkv.shape[1]
    kern = functools.partial(_mla_proj_kernel, q_rank=q_rank, kv_rank=kv_rank,
                             q_scale=(MLA_NOPE + MLA_ROPE) ** -0.5 * LOG2E)
    qk_w = MLA_HEADS * MLA_QK_PAD
    return pl.pallas_call(
        kern,
        grid=(t // tm,),
        in_specs=[
            pl.BlockSpec((tm, zw), lambda i: (i, 0)),
            pl.BlockSpec((1, q_rank), lambda i: (0, 0)),
            pl.BlockSpec((1, kv_rank), lambda i: (0, 0)),
            pl.BlockSpec(wq.shape, lambda i: (0, 0)),
            pl.BlockSpec(wkv.shape, lambda i: (0, 0)),
            pl.BlockSpec((tm, LANES), lambda i: (i % nl, 0)),
            pl.BlockSpec((tm, LANES), lambda i: (i % nl, 0)),
        ],
        out_specs=[
            pl.BlockSpec((tm, qk_w), lambda i: (i, 0)),
            pl.BlockSpec((tm, qk_w), lambda i: (i, 0)),
            pl.BlockSpec((tm, MLA_HEADS * MLA_V), lambda i: (i, 0)),
        ],
        out_shape=[
            jax.ShapeDtypeStruct((t, qk_w), BF16),
            jax.ShapeDtypeStruct((t, qk_w), BF16),
            jax.ShapeDtypeStruct((t, MLA_HEADS * MLA_V), BF16),
        ],
        compiler_params=_params("parallel"),
        name="mla_proj",
    )(zf, gq, gkv, wq, wkv, rope_c, rope_s)


def _online_softmax_step(s, v, m_ref, acc_ref, shift):
    m_prev = m_ref[...]
    m_cur = jnp.max(s, axis=1, keepdims=True)
    if shift is not None:
        m_cur = m_cur + shift
    m_next = jnp.maximum(m_prev, m_cur)
    alpha = jnp.exp2(m_prev - m_next)
    m_sub = m_next if shift is None else m_next - shift
    p = jnp.exp2(s - _lane_tile(m_sub, s.shape[1] // LANES))
    v1 = jnp.concatenate([v, jnp.ones_like(v)], axis=1)
    acc_ref[...] = _lane_tile(alpha, 2) * acc_ref[...] + jnp.dot(p.astype(BF16), v1, preferred_element_type=F32)
    m_ref[...] = m_next


def _init_softmax_state(m_ref, acc_ref):
    m_ref[...] = jnp.full(m_ref.shape, MASK_VALUE, F32)
    acc_ref[...] = jnp.zeros(acc_ref.shape, F32)


def _softmax_result(acc_ref):
    acc = acc_ref[...]
    return acc[:, :LANES] / acc[:, LANES:]


def _qk(q, k):
    return lax.dot_general(q, k, (((1,), (1,)), ((), ())), preferred_element_type=F32)


def _mla_flash_kernel(q_ref, k_ref, v_ref, o_ref, m_ref, acc_ref, *, tkc, unroll):
    _init_softmax_state(m_ref, acc_ref)
    q = q_ref[...]

    def block(bi, carry):
        for c in range(unroll):
            off = pl.multiple_of((bi * unroll + c) * tkc, tkc)
            s = _qk(q, k_ref[pl.ds(off, tkc), :])
            _online_softmax_step(s, v_ref[pl.ds(off, tkc), :], m_ref, acc_ref, None)
        return carry

    lax.fori_loop(0, k_ref.shape[0] // (tkc * unroll), block, 0)
    o_ref[...] = _softmax_result(acc_ref).astype(o_ref.dtype)


def _mla_flash(q, k, v, batch, seq, tq=1024, tkc=512, unroll=16):
    t = q.shape[0]
    tq, tkc = _tile(seq, tq), _tile(seq, tkc)
    unroll = _tile(seq // tkc, unroll)
    nq = seq // tq
    return pl.pallas_call(
        functools.partial(_mla_flash_kernel, tkc=tkc, unroll=unroll),
        grid=(MLA_HEADS, batch, nq),
        in_specs=[
            pl.BlockSpec((tq, MLA_QK_PAD), lambda g, b, qi: (b * nq + qi, g)),
            pl.BlockSpec((seq, MLA_QK_PAD), lambda g, b, qi: (b, g)),
            pl.BlockSpec((seq, MLA_V), lambda g, b, qi: (b, g)),
        ],
        out_specs=pl.BlockSpec((tq, MLA_V), lambda g, b, qi: (b * nq + qi, g)),
        out_shape=jax.ShapeDtypeStruct((t, MLA_HEADS * MLA_V), BF16),
        scratch_shapes=[pltpu.VMEM((tq, LANES), F32), pltpu.VMEM((tq, MLA_V + LANES), F32)],
        compiler_params=_params("parallel", "parallel", "parallel"),
        name="mla_flash",
    )(q, k, v)


def _key_norm_kernel(k_ref, o_ref):
    @pl.when(pl.program_id(1) == 0)
    def _():
        o_ref[...] = jnp.zeros(o_ref.shape, F32)

    k = k_ref[...].astype(F32)
    sq = k * k
    for g in range(DIFF_HEADS):
        top = jnp.max(jnp.sum(sq[:, g * LANES:(g + 1) * LANES], axis=1, keepdims=True), axis=0, keepdims=True)
        o_ref[g * 8:(g + 1) * 8, :] = jnp.maximum(o_ref[g * 8:(g + 1) * 8, :], top)


def _max_key_norm_sq(zb, kc, batch, seq, tk=2048):
    tk = _tile(seq, tk)
    nk = seq // tk
    assert kc % DIFF_HEADS == 0
    return pl.pallas_call(
        _key_norm_kernel,
        grid=(batch, nk),
        in_specs=[pl.BlockSpec((tk, DIFF_HEADS * LANES), lambda b, i: (b * nk + i, kc // DIFF_HEADS))],
        out_specs=pl.BlockSpec((DIFF_HEADS * 8, LANES), lambda b, i: (b, 0)),
        out_shape=jax.ShapeDtypeStruct((batch * DIFF_HEADS * 8, LANES), F32),
        compiler_params=_params("parallel", "arbitrary"),
        name="key_norm",
    )(zb)


def _diff_flash_kernel(slope_ref, kn_ref, lq1_ref, lk1_ref, lq2_ref, lk2_ref, gsub_ref, q_ref, k_ref, v_ref, o_ref,
                       qs_ref, m_ref, acc_ref, bias_ref, cap_ref, qx_ref, kx_ref, *, tq, tkb, lam_init):
    g, qi = pl.program_id(0), pl.program_id(2)
    slope = slope_ref[g]
    nk = k_ref.shape[0] // tkb
    nch = tkb // tq
    q0 = qi * tq

    @pl.when((qi == 0) & (pl.program_id(1) == 0))
    def _():
        shape = (2 * tq, tq)
        row = lax.broadcasted_iota(jnp.int32, shape, 0)
        row = jnp.where(row >= tq, row - tq, row)
        d = (row - lax.broadcasted_iota(jnp.int32, shape, 1)).astype(F32)
        bias_ref[0] = -slope * d
        bias_ref[1] = -slope * jnp.abs(d)
        bias_ref[2] = slope * d
        whole = jnp.full((1, LANES), slope, F32)
        p1 = whole.astype(BF16).astype(F32)
        p2 = (whole - p1).astype(BF16).astype(F32)
        p3 = (whole - p1 - p2).astype(BF16).astype(F32)
        lane = lax.broadcasted_iota(jnp.int32, (tq, LANES), 1)
        pos = lax.broadcasted_iota(jnp.int32, (tq, LANES), 0)
        even = (lane & 1) == 0
        digit = jnp.where(even, pos >> 4, pos & 15).astype(F32)
        pair = jnp.where(lane >= 6, lane - 6, lane) >> 1
        piece = jnp.where(pair == 0, p1, jnp.where(pair == 1, p2, p3))
        weight = jnp.where(even, 16.0 * piece, piece)
        kx_ref[...] = jnp.where(lane < 6, weight, jnp.where(lane < 12, digit, 0.0)).astype(BF16)
        after = jnp.where(lane < 6, digit, jnp.where(lane < 12, -weight, 0.0)).astype(BF16)
        for half in range(2):
            qx_ref[0, half * tq:(half + 1) * tq, LANES:] = -after
            qx_ref[1, half * tq:(half + 1) * tq, LANES:] = after

    _init_softmax_state(m_ref, acc_ref)
    q = q_ref[...]
    lane = lax.broadcasted_iota(jnp.int32, q.shape, 1)
    qs_ref[:tq] = jnp.where(lane < DIFF_QK_DIM, q, jnp.zeros_like(q))
    qs_ref[tq:] = jnp.where(lane >= DIFF_QK_DIM, q, jnp.zeros_like(q))
    for side in range(2):
        qx_ref[side, :, :LANES] = qs_ref[...]
    qf = q.astype(F32)
    q_norm = jnp.sqrt(jnp.sum(qf * qf, axis=1, keepdims=True))
    cap = q_norm * jnp.sqrt(kn_ref[0:1, :]) * NORM_BOUND_SLACK + NORM_BOUND_SLACK
    cap_ref[:tq] = cap
    cap_ref[tq:] = cap

    unrolled_from = min(nch, (5 * nch + 7) // 8)

    def block(ki, far):
        kb0 = ((q0 // tkb + ki) % nk) * tkb
        best = jnp.max(cap_ref[...] - m_ref[...], axis=0, keepdims=True)
        cidx = lax.broadcasted_iota(jnp.int32, (8, LANES), 0)
        kc0 = kb0 + cidx * tq
        nearest = jnp.maximum(jnp.maximum(q0 - (kc0 + tq) + 1, kc0 - (q0 + tq) + 1), 0)
        live = (cidx < nch) & (best >= slope * nearest.astype(F32) - EXP2_UNDERFLOW)
        n_live = jnp.sum(jnp.where(live, 1, 0), axis=0, keepdims=True)[0, 0]
        first_live = jnp.min(jnp.where(live, cidx, nch), axis=0, keepdims=True)[0, 0]

        def chunk(c):
            k0 = pl.multiple_of(kb0 + c * tq, tq)
            gap = jnp.full((1, LANES), jnp.abs(q0 - k0), jnp.int32).astype(F32)
            if far:
                keys = jnp.concatenate([k_ref[pl.ds(k0, tq), :], kx_ref[...]], axis=1)
                s = _qk(qx_ref[jnp.where(k0 < q0, 0, 1)], keys)
            else:
                which = jnp.where(k0 < q0, 0, jnp.where(k0 == q0, 1, 2))
                s = _qk(qs_ref[...], k_ref[pl.ds(k0, tq), :]) + bias_ref[which]
            _online_softmax_step(s, v_ref[pl.ds(k0, tq), :], m_ref, acc_ref, -slope * gap)

        @pl.when(n_live >= unrolled_from)
        def _():
            for c in range(nch):
                chunk(c)

        @pl.when((n_live > 0) & (n_live < unrolled_from))
        def _():
            def body(c, inner):
                chunk(c)
                return inner

            lax.fori_loop(first_live, first_live + n_live, body, 0)

    block(0, far=False)

    def far_block(ki, carry):
        block(ki, far=True)
        return carry

    lax.fori_loop(1, nk, far_block, 0)

    o = _softmax_result(acc_ref)
    lam = (jnp.exp(jnp.sum(lq1_ref[...] * lk1_ref[...], axis=-1, keepdims=True))
           - jnp.exp(jnp.sum(lq2_ref[...] * lk2_ref[...], axis=-1, keepdims=True)) + lam_init)
    d = o[:tq] - lam * o[tq:]
    o_ref[...] = (_rms(d, gsub_ref[...]) * (1.0 - lam_init)).astype(o_ref.dtype)


def _diff_flash(zb, col0, slopes, lq1, lk1, lq2, lk2, gsub, lam_init, batch, seq, tq=512, tkb=4096):
    t = zb.shape[0]
    tq = _tile(seq, tq)
    tkb = _tile(seq, tkb)
    assert tkb % tq == 0 and tkb // tq <= 8
    nq = seq // tq
    qc, kc, vc = col0
    key_norm = _max_key_norm_sq(zb, kc, batch, seq)
    vec = lambda a: pl.BlockSpec(a.shape, lambda g, b, qi: (0, 0))
    return pl.pallas_call(
        functools.partial(_diff_flash_kernel, tq=tq, tkb=tkb, lam_init=lam_init),
        grid=(DIFF_HEADS, batch, nq),
        in_specs=[
            pl.BlockSpec(memory_space=pltpu.SMEM),
            pl.BlockSpec((8, LANES), lambda g, b, qi: (b * DIFF_HEADS + g, 0)),
            vec(lq1), vec(lk1), vec(lq2), vec(lk2), vec(gsub),
            pl.BlockSpec((tq, LANES), lambda g, b, qi: (b * nq + qi, qc + g)),
            pl.BlockSpec((seq, LANES), lambda g, b, qi: (b, kc + g)),
            pl.BlockSpec((seq, DIFF_V_DIM), lambda g, b, qi: (b, vc + g)),
        ],
        out_specs=pl.BlockSpec((tq, DIFF_V_DIM), lambda g, b, qi: (b * nq + qi, g)),
        out_shape=jax.ShapeDtypeStruct((t, DIFF_HEADS * DIFF_V_DIM), BF16),
        scratch_shapes=[pltpu.VMEM((2 * tq, LANES), BF16), pltpu.VMEM((2 * tq, LANES), F32),
                        pltpu.VMEM((2 * tq, DIFF_V_DIM + LANES), F32), pltpu.VMEM((3, 2 * tq, tq), F32),
                        pltpu.VMEM((2 * tq, LANES), F32), pltpu.VMEM((2, 2 * tq, 2 * LANES), BF16),
                        pltpu.VMEM((tq, LANES), BF16)],
        compiler_params=_params("parallel", "arbitrary", "arbitrary"),
        name="diff_flash",
    )(slopes, key_norm, lq1, lk1, lq2, lk2, gsub, zb, zb, zb)


def _na_kernel(q_ref, kp_ref, kc_ref, kn_ref, vp_ref, vc_ref, vn_ref, bias_ref, o_ref, kb_ref, vb_ref, *, rows):
    j = pl.program_id(2)
    blk = NA_ROWS_PER_STEP * GRID_W
    for n, (kr, vr) in enumerate(((kp_ref, vp_ref), (kc_ref, vc_ref), (kn_ref, vn_ref))):
        kb_ref[n * blk:(n + 1) * blk] = kr[...]
        vb_ref[n * blk:(n + 1) * blk] = vr[...]
    n_groups = rows // NA_GROUP
    gq = NA_GROUP * GRID_W
    for gi in range(NA_ROWS_PER_STEP // NA_GROUP):
        group = j * (NA_ROWS_PER_STEP // NA_GROUP) + gi
        union_start = jnp.clip(group * NA_GROUP - NA_WIN_ROWS // 2, 0, rows - NA_UNION)
        start = pl.multiple_of((union_start - (j - 1) * NA_ROWS_PER_STEP) * GRID_W, GRID_W)
        variant = jnp.where(group == 0, 0, jnp.where(group == n_groups - 1, 2, 1))
        q = q_ref[gi * gq:(gi + 1) * gq, :]
        s = _qk(q, kb_ref[pl.ds(start, NA_UNION * GRID_W), :]) + bias_ref[variant]
        p = jnp.exp2(s - jnp.max(s, axis=1, keepdims=True))
        v = vb_ref[pl.ds(start, NA_UNION * GRID_W), :]
        acc = jnp.dot(p.astype(BF16), jnp.concatenate([v, jnp.ones_like(v)], axis=1), preferred_element_type=F32)
        o_ref[gi * gq:(gi + 1) * gq, :] = (acc[:, :NA_HEAD_DIM] / acc[:, NA_HEAD_DIM:]).astype(o_ref.dtype)


def _na_attention(zb, col0, bias, batch, seq):
    t = zb.shape[0]
    rows = seq // GRID_W
    assert rows % NA_ROWS_PER_STEP == 0
    nj = rows // NA_ROWS_PER_STEP
    blk = NA_ROWS_PER_STEP * GRID_W
    qc, kc, vc = col0

    def spec(c0, dj):
        return pl.BlockSpec((blk, NA_HEAD_DIM),
                            lambda h, b, j: (b * nj + jnp.clip(j + dj, 0, nj - 1), c0 + h))

    return pl.pallas_call(
        functools.partial(_na_kernel, rows=rows),
        grid=(NA_HEADS, batch, nj),
        in_specs=[spec(qc, 0), spec(kc, -1), spec(kc, 0), spec(kc, 1), spec(vc, -1), spec(vc, 0), spec(vc, 1),
                  pl.BlockSpec((None,) + bias.shape[1:], lambda h, b, j: (h, 0, 0, 0))],
        out_specs=pl.BlockSpec((blk, NA_HEAD_DIM), lambda h, b, j: (b * nj + j, h)),
        out_shape=jax.ShapeDtypeStruct((t, NA_HEADS * NA_HEAD_DIM), BF16),
        scratch_shapes=[pltpu.VMEM((3 * blk, NA_HEAD_DIM), BF16), pltpu.VMEM((3 * blk, NA_HEAD_DIM), BF16)],
        compiler_params=_params("parallel", "parallel", "parallel"),
        name="na_attention",
    )(zb, zb, zb, zb, zb, zb, zb, bias)


def _na_bias_table(rpb):
    col = np.arange(GRID_W)
    col_start = np.clip(col - NA_WIN_COLS // 2, 0, GRID_W - NA_WIN_COLS)
    kc = np.arange(GRID_W)
    valid = (kc[None, :] >= col_start[:, None]) & (kc[None, :] < col_start[:, None] + NA_WIN_COLS)
    by_row = jnp.stack([rpb[:, NA_WIN_ROWS - 1 - t:2 * NA_WIN_ROWS - 1 - t] for t in range(NA_WIN_ROWS)], axis=1)
    pad = GRID_W - NA_WIN_COLS
    padded = jnp.pad(by_row, ((0, 0), (0, 0), (0, 0), (pad, pad)))
    tbl = jnp.stack([padded[..., GRID_W - 1 - c:2 * GRID_W - 1 - c] for c in range(GRID_W)], axis=2)
    tbl = jnp.where(valid[None, None, :, None, :], tbl * LOG2E, MASK_VALUE)
    per_row = tbl.reshape(rpb.shape[0], NA_WIN_ROWS, GRID_W, NA_WIN_ROWS * GRID_W).astype(F32)

    extra = NA_UNION - NA_WIN_ROWS

    def place(t, offset):
        return jnp.pad(per_row[:, t], ((0, 0), (0, 0), (offset * GRID_W, (extra - offset) * GRID_W)),
                       constant_values=MASK_VALUE)

    half = NA_WIN_ROWS // 2
    first = jnp.concatenate([place(i, 0) for i in range(NA_GROUP)], axis=1)
    inner = jnp.concatenate([place(half, i) for i in range(NA_GROUP)], axis=1)
    last = jnp.concatenate([place(half + i, extra) for i in range(NA_GROUP)], axis=1)
    return jnp.stack([first, inner, last], axis=1)


def _out_proj_kernel(a_ref, b_ref, c_ref, wa_ref, wb_ref, wc_ref, h_ref, o_ref, *, tn):
    a, b, c = a_ref[...], b_ref[...], c_ref[...]
    for j in range(o_ref.shape[1] // tn):
        cols = slice(j * tn, (j + 1) * tn)
        acc = jnp.dot(a, wa_ref[:, cols], preferred_element_type=F32)
        acc = acc + jnp.dot(b, wb_ref[:, cols], preferred_element_type=F32)
        acc = acc + jnp.dot(c, wc_ref[:, cols], preferred_element_type=F32)
        o_ref[:, cols] = h_ref[:, cols] + acc


def _out_proj(a, b, c, wa, wb, wc, h, tm=512, tn=512):
    t, d = h.shape
    tm, tn = _tile(t, tm), _tile(d, tn)
    rows = lambda x: pl.BlockSpec((tm, x.shape[1]), lambda i: (i, 0))
    whole = lambda w: pl.BlockSpec(w.shape, lambda i: (0, 0))
    return pl.pallas_call(
        functools.partial(_out_proj_kernel, tn=tn),
        grid=(t // tm,),
        in_specs=[rows(a), rows(b), rows(c), whole(wa), whole(wb), whole(wc), rows(h)],
        out_specs=rows(h),
        out_shape=jax.ShapeDtypeStruct((t, d), F32),
        compiler_params=_params("parallel"),
        name="out_proj",
    )(a, b, c, wa, wb, wc, h)


def _ffn_kernel(h_ref, hp_ref, hn_ref, g_ref, wgu_ref, cwb_ref, wd_ref, o_ref, xn_ref, *, tm, tiles_per_seq):
    i, j = pl.program_id(0), pl.program_id(1)

    @pl.when(j == 0)
    def _():
        x = h_ref[...]
        g = g_ref[...]
        xn_ref[:tm] = _rms(x, g).astype(BF16)
        pos = i % tiles_per_seq
        before = jnp.where(pos == 0, 0.0, _rms(hp_ref[...], g)[BF16_ROWS - 1:BF16_ROWS])
        after = jnp.where(pos == tiles_per_seq - 1, 0.0, _rms(hn_ref[...], g)[0:1])
        rid = lax.broadcasted_iota(jnp.int32, (BF16_ROWS, x.shape[1]), 0)
        halo = jnp.where(rid == 0, before, jnp.where(rid == 1, after, 0.0))
        xn_ref[tm:] = halo.astype(BF16)
        o_ref[...] = x

    tf = wd_ref.shape[0]
    ga = jnp.dot(xn_ref[...], wgu_ref[:, :tf], preferred_element_type=F32)
    up = jnp.dot(xn_ref[:tm], wgu_ref[:, tf:], preferred_element_type=F32)
    gate = ga[:tm]
    rid = lax.broadcasted_iota(jnp.int32, gate.shape, 0)
    g_prev = jnp.where(rid == 0, ga[tm:tm + 1], pltpu.roll(gate, 1, 0))
    g_next = jnp.where(rid == tm - 1, ga[tm + 1:tm + 2], pltpu.roll(gate, tm - 1, 0))
    cw = cwb_ref[...]
    conv = g_prev * cw[0:1] + gate * cw[1:2] + g_next * cw[2:3] + cw[3:4]
    act = (jax.nn.gelu(conv) * up).astype(BF16)
    o_ref[...] += jnp.dot(act, wd_ref[...], preferred_element_type=F32)


def _ffn_tile(f):
    return _tile(f, FFN_HIDDEN_TILE)


def _ffn(h, g, w_gu_tiles, conv_wb, w_down, seq, tm=512):
    t, d = h.shape
    nf, _, tf2 = w_gu_tiles.shape
    tf = tf2 // 2
    tm = _tile(seq, tm)
    hb = tm // BF16_ROWS
    nhb = t // BF16_ROWS
    return pl.pallas_call(
        functools.partial(_ffn_kernel, tm=tm, tiles_per_seq=seq // tm),
        grid=(t // tm, nf),
        in_specs=[
            pl.BlockSpec((tm, d), lambda i, j: (i, 0)),
            pl.BlockSpec((BF16_ROWS, d), lambda i, j: (jnp.maximum(i * hb - 1, 0), 0)),
            pl.BlockSpec((BF16_ROWS, d), lambda i, j: (jnp.minimum((i + 1) * hb, nhb - 1), 0)),
            pl.BlockSpec((1, d), lambda i, j: (0, 0)),
            pl.BlockSpec((None, d, tf2), lambda i, j: (j, 0, 0)),
            pl.BlockSpec((conv_wb.shape[0], tf), lambda i, j: (0, j)),
            pl.BlockSpec((tf, d), lambda i, j: (j, 0)),
        ],
        out_specs=pl.BlockSpec((tm, d), lambda i, j: (i, 0)),
        out_shape=jax.ShapeDtypeStruct((t, d), F32),
        scratch_shapes=[pltpu.VMEM((tm + BF16_ROWS, d), BF16)],
        compiler_params=_params("parallel", "arbitrary"),
        name="ffn",
    )(h, h, h, g, w_gu_tiles, conv_wb, w_down)


def _ple_kernel(h_ref, p_ref, g_ref, wg_ref, wp_ref, gf_ref, o_ref, *, tn, final_norm):
    h = h_ref[...]
    xn = _rms(h, g_ref[...]).astype(BF16)
    pb = p_ref[...].astype(BF16)
    for j in range(h.shape[1] // tn):
        cols = slice(j * tn, (j + 1) * tn)
        gate = jax.nn.sigmoid(jnp.dot(xn, wg_ref[:, cols], preferred_element_type=F32))
        emb = jnp.dot(pb, wp_ref[:, cols], preferred_element_type=F32)
        o_ref[:, cols] = h[:, cols] + emb * gate
    if final_norm:
        o_ref[...] = _rms(o_ref[...], gf_ref[...])


def _ple(h, p, g, w_plg, w_ple, g_final, final_norm, tm=512, tn=512):
    t, d = h.shape
    tm, tn = _tile(t, tm), _tile(d, tn)
    whole = lambda a: pl.BlockSpec(a.shape, lambda i: (0, 0))
    return pl.pallas_call(
        functools.partial(_ple_kernel, tn=tn, final_norm=final_norm),
        grid=(t // tm,),
        in_specs=[
            pl.BlockSpec((tm, d), lambda i: (i, 0)),
            pl.BlockSpec((tm, p.shape[1]), lambda i: (i, 0)),
            whole(g), whole(w_plg), whole(w_ple), whole(g_final),
        ],
        out_specs=pl.BlockSpec((tm, d), lambda i: (i, 0)),
        out_shape=jax.ShapeDtypeStruct((t, d), F32),
        compiler_params=_params("parallel"),
        name="ple",
    )(h, p, g, w_plg, w_ple, g_final)


def _swap_halves(x):
    half = x.shape[-1] // 2
    return jnp.concatenate([x[..., half:], x[..., :half]], axis=-1)


def _pad_lanes(x):
    return jnp.concatenate([x, jnp.zeros(x.shape[:-1] + (LANES - x.shape[-1],), x.dtype)], axis=-1)


def _gate_up_tiles(w_gu):
    d, f2 = w_gu.shape
    tf = _ffn_tile(f2 // 2)
    halves = w_gu.reshape(d, 2, f2 // 2 // tf, tf)
    return halves.transpose(2, 0, 1, 3).reshape(f2 // 2 // tf, d, 2 * tf)


def _prepare_layer(i, w_in, g_qa, w_qb, g_kva, w_kvb, rpb, w_out, w_gu, w_down, w_plg, w_ple):
    d = w_in.shape[0]
    q_rank, kv_rank = g_qa.shape[0], g_kva.shape[0]
    o = q_rank + kv_rank
    kpe = w_in[:, o:o + MLA_ROPE]
    w_f = jnp.concatenate([w_in[:, :o], _pad_lanes(kpe), _pad_lanes(_swap_halves(kpe))], axis=1)
    w_b = w_in[:, o + MLA_ROPE:]
    na_w = NA_HEADS * NA_HEAD_DIM
    diff_w = DIFF_HEADS * 2 * DIFF_QK_DIM
    scale_b = np.ones((1, w_b.shape[1]), np.float32)
    scale_b[:, :na_w] = NA_HEAD_DIM ** -0.5 * LOG2E
    scale_b[:, 3 * na_w:3 * na_w + diff_w] = DIFF_QK_DIM ** -0.5 * LOG2E
    wq = w_qb.reshape(q_rank, MLA_HEADS, MLA_NOPE + MLA_ROPE)
    pe = wq[..., MLA_NOPE:]
    wq = jnp.concatenate([wq[..., :MLA_NOPE].reshape(q_rank, -1), _pad_lanes(pe).reshape(q_rank, -1),
                          _pad_lanes(_swap_halves(pe)).reshape(q_rank, -1)], axis=1)
    mla_w = MLA_HEADS * MLA_V
    return dict(
        w_in=jnp.concatenate([w_b, w_f], axis=1).astype(BF16), n_bf16=w_b.shape[1], scale_b=jnp.asarray(scale_b),
        wq=wq.astype(BF16), wkv=w_kvb.astype(BF16),
        na_bias=_na_bias_table(rpb),
        wo_mla=w_out[:mla_w].astype(BF16), wo_na=w_out[mla_w:mla_w + na_w].astype(BF16),
        wo_diff=w_out[mla_w + na_w:].astype(BF16),
        w_gu=_gate_up_tiles(w_gu.astype(BF16)), w_down=w_down.astype(BF16),
        w_plg=w_plg.astype(BF16), w_ple=w_ple.astype(BF16),
        lam_init=0.8 - 0.6 * math.exp(-0.3 * i),
    )


def _rope_tables(seq):
    half = MLA_ROPE // 2
    freqs = jnp.power(ROPE_THETA, -jnp.arange(half, dtype=F32) / half)
    ang = jnp.arange(seq, dtype=jnp.int32).astype(F32)[:, None] * freqs[None, :]
    cos, sin = jnp.cos(ang), jnp.sin(ang)
    return _pad_lanes(jnp.concatenate([cos, cos], axis=1)), _pad_lanes(jnp.concatenate([-sin, sin], axis=1))


def _trunk(x, p, layers, vecs, g_final):
    batch, seq, d = x.shape
    t = batch * seq
    h = x.reshape(t, d)
    rope_c, rope_s = _rope_tables(seq)
    slopes = jnp.exp2(-8.0 * jnp.arange(1, DIFF_HEADS + 1, dtype=F32) / DIFF_HEADS) * LOG2E
    nb = NA_HEADS * NA_HEAD_DIM // LANES
    for i, (lw, lv) in enumerate(zip(layers, vecs)):
        zb, zf = _in_proj(h, lv["g_attn"], lw["w_in"], lw["scale_b"], lw["n_bf16"])
        q, k, v = _mla_proj(zf, lv["g_qa"], lv["g_kva"], lw["wq"], lw["wkv"], rope_c, rope_s, seq)
        o_mla = _mla_flash(q, k, v, batch, seq)
        o_na = _na_attention(zb, (0, nb, 2 * nb), lw["na_bias"], batch, seq)
        o_diff = _diff_flash(zb, (3 * nb, 4 * nb, 5 * nb), slopes, lv["lam_q1"], lv["lam_k1"], lv["lam_q2"],
                             lv["lam_k2"], lv["g_subln"], lw["lam_init"], batch, seq)
        h = _out_proj(o_mla, o_na, o_diff, lw["wo_mla"], lw["wo_na"], lw["wo_diff"], h)
        h = _ffn(h, lv["g_ffn"], lw["w_gu"], lv["conv_wb"], lw["w_down"], seq)
        h = _ple(h, p[i].reshape(t, -1), lv["g_ple"], lw["w_plg"], lw["w_ple"], g_final.reshape(1, d),
                 final_norm=i == len(layers) - 1)
    return h.reshape(batch, seq, d)


def kernel(x_prompt, x_sample, p_prompt, p_sample, g_attn, w_in, g_qa, w_qb, g_kva, w_kvb, rpb, lam_q1, lam_k1,
           lam_q2, lam_k2, g_subln, w_out, g_ffn, w_gu, conv_w, conv_b, w_down, g_ple, w_plg, w_ple, g_final):
    depth = w_in.shape[0]
    layers = [_prepare_layer(i, w_in[i], g_qa[i], w_qb[i], g_kva[i], w_kvb[i], rpb[i], w_out[i], w_gu[i],
                             w_down[i], w_plg[i], w_ple[i]) for i in range(depth)]
    row = lambda a: a.reshape(1, -1)
    vecs = [dict(g_attn=row(g_attn[i]), g_qa=row(g_qa[i]), g_kva=row(g_kva[i]), lam_q1=row(lam_q1[i]),
                 lam_k1=row(lam_k1[i]), lam_q2=row(lam_q2[i]), lam_k2=row(lam_k2[i]), g_subln=row(g_subln[i]),
                 g_ffn=row(g_ffn[i]), conv_wb=jnp.concatenate([conv_w[i], row(conv_b[i])], axis=0),
                 g_ple=row(g_ple[i]))
            for i in range(depth)]
    return (_trunk(x_prompt, p_prompt, layers, vecs, g_final), _trunk(x_sample, p_sample, layers, vecs, g_final))
```

```python
import functools
import math

import jax
import jax.numpy as jnp
import numpy as np
from jax import lax
from jax.experimental import pallas as pl
from jax.experimental.pallas import tpu as pltpu

F32 = jnp.float32
BF16 = jnp.bfloat16

GRID_W = 64
MLA_HEADS = 8
MLA_NOPE = 128
MLA_ROPE = 64
MLA_V = 128
MLA_QK_PAD = 256
ROPE_THETA = 10000.0
NA_HEADS = 4
NA_HEAD_DIM = 128
NA_WIN_ROWS = 8
NA_WIN_COLS = 16
NA_ROWS_PER_STEP = 32
NA_GROUP = 4
NA_UNION = NA_WIN_ROWS + NA_GROUP - 1
DIFF_HEADS = 4
DIFF_QK_DIM = 64
DIFF_V_DIM = 128
RMS_EPS = 1e-6
LOG2E = 1.4426950408889634
MASK_VALUE = -1e30
EXP2_UNDERFLOW = 160.0
NORM_BOUND_SLACK = 1.001
LANES = 128
BF16_ROWS = 16
VMEM_LIMIT = 56 * 1024 * 1024


def _tile(n, pref):
    t = min(n, pref)
    assert n % t == 0, (n, pref)
    return t


def _rms(x, g):
    ms = jnp.mean(x * x, axis=-1, keepdims=True)
    return x * lax.rsqrt(ms + RMS_EPS) * g


def _lane_tile(x, n):
    return jnp.concatenate([x] * n, axis=1)


def _params(*sem):
    return pltpu.CompilerParams(dimension_semantics=sem, vmem_limit_bytes=VMEM_LIMIT)


def _in_proj_kernel(x_ref, g_ref, w_ref, cs_ref, zb_ref, zf_ref, xn_ref, *, nb):
    j = pl.program_id(1)

    @pl.when(j == 0)
    def _():
        xn_ref[...] = _rms(x_ref[...], g_ref[...]).astype(BF16)

    acc = jnp.dot(xn_ref[...], w_ref[...], preferred_element_type=F32)

    @pl.when(j < nb)
    def _():
        zb_ref[...] = (acc * cs_ref[...]).astype(BF16)

    @pl.when(j >= nb)
    def _():
        zf_ref[...] = acc


def _in_proj(x, g, w, col_scale, n_bf16, tm=1024, tn=1024):
    t, d = x.shape
    n = w.shape[1]
    tm, tn = _tile(t, tm), _tile(math.gcd(n_bf16, n - n_bf16), tn)
    nb, nf = n_bf16 // tn, (n - n_bf16) // tn
    return pl.pallas_call(
        functools.partial(_in_proj_kernel, nb=nb),
        grid=(t // tm, nb + nf),
        in_specs=[
            pl.BlockSpec((tm, d), lambda i, j: (i, 0)),
            pl.BlockSpec((1, d), lambda i, j: (0, 0)),
            pl.BlockSpec((d, tn), lambda i, j: (0, j)),
            pl.BlockSpec((1, tn), lambda i, j: (0, jnp.minimum(j, nb - 1))),
        ],
        out_specs=[pl.BlockSpec((tm, tn), lambda i, j: (i, jnp.minimum(j, nb - 1))),
                   pl.BlockSpec((tm, tn), lambda i, j: (i, jnp.maximum(j - nb, 0)))],
        out_shape=[jax.ShapeDtypeStruct((t, n_bf16), BF16), jax.ShapeDtypeStruct((t, n - n_bf16), F32)],
        scratch_shapes=[pltpu.VMEM((tm, d), BF16)],
        compiler_params=_params("parallel", "arbitrary"),
        name="in_proj",
    )(x, g, w, col_scale)


def _mla_proj_kernel(zf_ref, gq_ref, gkv_ref, wq_ref, wkv_ref, c_ref, s_ref, q_ref, k_ref, v_ref, *,
                     q_rank, kv_rank, q_scale):
    zf = zf_ref[...]
    cos, sin = c_ref[...], s_ref[...]
    qn = _rms(zf[:, :q_rank], gq_ref[...]).astype(BF16)
    qa = jnp.dot(qn, wq_ref[...], preferred_element_type=F32)
    kvn = _rms(zf[:, q_rank:q_rank + kv_rank], gkv_ref[...]).astype(BF16)
    kva = jnp.dot(kvn, wkv_ref[...], preferred_element_type=F32)
    o = q_rank + kv_rank
    kpe = (zf[:, o:o + LANES] * cos + zf[:, o + LANES:o + 2 * LANES] * sin).astype(BF16)
    hp = MLA_HEADS * LANES
    for h in range(MLA_HEADS):
        lo = h * LANES
        rope = qa[:, hp + lo:hp + lo + LANES] * cos + qa[:, 2 * hp + lo:2 * hp + lo + LANES] * sin
        q_ref[:, 2 * lo:2 * lo + LANES] = (qa[:, lo:lo + LANES] * q_scale).astype(BF16)
        q_ref[:, 2 * lo + LANES:2 * lo + 2 * LANES] = (rope * q_scale).astype(BF16)
        k_ref[:, 2 * lo:2 * lo + LANES] = kva[:, 2 * lo:2 * lo + LANES].astype(BF16)
        k_ref[:, 2 * lo + LANES:2 * lo + 2 * LANES] = kpe
        v_ref[:, lo:lo + LANES] = kva[:, 2 * lo + LANES:2 * lo + 2 * LANES].astype(BF16)


def _mla_proj(zf, gq, gkv, wq, wkv, rope_c, rope_s, seq, tm=512):
    t, zw = zf.shape
    tm = _tile(seq, tm)
    nl = seq // tm
    q_rank, kv_rank = gq.shape[1], gkv.shape[1]
    kern = functools.partial(_mla_proj_kernel, q_rank=q_rank, kv_rank=kv_rank,
                             q_scale=(MLA_NOPE + MLA_ROPE) ** -0.5 * LOG2E)
    qk_w = MLA_HEADS * MLA_QK_PAD
    return pl.pallas_call(
        kern,
        grid=(t // tm,),
        in_specs=[
            pl.BlockSpec((tm, zw), lambda i: (i, 0)),
            pl.BlockSpec((1, q_rank), lambda i: (0, 0)),
            pl.BlockSpec((1, kv_rank), lambda i: (0, 0)),
            pl.BlockSpec(wq.shape, lambda i: (0, 0)),
            pl.BlockSpec(wkv.shape, lambda i: (0, 0)),
            pl.BlockSpec((tm, LANES), lambda i: (i % nl, 0)),
            pl.BlockSpec((tm, LANES), lambda i: (i % nl, 0)),
        ],
        out_specs=[
            pl.BlockSpec((tm, qk_w), lambda i: (i, 0)),
            pl.BlockSpec((tm, qk_w), lambda i: (i, 0)),
            pl.BlockSpec((tm, MLA_HEADS * MLA_V), lambda i: (i, 0)),
        ],
        out_shape=[
            jax.ShapeDtypeStruct((t, qk_w), BF16),
            jax.ShapeDtypeStruct((t, qk_w), BF16),
            jax.ShapeDtypeStruct((t, MLA_HEADS * MLA_V), BF16),
        ],
        compiler_params=_params("parallel"),
        name="mla_proj",
    )(zf, gq, gkv, wq, wkv, rope_c, rope_s)


def _online_softmax_step(s, v, m_ref, acc_ref, shift):
    m_prev = m_ref[...]
    m_cur = jnp.max(s, axis=1, keepdims=True)
    if shift is not None:
        m_cur = m_cur + shift
    m_next = jnp.maximum(m_prev, m_cur)
    alpha = jnp.exp2(m_prev - m_next)
    m_sub = m_next if shift is None else m_next - shift
    p = jnp.exp2(s - _lane_tile(m_sub, s.shape[1] // LANES))
    v1 = jnp.concatenate([v, jnp.ones_like(v)], axis=1)
    acc_ref[...] = _lane_tile(alpha, 2) * acc_ref[...] + jnp.dot(p.astype(BF16), v1, preferred_element_type=F32)
    m_ref[...] = m_next


def _init_softmax_state(m_ref, acc_ref):
    m_ref[...] = jnp.full(m_ref.shape, MASK_VALUE, F32)
    acc_ref[...] = jnp.zeros(acc_ref.shape, F32)


def _softmax_result(acc_ref):
    acc = acc_ref[...]
    return acc[:, :LANES] / acc[:, LANES:]


def _qk(q, k):
    return lax.dot_general(q, k, (((1,), (1,)), ((), ())), preferred_element_type=F32)


def _mla_flash_kernel(q_ref, k_ref, v_ref, o_ref, m_ref, acc_ref, *, tkc, unroll):
    _init_softmax_state(m_ref, acc_ref)
    q = q_ref[...]

    def block(bi, carry):
        for c in range(unroll):
            off = pl.multiple_of((bi * unroll + c) * tkc, tkc)
            s = _qk(q, k_ref[pl.ds(off, tkc), :])
            _online_softmax_step(s, v_ref[pl.ds(off, tkc), :], m_ref, acc_ref, None)
        return carry

    lax.fori_loop(0, k_ref.shape[0] // (tkc * unroll), block, 0)
    o_ref[...] = _softmax_result(acc_ref).astype(o_ref.dtype)


def _mla_flash(q, k, v, batch, seq, tq=1024, tkc=512, unroll=32):
    t = q.shape[0]
    tq, tkc = _tile(seq, tq), _tile(seq, tkc)
    unroll = _tile(seq // tkc, unroll)
    nq = seq // tq
    return pl.pallas_call(
        functools.partial(_mla_flash_kernel, tkc=tkc, unroll=unroll),
        grid=(MLA_HEADS, batch, nq),
        in_specs=[
            pl.BlockSpec((tq, MLA_QK_PAD), lambda g, b, qi: (b * nq + qi, g)),
            pl.BlockSpec((seq, MLA_QK_PAD), lambda g, b, qi: (b, g)),
            pl.BlockSpec((seq, MLA_V), lambda g, b, qi: (b, g)),
        ],
        out_specs=pl.BlockSpec((tq, MLA_V), lambda g, b, qi: (b * nq + qi, g)),
        out_shape=jax.ShapeDtypeStruct((t, MLA_HEADS * MLA_V), BF16),
        scratch_shapes=[pltpu.VMEM((tq, LANES), F32), pltpu.VMEM((tq, MLA_V + LANES), F32)],
        compiler_params=_params("parallel", "parallel", "parallel"),
        name="mla_flash",
    )(q, k, v)


def _key_norm_kernel(k_ref, o_ref):
    @pl.when(pl.program_id(1) == 0)
    def _():
        o_ref[...] = jnp.zeros(o_ref.shape, F32)

    k = k_ref[...].astype(F32)
    sq = k * k
    for g in range(DIFF_HEADS):
        top = jnp.max(jnp.sum(sq[:, g * LANES:(g + 1) * LANES], axis=1, keepdims=True), axis=0, keepdims=True)
        o_ref[g * 8:(g + 1) * 8, :] = jnp.maximum(o_ref[g * 8:(g + 1) * 8, :], top)


def _max_key_norm_sq(zb, kc, batch, seq, tk=2048):
    tk = _tile(seq, tk)
    nk = seq // tk
    assert kc % DIFF_HEADS == 0
    return pl.pallas_call(
        _key_norm_kernel,
        grid=(batch, nk),
        in_specs=[pl.BlockSpec((tk, DIFF_HEADS * LANES), lambda b, i: (b * nk + i, kc // DIFF_HEADS))],
        out_specs=pl.BlockSpec((DIFF_HEADS * 8, LANES), lambda b, i: (b, 0)),
        out_shape=jax.ShapeDtypeStruct((batch * DIFF_HEADS * 8, LANES), F32),
        compiler_params=_params("parallel", "arbitrary"),
        name="key_norm",
    )(zb)


def _diff_flash_kernel(slope_ref, kn_ref, lq1_ref, lk1_ref, lq2_ref, lk2_ref, gsub_ref, q_ref, k_ref, v_ref, o_ref,
                       qs_ref, m_ref, acc_ref, bias_ref, cap_ref, qx_ref, kx_ref, *, tq, tkb, lam_init):
    g, qi = pl.program_id(0), pl.program_id(2)
    slope = slope_ref[g]
    nk = k_ref.shape[0] // tkb
    nch = tkb // tq
    q0 = qi * tq

    @pl.when((qi == 0) & (pl.program_id(1) == 0))
    def _():
        shape = (2 * tq, tq)
        row = lax.broadcasted_iota(jnp.int32, shape, 0)
        row = jnp.where(row >= tq, row - tq, row)
        d = (row - lax.broadcasted_iota(jnp.int32, shape, 1)).astype(F32)
        bias_ref[0] = -slope * d
        bias_ref[1] = -slope * jnp.abs(d)
        bias_ref[2] = slope * d
        whole = jnp.full((1, LANES), slope, F32)
        p1 = whole.astype(BF16).astype(F32)
        p2 = (whole - p1).astype(BF16).astype(F32)
        p3 = (whole - p1 - p2).astype(BF16).astype(F32)
        lane = lax.broadcasted_iota(jnp.int32, (tq, LANES), 1)
        pos = lax.broadcasted_iota(jnp.int32, (tq, LANES), 0)
        even = (lane & 1) == 0
        digit = jnp.where(even, pos >> 4, pos & 15).astype(F32)
        pair = jnp.where(lane >= 6, lane - 6, lane) >> 1
        piece = jnp.where(pair == 0, p1, jnp.where(pair == 1, p2, p3))
        weight = jnp.where(even, 16.0 * piece, piece)
        kx_ref[...] = jnp.where(lane < 6, weight, jnp.where(lane < 12, digit, 0.0)).astype(BF16)
        after = jnp.where(lane < 6, digit, jnp.where(lane < 12, -weight, 0.0)).astype(BF16)
        for half in range(2):
            qx_ref[0, half * tq:(half + 1) * tq, LANES:] = -after
            qx_ref[1, half * tq:(half + 1) * tq, LANES:] = after

    _init_softmax_state(m_ref, acc_ref)
    q = q_ref[...]
    lane = lax.broadcasted_iota(jnp.int32, q.shape, 1)
    qs_ref[:tq] = jnp.where(lane < DIFF_QK_DIM, q, jnp.zeros_like(q))
    qs_ref[tq:] = jnp.where(lane >= DIFF_QK_DIM, q, jnp.zeros_like(q))
    for side in range(2):
        qx_ref[side, :, :LANES] = qs_ref[...]
    qf = q.astype(F32)
    q_norm = jnp.sqrt(jnp.sum(qf * qf, axis=1, keepdims=True))
    cap = q_norm * jnp.sqrt(kn_ref[0:1, :]) * NORM_BOUND_SLACK + NORM_BOUND_SLACK
    cap_ref[:tq] = cap
    cap_ref[tq:] = cap

    unrolled_from = min(nch, (5 * nch + 7) // 8)

    def block(ki, far):
        kb0 = ((q0 // tkb + ki) % nk) * tkb
        best = jnp.max(cap_ref[...] - m_ref[...], axis=0, keepdims=True)
        cidx = lax.broadcasted_iota(jnp.int32, (8, LANES), 0)
        kc0 = kb0 + cidx * tq
        nearest = jnp.maximum(jnp.maximum(q0 - (kc0 + tq) + 1, kc0 - (q0 + tq) + 1), 0)
        live = (cidx < nch) & (best >= slope * nearest.astype(F32) - EXP2_UNDERFLOW)
        n_live = jnp.sum(jnp.where(live, 1, 0), axis=0, keepdims=True)[0, 0]
        first_live = jnp.min(jnp.where(live, cidx, nch), axis=0, keepdims=True)[0, 0]

        def chunk(c):
            k0 = pl.multiple_of(kb0 + c * tq, tq)
            gap = jnp.full((1, LANES), jnp.abs(q0 - k0), jnp.int32).astype(F32)
            if far:
                keys = jnp.concatenate([k_ref[pl.ds(k0, tq), :], kx_ref[...]], axis=1)
                s = _qk(qx_ref[jnp.where(k0 < q0, 0, 1)], keys)
            else:
                which = jnp.where(k0 < q0, 0, jnp.where(k0 == q0, 1, 2))
                s = _qk(qs_ref[...], k_ref[pl.ds(k0, tq), :]) + bias_ref[which]
            _online_softmax_step(s, v_ref[pl.ds(k0, tq), :], m_ref, acc_ref, -slope * gap)

        @pl.when(n_live >= unrolled_from)
        def _():
            for c in range(nch):
                chunk(c)

        @pl.when((n_live > 0) & (n_live < unrolled_from))
        def _():
            def body(c, inner):
                chunk(c)
                return inner

            lax.fori_loop(first_live, first_live + n_live, body, 0)

    block(0, far=False)

    def far_block(ki, carry):
        block(ki, far=True)
        return carry

    lax.fori_loop(1, nk, far_block, 0)

    o = _softmax_result(acc_ref)
    lam = (jnp.exp(jnp.sum(lq1_ref[...] * lk1_ref[...], axis=-1, keepdims=True))
           - jnp.exp(jnp.sum(lq2_ref[...] * lk2_ref[...], axis=-1, keepdims=True)) + lam_init)
    d = o[:tq] - lam * o[tq:]
    o_ref[...] = (_rms(d, gsub_ref[...]) * (1.0 - lam_init)).astype(o_ref.dtype)


def _diff_flash(zb, col0, slopes, lq1, lk1, lq2, lk2, gsub, lam_init, batch, seq, tq=512, tkb=4096):
    t = zb.shape[0]
    tq = _tile(seq, tq)
    tkb = _tile(seq, tkb)
    assert tkb % tq == 0 and tkb // tq <= 8
    nq = seq // tq
    qc, kc, vc = col0
    key_norm = _max_key_norm_sq(zb, kc, batch, seq)
    vec = lambda a: pl.BlockSpec(a.shape, lambda g, b, qi: (0, 0))
    return pl.pallas_call(
        functools.partial(_diff_flash_kernel, tq=tq, tkb=tkb, lam_init=lam_init),
        grid=(DIFF_HEADS, batch, nq),
        in_specs=[
            pl.BlockSpec(memory_space=pltpu.SMEM),
            pl.BlockSpec((8, LANES), lambda g, b, qi: (b * DIFF_HEADS + g, 0)),
            vec(lq1), vec(lk1), vec(lq2), vec(lk2), vec(gsub),
            pl.BlockSpec((tq, LANES), lambda g, b, qi: (b * nq + qi, qc + g)),
            pl.BlockSpec((seq, LANES), lambda g, b, qi: (b, kc + g)),
            pl.BlockSpec((seq, DIFF_V_DIM), lambda g, b, qi: (b, vc + g)),
        ],
        out_specs=pl.BlockSpec((tq, DIFF_V_DIM), lambda g, b, qi: (b * nq + qi, g)),
        out_shape=jax.ShapeDtypeStruct((t, DIFF_HEADS * DIFF_V_DIM), BF16),
        scratch_shapes=[pltpu.VMEM((2 * tq, LANES), BF16), pltpu.VMEM((2 * tq, LANES), F32),
                        pltpu.VMEM((2 * tq, DIFF_V_DIM + LANES), F32), pltpu.VMEM((3, 2 * tq, tq), F32),
                        pltpu.VMEM((2 * tq, LANES), F32), pltpu.VMEM((2, 2 * tq, 2 * LANES), BF16),
                        pltpu.VMEM((tq, LANES), BF16)],
        compiler_params=_params("parallel", "arbitrary", "arbitrary"),
        name="diff_flash",
    )(slopes, key_norm, lq1, lk1, lq2, lk2, gsub, zb, zb, zb)


def _na_kernel(q_ref, kp_ref, kc_ref, kn_ref, vp_ref, vc_ref, vn_ref, bias_ref, o_ref, kb_ref, vb_ref, *, rows):
    j = pl.program_id(2)
    blk = NA_ROWS_PER_STEP * GRID_W
    for n, (kr, vr) in enumerate(((kp_ref, vp_ref), (kc_ref, vc_ref), (kn_ref, vn_ref))):
        kb_ref[n * blk:(n + 1) * blk] = kr[...]
        vb_ref[n * blk:(n + 1) * blk] = vr[...]
    n_groups = rows // NA_GROUP
    gq = NA_GROUP * GRID_W
    for gi in range(NA_ROWS_PER_STEP // NA_GROUP):
        group = j * (NA_ROWS_PER_STEP // NA_GROUP) + gi
        union_start = jnp.clip(group * NA_GROUP - NA_WIN_ROWS // 2, 0, rows - NA_UNION)
        start = pl.multiple_of((union_start - (j - 1) * NA_ROWS_PER_STEP) * GRID_W, GRID_W)
        variant = jnp.where(group == 0, 0, jnp.where(group == n_groups - 1, 2, 1))
        q = q_ref[gi * gq:(gi + 1) * gq, :]
        s = _qk(q, kb_ref[pl.ds(start, NA_UNION * GRID_W), :]) + bias_ref[variant]
        p = jnp.exp2(s - jnp.max(s, axis=1, keepdims=True))
        v = vb_ref[pl.ds(start, NA_UNION * GRID_W), :]
        acc = jnp.dot(p.astype(BF16), jnp.concatenate([v, jnp.ones_like(v)], axis=1), preferred_element_type=F32)
        o_ref[gi * gq:(gi + 1) * gq, :] = (acc[:, :NA_HEAD_DIM] / acc[:, NA_HEAD_DIM:]).astype(o_ref.dtype)


def _na_attention(zb, col0, bias, batch, seq):
    t = zb.shape[0]
    rows = seq // GRID_W
    assert rows % NA_ROWS_PER_STEP == 0
    nj = rows // NA_ROWS_PER_STEP
    blk = NA_ROWS_PER_STEP * GRID_W
    qc, kc, vc = col0

    def spec(c0, dj):
        return pl.BlockSpec((blk, NA_HEAD_DIM),
                            lambda h, b, j: (b * nj + jnp.clip(j + dj, 0, nj - 1), c0 + h))

    return pl.pallas_call(
        functools.partial(_na_kernel, rows=rows),
        grid=(NA_HEADS, batch, nj),
        in_specs=[spec(qc, 0), spec(kc, -1), spec(kc, 0), spec(kc, 1), spec(vc, -1), spec(vc, 0), spec(vc, 1),
                  pl.BlockSpec((None,) + bias.shape[1:], lambda h, b, j: (h, 0, 0, 0))],
        out_specs=pl.BlockSpec((blk, NA_HEAD_DIM), lambda h, b, j: (b * nj + j, h)),
        out_shape=jax.ShapeDtypeStruct((t, NA_HEADS * NA_HEAD_DIM), BF16),
        scratch_shapes=[pltpu.VMEM((3 * blk, NA_HEAD_DIM), BF16), pltpu.VMEM((3 * blk, NA_HEAD_DIM), BF16)],
        compiler_params=_params("parallel", "parallel", "parallel"),
        name="na_attention",
    )(zb, zb, zb, zb, zb, zb, zb, bias)


def _na_bias_table(rpb):
    col = np.arange(GRID_W)
    col_start = np.clip(col - NA_WIN_COLS // 2, 0, GRID_W - NA_WIN_COLS)
    kc = np.arange(GRID_W)
    valid = (kc[None, :] >= col_start[:, None]) & (kc[None, :] < col_start[:, None] + NA_WIN_COLS)
    by_row = jnp.stack([rpb[:, NA_WIN_ROWS - 1 - t:2 * NA_WIN_ROWS - 1 - t] for t in range(NA_WIN_ROWS)], axis=1)
    pad = GRID_W - NA_WIN_COLS
    padded = jnp.pad(by_row, ((0, 0), (0, 0), (0, 0), (pad, pad)))
    tbl = jnp.stack([padded[..., GRID_W - 1 - c:2 * GRID_W - 1 - c] for c in range(GRID_W)], axis=2)
    tbl = jnp.where(valid[None, None, :, None, :], tbl * LOG2E, MASK_VALUE)
    per_row = tbl.reshape(rpb.shape[0], NA_WIN_ROWS, GRID_W, NA_WIN_ROWS * GRID_W).astype(F32)

    extra = NA_UNION - NA_WIN_ROWS

    def place(t, offset):
        return jnp.pad(per_row[:, t], ((0, 0), (0, 0), (offset * GRID_W, (extra - offset) * GRID_W)),
                       constant_values=MASK_VALUE)

    half = NA_WIN_ROWS // 2
    first = jnp.concatenate([place(i, 0) for i in range(NA_GROUP)], axis=1)
    inner = jnp.concatenate([place(half, i) for i in range(NA_GROUP)], axis=1)
    last = jnp.concatenate([place(half + i, extra) for i in range(NA_GROUP)], axis=1)
    return jnp.stack([first, inner, last], axis=1)


def _out_proj_kernel(a_ref, b_ref, c_ref, wa_ref, wb_ref, wc_ref, h_ref, o_ref, *, tn):
    a, b, c = a_ref[...], b_ref[...], c_ref[...]
    for j in range(o_ref.shape[1] // tn):
        cols = slice(j * tn, (j + 1) * tn)
        acc = jnp.dot(a, wa_ref[:, cols], preferred_element_type=F32)
        acc = acc + jnp.dot(b, wb_ref[:, cols], preferred_element_type=F32)
        acc = acc + jnp.dot(c, wc_ref[:, cols], preferred_element_type=F32)
        o_ref[:, cols] = h_ref[:, cols] + acc


def _out_proj(a, b, c, wa, wb, wc, h, tm=512, tn=512):
    t, d = h.shape
    tm, tn = _tile(t, tm), _tile(d, tn)
    rows = lambda x: pl.BlockSpec((tm, x.shape[1]), lambda i: (i, 0))
    whole = lambda w: pl.BlockSpec(w.shape, lambda i: (0, 0))
    return pl.pallas_call(
        functools.partial(_out_proj_kernel, tn=tn),
        grid=(t // tm,),
        in_specs=[rows(a), rows(b), rows(c), whole(wa), whole(wb), whole(wc), rows(h)],
        out_specs=rows(h),
        out_shape=jax.ShapeDtypeStruct((t, d), F32),
        compiler_params=_params("parallel"),
        name="out_proj",
    )(a, b, c, wa, wb, wc, h)


def _ffn_kernel(h_ref, hp_ref, hn_ref, g_ref, wg_ref, wu_ref, cw_ref, cb_ref, wd_ref, o_ref, xn_ref, *,
                tm, tiles_per_seq):
    i, j = pl.program_id(0), pl.program_id(1)

    @pl.when(j == 0)
    def _():
        x = h_ref[...]
        g = g_ref[...]
        xn_ref[:tm] = _rms(x, g).astype(BF16)
        pos = i % tiles_per_seq
        before = jnp.where(pos == 0, 0.0, _rms(hp_ref[...], g)[BF16_ROWS - 1:BF16_ROWS])
        after = jnp.where(pos == tiles_per_seq - 1, 0.0, _rms(hn_ref[...], g)[0:1])
        rid = lax.broadcasted_iota(jnp.int32, (BF16_ROWS, x.shape[1]), 0)
        halo = jnp.where(rid == 0, before, jnp.where(rid == 1, after, 0.0))
        xn_ref[tm:] = halo.astype(BF16)
        o_ref[...] = x

    ga = jnp.dot(xn_ref[...], wg_ref[...], preferred_element_type=F32)
    up = jnp.dot(xn_ref[:tm], wu_ref[...], preferred_element_type=F32)
    gate = ga[:tm]
    rid = lax.broadcasted_iota(jnp.int32, gate.shape, 0)
    g_prev = jnp.where(rid == 0, ga[tm:tm + 1], pltpu.roll(gate, 1, 0))
    g_next = jnp.where(rid == tm - 1, ga[tm + 1:tm + 2], pltpu.roll(gate, tm - 1, 0))
    cw = cw_ref[...]
    conv = g_prev * cw[0:1] + gate * cw[1:2] + g_next * cw[2:3] + cb_ref[...]
    act = (jax.nn.gelu(conv) * up).astype(BF16)
    o_ref[...] += jnp.dot(act, wd_ref[...], preferred_element_type=F32)


def _ffn(h, g, w_gu, conv_w, conv_b, w_down, seq, tm=512, tf=512):
    t, d = h.shape
    f = w_down.shape[0]
    tm, tf = _tile(seq, tm), _tile(f, tf)
    nf = f // tf
    hb = tm // BF16_ROWS
    nhb = t // BF16_ROWS
    return pl.pallas_call(
        functools.partial(_ffn_kernel, tm=tm, tiles_per_seq=seq // tm),
        grid=(t // tm, nf),
        in_specs=[
            pl.BlockSpec((tm, d), lambda i, j: (i, 0)),
            pl.BlockSpec((BF16_ROWS, d), lambda i, j: (jnp.maximum(i * hb - 1, 0), 0)),
            pl.BlockSpec((BF16_ROWS, d), lambda i, j: (jnp.minimum((i + 1) * hb, nhb - 1), 0)),
            pl.BlockSpec((1, d), lambda i, j: (0, 0)),
            pl.BlockSpec((d, tf), lambda i, j: (0, j)),
            pl.BlockSpec((d, tf), lambda i, j: (0, nf + j)),
            pl.BlockSpec((conv_w.shape[0], tf), lambda i, j: (0, j)),
            pl.BlockSpec((1, tf), lambda i, j: (0, j)),
            pl.BlockSpec((tf, d), lambda i, j: (j, 0)),
        ],
        out_specs=pl.BlockSpec((tm, d), lambda i, j: (i, 0)),
        out_shape=jax.ShapeDtypeStruct((t, d), F32),
        scratch_shapes=[pltpu.VMEM((tm + BF16_ROWS, d), BF16)],
        compiler_params=_params("parallel", "arbitrary"),
        name="ffn",
    )(h, h, h, g, w_gu, w_gu, conv_w, conv_b, w_down)


def _ple_kernel(h_ref, p_ref, g_ref, wg_ref, wp_ref, gf_ref, o_ref, *, tn, final_norm):
    h = h_ref[...]
    xn = _rms(h, g_ref[...]).astype(BF16)
    pb = p_ref[...].astype(BF16)
    for j in range(h.shape[1] // tn):
        cols = slice(j * tn, (j + 1) * tn)
        gate = jax.nn.sigmoid(jnp.dot(xn, wg_ref[:, cols], preferred_element_type=F32))
        emb = jnp.dot(pb, wp_ref[:, cols], preferred_element_type=F32)
        o_ref[:, cols] = h[:, cols] + emb * gate
    if final_norm:
        o_ref[...] = _rms(o_ref[...], gf_ref[...])


def _ple(h, p, g, w_plg, w_ple, g_final, final_norm, tm=512, tn=512):
    t, d = h.shape
    tm, tn = _tile(t, tm), _tile(d, tn)
    whole = lambda a: pl.BlockSpec(a.shape, lambda i: (0, 0))
    return pl.pallas_call(
        functools.partial(_ple_kernel, tn=tn, final_norm=final_norm),
        grid=(t // tm,),
        in_specs=[
            pl.BlockSpec((tm, d), lambda i: (i, 0)),
            pl.BlockSpec((tm, p.shape[1]), lambda i: (i, 0)),
            whole(g), whole(w_plg), whole(w_ple), whole(g_final),
        ],
        out_specs=pl.BlockSpec((tm, d), lambda i: (i, 0)),
        out_shape=jax.ShapeDtypeStruct((t, d), F32),
        compiler_params=_params("parallel"),
        name="ple",
    )(h, p, g, w_plg, w_ple, g_final)


def _swap_halves(x):
    half = x.shape[-1] // 2
    return jnp.concatenate([x[..., half:], x[..., :half]], axis=-1)


def _pad_lanes(x):
    return jnp.concatenate([x, jnp.zeros(x.shape[:-1] + (LANES - x.shape[-1],), x.dtype)], axis=-1)


def _prepare_layer(i, w_in, g_qa, w_qb, g_kva, w_kvb, rpb, w_out, w_gu, w_down, w_plg, w_ple):
    d = w_in.shape[0]
    q_rank, kv_rank = g_qa.shape[0], g_kva.shape[0]
    o = q_rank + kv_rank
    kpe = w_in[:, o:o + MLA_ROPE]
    w_f = jnp.concatenate([w_in[:, :o], _pad_lanes(kpe), _pad_lanes(_swap_halves(kpe))], axis=1)
    w_b = w_in[:, o + MLA_ROPE:]
    na_w = NA_HEADS * NA_HEAD_DIM
    diff_w = DIFF_HEADS * 2 * DIFF_QK_DIM
    scale_b = np.ones((1, w_b.shape[1]), np.float32)
    scale_b[:, :na_w] = NA_HEAD_DIM ** -0.5 * LOG2E
    scale_b[:, 3 * na_w:3 * na_w + diff_w] = DIFF_QK_DIM ** -0.5 * LOG2E
    wq = w_qb.reshape(q_rank, MLA_HEADS, MLA_NOPE + MLA_ROPE)
    pe = wq[..., MLA_NOPE:]
    wq = jnp.concatenate([wq[..., :MLA_NOPE].reshape(q_rank, -1), _pad_lanes(pe).reshape(q_rank, -1),
                          _pad_lanes(_swap_halves(pe)).reshape(q_rank, -1)], axis=1)
    mla_w = MLA_HEADS * MLA_V
    return dict(
        w_in=jnp.concatenate([w_b, w_f], axis=1).astype(BF16), n_bf16=w_b.shape[1], scale_b=jnp.asarray(scale_b),
        wq=wq.astype(BF16), wkv=w_kvb.astype(BF16),
        na_bias=_na_bias_table(rpb),
        wo_mla=w_out[:mla_w].astype(BF16), wo_na=w_out[mla_w:mla_w + na_w].astype(BF16),
        wo_diff=w_out[mla_w + na_w:].astype(BF16),
        w_gu=w_gu.astype(BF16), w_down=w_down.astype(BF16),
        w_plg=w_plg.astype(BF16), w_ple=w_ple.astype(BF16),
        lam_init=0.8 - 0.6 * math.exp(-0.3 * i),
    )


def _rope_tables(seq):
    half = MLA_ROPE // 2
    freqs = jnp.power(ROPE_THETA, -jnp.arange(half, dtype=F32) / half)
    ang = jnp.arange(seq, dtype=jnp.int32).astype(F32)[:, None] * freqs[None, :]
    cos, sin = jnp.cos(ang), jnp.sin(ang)
    return _pad_lanes(jnp.concatenate([cos, cos], axis=1)), _pad_lanes(jnp.concatenate([-sin, sin], axis=1))


def _trunk(x, p, layers, vecs, g_final):
    batch, seq, d = x.shape
    t = batch * seq
    h = x.reshape(t, d)
    rope_c, rope_s = _rope_tables(seq)
    slopes = jnp.exp2(-8.0 * jnp.arange(1, DIFF_HEADS + 1, dtype=F32) / DIFF_HEADS) * LOG2E
    nb = NA_HEADS * NA_HEAD_DIM // LANES
    for i, (lw, lv) in enumerate(zip(layers, vecs)):
        zb, zf = _in_proj(h, lv["g_attn"], lw["w_in"], lw["scale_b"], lw["n_bf16"])
        q, k, v = _mla_proj(zf, lv["g_qa"], lv["g_kva"], lw["wq"], lw["wkv"], rope_c, rope_s, seq)
        o_mla = _mla_flash(q, k, v, batch, seq)
        o_na = _na_attention(zb, (0, nb, 2 * nb), lw["na_bias"], batch, seq)
        o_diff = _diff_flash(zb, (3 * nb, 4 * nb, 5 * nb), slopes, lv["lam_q1"], lv["lam_k1"], lv["lam_q2"],
                             lv["lam_k2"], lv["g_subln"], lw["lam_init"], batch, seq)
        h = _out_proj(o_mla, o_na, o_diff, lw["wo_mla"], lw["wo_na"], lw["wo_diff"], h)
        h = _ffn(h, lv["g_ffn"], lw["w_gu"], lv["conv_w"], lv["conv_b"], lw["w_down"], seq)
        h = _ple(h, p[i].reshape(t, -1), lv["g_ple"], lw["w_plg"], lw["w_ple"], g_final.reshape(1, d),
                 final_norm=i == len(layers) - 1)
    return h.reshape(batch, seq, d)


def kernel(x_prompt, x_sample, p_prompt, p_sample, g_attn, w_in, g_qa, w_qb, g_kva, w_kvb, rpb, lam_q1, lam_k1,
           lam_q2, lam_k2, g_subln, w_out, g_ffn, w_gu, conv_w, conv_b, w_down, g_ple, w_plg, w_ple, g_final):
    depth = w_in.shape[0]
    layers = [_prepare_layer(i, w_in[i], g_qa[i], w_qb[i], g_kva[i], w_kvb[i], rpb[i], w_out[i], w_gu[i],
                             w_down[i], w_plg[i], w_ple[i]) for i in range(depth)]
    row = lambda a: a.reshape(1, -1)
    vecs = [dict(g_attn=row(g_attn[i]), g_qa=row(g_qa[i]), g_kva=row(g_kva[i]), lam_q1=row(lam_q1[i]),
                 lam_k1=row(lam_k1[i]), lam_q2=row(lam_q2[i]), lam_k2=row(lam_k2[i]), g_subln=row(g_subln[i]),
                 g_ffn=row(g_ffn[i]), conv_w=conv_w[i], conv_b=row(conv_b[i]), g_ple=row(g_ple[i]))
            for i in range(depth)]
    return (_trunk(x_prompt, p_prompt, layers, vecs, g_final), _trunk(x_sample, p_sample, layers, vecs, g_final))
```

```python
import functools
import math

import jax
import jax.numpy as jnp
import numpy as np
from jax import lax
from jax.experimental import pallas as pl
from jax.experimental.pallas import tpu as pltpu

F32 = jnp.float32
BF16 = jnp.bfloat16

GRID_W = 64
MLA_HEADS = 8
MLA_NOPE = 128
MLA_ROPE = 64
MLA_V = 128
MLA_QK_PAD = 256
ROPE_THETA = 10000.0
NA_HEADS = 4
NA_HEAD_DIM = 128
NA_WIN_ROWS = 8
NA_WIN_COLS = 16
NA_ROWS_PER_STEP = 32
NA_GROUP = 4
NA_UNION = NA_WIN_ROWS + NA_GROUP - 1
DIFF_HEADS = 4
DIFF_QK_DIM = 64
DIFF_V_DIM = 128
RMS_EPS = 1e-6
LOG2E = 1.4426950408889634
MASK_VALUE = -1e30
EXP2_UNDERFLOW = 160.0
NORM_BOUND_SLACK = 1.001
LANES = 128
BF16_ROWS = 16
VMEM_LIMIT = 56 * 1024 * 1024


def _tile(n, pref):
    t = min(n, pref)
    assert n % t == 0, (n, pref)
    return t


def _rms(x, g):
    ms = jnp.mean(x * x, axis=-1, keepdims=True)
    return x * lax.rsqrt(ms + RMS_EPS) * g


def _lane_tile(x, n):
    return jnp.concatenate([x] * n, axis=1)


def _params(*sem):
    return pltpu.CompilerParams(dimension_semantics=sem, vmem_limit_bytes=VMEM_LIMIT)


def _in_proj_kernel(x_ref, g_ref, w_ref, cs_ref, zb_ref, zf_ref, xn_ref, *, nb):
    j = pl.program_id(1)

    @pl.when(j == 0)
    def _():
        xn_ref[...] = _rms(x_ref[...], g_ref[...]).astype(BF16)

    acc = jnp.dot(xn_ref[...], w_ref[...], preferred_element_type=F32)

    @pl.when(j < nb)
    def _():
        zb_ref[...] = (acc * cs_ref[...]).astype(BF16)

    @pl.when(j >= nb)
    def _():
        zf_ref[...] = acc


def _in_proj(x, g, w, col_scale, n_bf16, tm=1024, tn=1024):
    t, d = x.shape
    n = w.shape[1]
    tm, tn = _tile(t, tm), _tile(math.gcd(n_bf16, n - n_bf16), tn)
    nb, nf = n_bf16 // tn, (n - n_bf16) // tn
    return pl.pallas_call(
        functools.partial(_in_proj_kernel, nb=nb),
        grid=(t // tm, nb + nf),
        in_specs=[
            pl.BlockSpec((tm, d), lambda i, j: (i, 0)),
            pl.BlockSpec((1, d), lambda i, j: (0, 0)),
            pl.BlockSpec((d, tn), lambda i, j: (0, j)),
            pl.BlockSpec((1, tn), lambda i, j: (0, jnp.minimum(j, nb - 1))),
        ],
        out_specs=[pl.BlockSpec((tm, tn), lambda i, j: (i, jnp.minimum(j, nb - 1))),
                   pl.BlockSpec((tm, tn), lambda i, j: (i, jnp.maximum(j - nb, 0)))],
        out_shape=[jax.ShapeDtypeStruct((t, n_bf16), BF16), jax.ShapeDtypeStruct((t, n - n_bf16), F32)],
        scratch_shapes=[pltpu.VMEM((tm, d), BF16)],
        compiler_params=_params("parallel", "arbitrary"),
        name="in_proj",
    )(x, g, w, col_scale)


def _mla_proj_kernel(zf_ref, gq_ref, gkv_ref, wq_ref, wkv_ref, c_ref, s_ref, q_ref, k_ref, v_ref, *,
                     q_rank, kv_rank, q_scale):
    zf = zf_ref[...]
    cos, sin = c_ref[...], s_ref[...]
    qn = _rms(zf[:, :q_rank], gq_ref[...]).astype(BF16)
    qa = jnp.dot(qn, wq_ref[...], preferred_element_type=F32)
    kvn = _rms(zf[:, q_rank:q_rank + kv_rank], gkv_ref[...]).astype(BF16)
    kva = jnp.dot(kvn, wkv_ref[...], preferred_element_type=F32)
    o = q_rank + kv_rank
    kpe = (zf[:, o:o + LANES] * cos + zf[:, o + LANES:o + 2 * LANES] * sin).astype(BF16)
    hp = MLA_HEADS * LANES
    for h in range(MLA_HEADS):
        lo = h * LANES
        rope = qa[:, hp + lo:hp + lo + LANES] * cos + qa[:, 2 * hp + lo:2 * hp + lo + LANES] * sin
        q_ref[:, 2 * lo:2 * lo + LANES] = (qa[:, lo:lo + LANES] * q_scale).astype(BF16)
        q_ref[:, 2 * lo + LANES:2 * lo + 2 * LANES] = (rope * q_scale).astype(BF16)
        k_ref[:, 2 * lo:2 * lo + LANES] = kva[:, 2 * lo:2 * lo + LANES].astype(BF16)
        k_ref[:, 2 * lo + LANES:2 * lo + 2 * LANES] = kpe
        v_ref[:, lo:lo + LANES] = kva[:, 2 * lo + LANES:2 * lo + 2 * LANES].astype(BF16)


def _mla_proj(zf, gq, gkv, wq, wkv, rope_c, rope_s, seq, tm=512):
    t, zw = zf.shape
    tm = _tile(seq, tm)
    nl = seq // tm
    q_rank, kv_rank = gq.shape[1], gkv.shape[1]
    kern = functools.partial(_mla_proj_kernel, q_rank=q_rank, kv_rank=kv_rank,
                             q_scale=(MLA_NOPE + MLA_ROPE) ** -0.5 * LOG2E)
    qk_w = MLA_HEADS * MLA_QK_PAD
    return pl.pallas_call(
        kern,
        grid=(t // tm,),
        in_specs=[
            pl.BlockSpec((tm, zw), lambda i: (i, 0)),
            pl.BlockSpec((1, q_rank), lambda i: (0, 0)),
            pl.BlockSpec((1, kv_rank), lambda i: (0, 0)),
            pl.BlockSpec(wq.shape, lambda i: (0, 0)),
            pl.BlockSpec(wkv.shape, lambda i: (0, 0)),
            pl.BlockSpec((tm, LANES), lambda i: (i % nl, 0)),
            pl.BlockSpec((tm, LANES), lambda i: (i % nl, 0)),
        ],
        out_specs=[
            pl.BlockSpec((tm, qk_w), lambda i: (i, 0)),
            pl.BlockSpec((tm, qk_w), lambda i: (i, 0)),
            pl.BlockSpec((tm, MLA_HEADS * MLA_V), lambda i: (i, 0)),
        ],
        out_shape=[
            jax.ShapeDtypeStruct((t, qk_w), BF16),
            jax.ShapeDtypeStruct((t, qk_w), BF16),
            jax.ShapeDtypeStruct((t, MLA_HEADS * MLA_V), BF16),
        ],
        compiler_params=_params("parallel"),
        name="mla_proj",
    )(zf, gq, gkv, wq, wkv, rope_c, rope_s)


def _online_softmax_step(s, v, m_ref, acc_ref, shift):
    m_prev = m_ref[...]
    m_cur = jnp.max(s, axis=1, keepdims=True)
    if shift is not None:
        m_cur = m_cur + shift
    m_next = jnp.maximum(m_prev, m_cur)
    alpha = jnp.exp2(m_prev - m_next)
    m_sub = m_next if shift is None else m_next - shift
    p = jnp.exp2(s - _lane_tile(m_sub, s.shape[1] // LANES))
    v1 = jnp.concatenate([v, jnp.ones_like(v)], axis=1)
    acc_ref[...] = _lane_tile(alpha, 2) * acc_ref[...] + jnp.dot(p.astype(BF16), v1, preferred_element_type=F32)
    m_ref[...] = m_next


def _init_softmax_state(m_ref, acc_ref):
    m_ref[...] = jnp.full(m_ref.shape, MASK_VALUE, F32)
    acc_ref[...] = jnp.zeros(acc_ref.shape, F32)


def _softmax_result(acc_ref):
    acc = acc_ref[...]
    return acc[:, :LANES] / acc[:, LANES:]


def _qk(q, k):
    return lax.dot_general(q, k, (((1,), (1,)), ((), ())), preferred_element_type=F32)


def _mla_flash_kernel(q_ref, k_ref, v_ref, o_ref, m_ref, acc_ref, *, tkc, unroll):
    _init_softmax_state(m_ref, acc_ref)
    q = q_ref[...]

    def block(bi, carry):
        offs = [pl.multiple_of((bi * unroll + c) * tkc, tkc) for c in range(unroll)]
        s = _qk(q, k_ref[pl.ds(offs[0], tkc), :])
        for c in range(unroll):
            s_next = _qk(q, k_ref[pl.ds(offs[c + 1], tkc), :]) if c + 1 < unroll else None
            _online_softmax_step(s, v_ref[pl.ds(offs[c], tkc), :], m_ref, acc_ref, None)
            s = s_next
        return carry

    lax.fori_loop(0, k_ref.shape[0] // (tkc * unroll), block, 0)
    o_ref[...] = _softmax_result(acc_ref).astype(o_ref.dtype)


def _mla_flash(q, k, v, batch, seq, tq=1024, tkc=512, unroll=32):
    t = q.shape[0]
    tq, tkc = _tile(seq, tq), _tile(seq, tkc)
    unroll = _tile(seq // tkc, unroll)
    nq = seq // tq
    return pl.pallas_call(
        functools.partial(_mla_flash_kernel, tkc=tkc, unroll=unroll),
        grid=(MLA_HEADS, batch, nq),
        in_specs=[
            pl.BlockSpec((tq, MLA_QK_PAD), lambda g, b, qi: (b * nq + qi, g)),
            pl.BlockSpec((seq, MLA_QK_PAD), lambda g, b, qi: (b, g)),
            pl.BlockSpec((seq, MLA_V), lambda g, b, qi: (b, g)),
        ],
        out_specs=pl.BlockSpec((tq, MLA_V), lambda g, b, qi: (b * nq + qi, g)),
        out_shape=jax.ShapeDtypeStruct((t, MLA_HEADS * MLA_V), BF16),
        scratch_shapes=[pltpu.VMEM((tq, LANES), F32), pltpu.VMEM((tq, MLA_V + LANES), F32)],
        compiler_params=_params("parallel", "parallel", "parallel"),
        name="mla_flash",
    )(q, k, v)


def _key_norm_kernel(k_ref, o_ref):
    @pl.when(pl.program_id(1) == 0)
    def _():
        o_ref[...] = jnp.zeros(o_ref.shape, F32)

    k = k_ref[...].astype(F32)
    sq = k * k
    for g in range(DIFF_HEADS):
        top = jnp.max(jnp.sum(sq[:, g * LANES:(g + 1) * LANES], axis=1, keepdims=True), axis=0, keepdims=True)
        o_ref[g * 8:(g + 1) * 8, :] = jnp.maximum(o_ref[g * 8:(g + 1) * 8, :], top)


def _max_key_norm_sq(zb, kc, batch, seq, tk=2048):
    tk = _tile(seq, tk)
    nk = seq // tk
    assert kc % DIFF_HEADS == 0
    return pl.pallas_call(
        _key_norm_kernel,
        grid=(batch, nk),
        in_specs=[pl.BlockSpec((tk, DIFF_HEADS * LANES), lambda b, i: (b * nk + i, kc // DIFF_HEADS))],
        out_specs=pl.BlockSpec((DIFF_HEADS * 8, LANES), lambda b, i: (b, 0)),
        out_shape=jax.ShapeDtypeStruct((batch * DIFF_HEADS * 8, LANES), F32),
        compiler_params=_params("parallel", "arbitrary"),
        name="key_norm",
    )(zb)


def _diff_flash_kernel(slope_ref, kn_ref, lq1_ref, lk1_ref, lq2_ref, lk2_ref, gsub_ref, q_ref, k_ref, v_ref, o_ref,
                       qs_ref, m_ref, acc_ref, bias_ref, cap_ref, qx_ref, kx_ref, *, tq, tkb, lam_init):
    g, qi = pl.program_id(0), pl.program_id(2)
    slope = slope_ref[g]
    nk = k_ref.shape[0] // tkb
    nch = tkb // tq
    q0 = qi * tq

    @pl.when((qi == 0) & (pl.program_id(1) == 0))
    def _():
        shape = (2 * tq, tq)
        row = lax.broadcasted_iota(jnp.int32, shape, 0)
        row = jnp.where(row >= tq, row - tq, row)
        d = (row - lax.broadcasted_iota(jnp.int32, shape, 1)).astype(F32)
        bias_ref[0] = -slope * d
        bias_ref[1] = -slope * jnp.abs(d)
        bias_ref[2] = slope * d
        whole = jnp.full((1, LANES), slope, F32)
        p1 = whole.astype(BF16).astype(F32)
        p2 = (whole - p1).astype(BF16).astype(F32)
        p3 = (whole - p1 - p2).astype(BF16).astype(F32)
        lane = lax.broadcasted_iota(jnp.int32, (tq, LANES), 1)
        pos = lax.broadcasted_iota(jnp.int32, (tq, LANES), 0)
        even = (lane & 1) == 0
        digit = jnp.where(even, pos >> 4, pos & 15).astype(F32)
        pair = jnp.where(lane >= 6, lane - 6, lane) >> 1
        piece = jnp.where(pair == 0, p1, jnp.where(pair == 1, p2, p3))
        weight = jnp.where(even, 16.0 * piece, piece)
        kx_ref[...] = jnp.where(lane < 6, weight, jnp.where(lane < 12, digit, 0.0)).astype(BF16)
        after = jnp.where(lane < 6, digit, jnp.where(lane < 12, -weight, 0.0)).astype(BF16)
        for half in range(2):
            qx_ref[0, half * tq:(half + 1) * tq, LANES:] = -after
            qx_ref[1, half * tq:(half + 1) * tq, LANES:] = after

    _init_softmax_state(m_ref, acc_ref)
    q = q_ref[...]
    lane = lax.broadcasted_iota(jnp.int32, q.shape, 1)
    qs_ref[:tq] = jnp.where(lane < DIFF_QK_DIM, q, jnp.zeros_like(q))
    qs_ref[tq:] = jnp.where(lane >= DIFF_QK_DIM, q, jnp.zeros_like(q))
    for side in range(2):
        qx_ref[side, :, :LANES] = qs_ref[...]
    qf = q.astype(F32)
    q_norm = jnp.sqrt(jnp.sum(qf * qf, axis=1, keepdims=True))
    cap = q_norm * jnp.sqrt(kn_ref[0:1, :]) * NORM_BOUND_SLACK + NORM_BOUND_SLACK
    cap_ref[:tq] = cap
    cap_ref[tq:] = cap

    unrolled_from = min(nch, (5 * nch + 7) // 8)

    def block(ki, far):
        kb0 = ((q0 // tkb + ki) % nk) * tkb
        best = jnp.max(cap_ref[...] - m_ref[...], axis=0, keepdims=True)
        cidx = lax.broadcasted_iota(jnp.int32, (8, LANES), 0)
        kc0 = kb0 + cidx * tq
        nearest = jnp.maximum(jnp.maximum(q0 - (kc0 + tq) + 1, kc0 - (q0 + tq) + 1), 0)
        live = (cidx < nch) & (best >= slope * nearest.astype(F32) - EXP2_UNDERFLOW)
        n_live = jnp.sum(jnp.where(live, 1, 0), axis=0, keepdims=True)[0, 0]
        first_live = jnp.min(jnp.where(live, cidx, nch), axis=0, keepdims=True)[0, 0]

        def chunk(c):
            k0 = pl.multiple_of(kb0 + c * tq, tq)
            gap = jnp.full((1, LANES), jnp.abs(q0 - k0), jnp.int32).astype(F32)
            if far:
                keys = jnp.concatenate([k_ref[pl.ds(k0, tq), :], kx_ref[...]], axis=1)
                s = _qk(qx_ref[jnp.where(k0 < q0, 0, 1)], keys)
            else:
                which = jnp.where(k0 < q0, 0, jnp.where(k0 == q0, 1, 2))
                s = _qk(qs_ref[...], k_ref[pl.ds(k0, tq), :]) + bias_ref[which]
            _online_softmax_step(s, v_ref[pl.ds(k0, tq), :], m_ref, acc_ref, -slope * gap)

        @pl.when(n_live >= unrolled_from)
        def _():
            for c in range(nch):
                chunk(c)

        @pl.when((n_live > 0) & (n_live < unrolled_from))
        def _():
            def body(c, inner):
                chunk(c)
                return inner

            lax.fori_loop(first_live, first_live + n_live, body, 0)

    block(0, far=False)

    def far_block(ki, carry):
        block(ki, far=True)
        return carry

    lax.fori_loop(1, nk, far_block, 0)

    o = _softmax_result(acc_ref)
    lam = (jnp.exp(jnp.sum(lq1_ref[...] * lk1_ref[...], axis=-1, keepdims=True))
           - jnp.exp(jnp.sum(lq2_ref[...] * lk2_ref[...], axis=-1, keepdims=True)) + lam_init)
    d = o[:tq] - lam * o[tq:]
    o_ref[...] = (_rms(d, gsub_ref[...]) * (1.0 - lam_init)).astype(o_ref.dtype)


def _diff_flash(zb, col0, slopes, lq1, lk1, lq2, lk2, gsub, lam_init, batch, seq, tq=512, tkb=4096):
    t = zb.shape[0]
    tq = _tile(seq, tq)
    tkb = _tile(seq, tkb)
    assert tkb % tq == 0 and tkb // tq <= 8
    nq = seq // tq
    qc, kc, vc = col0
    key_norm = _max_key_norm_sq(zb, kc, batch, seq)
    vec = lambda a: pl.BlockSpec(a.shape, lambda g, b, qi: (0, 0))
    return pl.pallas_call(
        functools.partial(_diff_flash_kernel, tq=tq, tkb=tkb, lam_init=lam_init),
        grid=(DIFF_HEADS, batch, nq),
        in_specs=[
            pl.BlockSpec(memory_space=pltpu.SMEM),
            pl.BlockSpec((8, LANES), lambda g, b, qi: (b * DIFF_HEADS + g, 0)),
            vec(lq1), vec(lk1), vec(lq2), vec(lk2), vec(gsub),
            pl.BlockSpec((tq, LANES), lambda g, b, qi: (b * nq + qi, qc + g)),
            pl.BlockSpec((seq, LANES), lambda g, b, qi: (b, kc + g)),
            pl.BlockSpec((seq, DIFF_V_DIM), lambda g, b, qi: (b, vc + g)),
        ],
        out_specs=pl.BlockSpec((tq, DIFF_V_DIM), lambda g, b, qi: (b * nq + qi, g)),
        out_shape=jax.ShapeDtypeStruct((t, DIFF_HEADS * DIFF_V_DIM), BF16),
        scratch_shapes=[pltpu.VMEM((2 * tq, LANES), BF16), pltpu.VMEM((2 * tq, LANES), F32),
                        pltpu.VMEM((2 * tq, DIFF_V_DIM + LANES), F32), pltpu.VMEM((3, 2 * tq, tq), F32),
                        pltpu.VMEM((2 * tq, LANES), F32), pltpu.VMEM((2, 2 * tq, 2 * LANES), BF16),
                        pltpu.VMEM((tq, LANES), BF16)],
        compiler_params=_params("parallel", "arbitrary", "arbitrary"),
        name="diff_flash",
    )(slopes, key_norm, lq1, lk1, lq2, lk2, gsub, zb, zb, zb)


def _na_kernel(q_ref, kp_ref, kc_ref, kn_ref, vp_ref, vc_ref, vn_ref, bias_ref, o_ref, kb_ref, vb_ref, *, rows):
    j = pl.program_id(2)
    blk = NA_ROWS_PER_STEP * GRID_W
    for n, (kr, vr) in enumerate(((kp_ref, vp_ref), (kc_ref, vc_ref), (kn_ref, vn_ref))):
        kb_ref[n * blk:(n + 1) * blk] = kr[...]
        vb_ref[n * blk:(n + 1) * blk] = vr[...]
    n_groups = rows // NA_GROUP
    gq = NA_GROUP * GRID_W
    for gi in range(NA_ROWS_PER_STEP // NA_GROUP):
        group = j * (NA_ROWS_PER_STEP // NA_GROUP) + gi
        union_start = jnp.clip(group * NA_GROUP - NA_WIN_ROWS // 2, 0, rows - NA_UNION)
        start = pl.multiple_of((union_start - (j - 1) * NA_ROWS_PER_STEP) * GRID_W, GRID_W)
        variant = jnp.where(group == 0, 0, jnp.where(group == n_groups - 1, 2, 1))
        q = q_ref[gi * gq:(gi + 1) * gq, :]
        s = _qk(q, kb_ref[pl.ds(start, NA_UNION * GRID_W), :]) + bias_ref[variant]
        p = jnp.exp2(s - jnp.max(s, axis=1, keepdims=True))
        v = vb_ref[pl.ds(start, NA_UNION * GRID_W), :]
        acc = jnp.dot(p.astype(BF16), jnp.concatenate([v, jnp.ones_like(v)], axis=1), preferred_element_type=F32)
        o_ref[gi * gq:(gi + 1) * gq, :] = (acc[:, :NA_HEAD_DIM] / acc[:, NA_HEAD_DIM:]).astype(o_ref.dtype)


def _na_attention(zb, col0, bias, batch, seq):
    t = zb.shape[0]
    rows = seq // GRID_W
    assert rows % NA_ROWS_PER_STEP == 0
    nj = rows // NA_ROWS_PER_STEP
    blk = NA_ROWS_PER_STEP * GRID_W
    qc, kc, vc = col0

    def spec(c0, dj):
        return pl.BlockSpec((blk, NA_HEAD_DIM),
                            lambda h, b, j: (b * nj + jnp.clip(j + dj, 0, nj - 1), c0 + h))

    return pl.pallas_call(
        functools.partial(_na_kernel, rows=rows),
        grid=(NA_HEADS, batch, nj),
        in_specs=[spec(qc, 0), spec(kc, -1), spec(kc, 0), spec(kc, 1), spec(vc, -1), spec(vc, 0), spec(vc, 1),
                  pl.BlockSpec((None,) + bias.shape[1:], lambda h, b, j: (h, 0, 0, 0))],
        out_specs=pl.BlockSpec((blk, NA_HEAD_DIM), lambda h, b, j: (b * nj + j, h)),
        out_shape=jax.ShapeDtypeStruct((t, NA_HEADS * NA_HEAD_DIM), BF16),
        scratch_shapes=[pltpu.VMEM((3 * blk, NA_HEAD_DIM), BF16), pltpu.VMEM((3 * blk, NA_HEAD_DIM), BF16)],
        compiler_params=_params("parallel", "parallel", "parallel"),
        name="na_attention",
    )(zb, zb, zb, zb, zb, zb, zb, bias)


def _na_bias_table(rpb):
    col = np.arange(GRID_W)
    col_start = np.clip(col - NA_WIN_COLS // 2, 0, GRID_W - NA_WIN_COLS)
    kc = np.arange(GRID_W)
    valid = (kc[None, :] >= col_start[:, None]) & (kc[None, :] < col_start[:, None] + NA_WIN_COLS)
    by_row = jnp.stack([rpb[:, NA_WIN_ROWS - 1 - t:2 * NA_WIN_ROWS - 1 - t] for t in range(NA_WIN_ROWS)], axis=1)
    pad = GRID_W - NA_WIN_COLS
    padded = jnp.pad(by_row, ((0, 0), (0, 0), (0, 0), (pad, pad)))
    tbl = jnp.stack([padded[..., GRID_W - 1 - c:2 * GRID_W - 1 - c] for c in range(GRID_W)], axis=2)
    tbl = jnp.where(valid[None, None, :, None, :], tbl * LOG2E, MASK_VALUE)
    per_row = tbl.reshape(rpb.shape[0], NA_WIN_ROWS, GRID_W, NA_WIN_ROWS * GRID_W).astype(F32)

    extra = NA_UNION - NA_WIN_ROWS

    def place(t, offset):
        return jnp.pad(per_row[:, t], ((0, 0), (0, 0), (offset * GRID_W, (extra - offset) * GRID_W)),
                       constant_values=MASK_VALUE)

    half = NA_WIN_ROWS // 2
    first = jnp.concatenate([place(i, 0) for i in range(NA_GROUP)], axis=1)
    inner = jnp.concatenate([place(half, i) for i in range(NA_GROUP)], axis=1)
    last = jnp.concatenate([place(half + i, extra) for i in range(NA_GROUP)], axis=1)
    return jnp.stack([first, inner, last], axis=1)


def _out_proj_kernel(a_ref, b_ref, c_ref, wa_ref, wb_ref, wc_ref, h_ref, o_ref, *, tn):
    a, b, c = a_ref[...], b_ref[...], c_ref[...]
    for j in range(o_ref.shape[1] // tn):
        cols = slice(j * tn, (j + 1) * tn)
        acc = jnp.dot(a, wa_ref[:, cols], preferred_element_type=F32)
        acc = acc + jnp.dot(b, wb_ref[:, cols], preferred_element_type=F32)
        acc = acc + jnp.dot(c, wc_ref[:, cols], preferred_element_type=F32)
        o_ref[:, cols] = h_ref[:, cols] + acc


def _out_proj(a, b, c, wa, wb, wc, h, tm=512, tn=512):
    t, d = h.shape
    tm, tn = _tile(t, tm), _tile(d, tn)
    rows = lambda x: pl.BlockSpec((tm, x.shape[1]), lambda i: (i, 0))
    whole = lambda w: pl.BlockSpec(w.shape, lambda i: (0, 0))
    return pl.pallas_call(
        functools.partial(_out_proj_kernel, tn=tn),
        grid=(t // tm,),
        in_specs=[rows(a), rows(b), rows(c), whole(wa), whole(wb), whole(wc), rows(h)],
        out_specs=rows(h),
        out_shape=jax.ShapeDtypeStruct((t, d), F32),
        compiler_params=_params("parallel"),
        name="out_proj",
    )(a, b, c, wa, wb, wc, h)


def _ffn_kernel(h_ref, hp_ref, hn_ref, g_ref, wg_ref, wu_ref, cw_ref, cb_ref, wd_ref, o_ref, xn_ref, *,
                tm, tiles_per_seq):
    i, j = pl.program_id(0), pl.program_id(1)

    @pl.when(j == 0)
    def _():
        x = h_ref[...]
        g = g_ref[...]
        xn_ref[:tm] = _rms(x, g).astype(BF16)
        pos = i % tiles_per_seq
        before = jnp.where(pos == 0, 0.0, _rms(hp_ref[...], g)[BF16_ROWS - 1:BF16_ROWS])
        after = jnp.where(pos == tiles_per_seq - 1, 0.0, _rms(hn_ref[...], g)[0:1])
        rid = lax.broadcasted_iota(jnp.int32, (BF16_ROWS, x.shape[1]), 0)
        halo = jnp.where(rid == 0, before, jnp.where(rid == 1, after, 0.0))
        xn_ref[tm:] = halo.astype(BF16)
        o_ref[...] = x

    ga = jnp.dot(xn_ref[...], wg_ref[...], preferred_element_type=F32)
    up = jnp.dot(xn_ref[:tm], wu_ref[...], preferred_element_type=F32)
    gate = ga[:tm]
    rid = lax.broadcasted_iota(jnp.int32, gate.shape, 0)
    g_prev = jnp.where(rid == 0, ga[tm:tm + 1], pltpu.roll(gate, 1, 0))
    g_next = jnp.where(rid == tm - 1, ga[tm + 1:tm + 2], pltpu.roll(gate, tm - 1, 0))
    cw = cw_ref[...]
    conv = g_prev * cw[0:1] + gate * cw[1:2] + g_next * cw[2:3] + cb_ref[...]
    act = (jax.nn.gelu(conv) * up).astype(BF16)
    o_ref[...] += jnp.dot(act, wd_ref[...], preferred_element_type=F32)


def _ffn(h, g, w_gu, conv_w, conv_b, w_down, seq, tm=512, tf=512):
    t, d = h.shape
    f = w_down.shape[0]
    tm, tf = _tile(seq, tm), _tile(f, tf)
    nf = f // tf
    hb = tm // BF16_ROWS
    nhb = t // BF16_ROWS
    return pl.pallas_call(
        functools.partial(_ffn_kernel, tm=tm, tiles_per_seq=seq // tm),
        grid=(t // tm, nf),
        in_specs=[
            pl.BlockSpec((tm, d), lambda i, j: (i, 0)),
            pl.BlockSpec((BF16_ROWS, d), lambda i, j: (jnp.maximum(i * hb - 1, 0), 0)),
            pl.BlockSpec((BF16_ROWS, d), lambda i, j: (jnp.minimum((i + 1) * hb, nhb - 1), 0)),
            pl.BlockSpec((1, d), lambda i, j: (0, 0)),
            pl.BlockSpec((d, tf), lambda i, j: (0, j)),
            pl.BlockSpec((d, tf), lambda i, j: (0, nf + j)),
            pl.BlockSpec((conv_w.shape[0], tf), lambda i, j: (0, j)),
            pl.BlockSpec((1, tf), lambda i, j: (0, j)),
            pl.BlockSpec((tf, d), lambda i, j: (j, 0)),
        ],
        out_specs=pl.BlockSpec((tm, d), lambda i, j: (i, 0)),
        out_shape=jax.ShapeDtypeStruct((t, d), F32),
        scratch_shapes=[pltpu.VMEM((tm + BF16_ROWS, d), BF16)],
        compiler_params=_params("parallel", "arbitrary"),
        name="ffn",
    )(h, h, h, g, w_gu, w_gu, conv_w, conv_b, w_down)


def _ple_kernel(h_ref, p_ref, g_ref, wg_ref, wp_ref, gf_ref, o_ref, *, tn, final_norm):
    h = h_ref[...]
    xn = _rms(h, g_ref[...]).astype(BF16)
    pb = p_ref[...].astype(BF16)
    for j in range(h.shape[1] // tn):
        cols = slice(j * tn, (j + 1) * tn)
        gate = jax.nn.sigmoid(jnp.dot(xn, wg_ref[:, cols], preferred_element_type=F32))
        emb = jnp.dot(pb, wp_ref[:, cols], preferred_element_type=F32)
        o_ref[:, cols] = h[:, cols] + emb * gate
    if final_norm:
        o_ref[...] = _rms(o_ref[...], gf_ref[...])


def _ple(h, p, g, w_plg, w_ple, g_final, final_norm, tm=512, tn=512):
    t, d = h.shape
    tm, tn = _tile(t, tm), _tile(d, tn)
    whole = lambda a: pl.BlockSpec(a.shape, lambda i: (0, 0))
    return pl.pallas_call(
        functools.partial(_ple_kernel, tn=tn, final_norm=final_norm),
        grid=(t // tm,),
        in_specs=[
            pl.BlockSpec((tm, d), lambda i: (i, 0)),
            pl.BlockSpec((tm, p.shape[1]), lambda i: (i, 0)),
            whole(g), whole(w_plg), whole(w_ple), whole(g_final),
        ],
        out_specs=pl.BlockSpec((tm, d), lambda i: (i, 0)),
        out_shape=jax.ShapeDtypeStruct((t, d), F32),
        compiler_params=_params("parallel"),
        name="ple",
    )(h, p, g, w_plg, w_ple, g_final)


def _swap_halves(x):
    half = x.shape[-1] // 2
    return jnp.concatenate([x[..., half:], x[..., :half]], axis=-1)


def _pad_lanes(x):
    return jnp.concatenate([x, jnp.zeros(x.shape[:-1] + (LANES - x.shape[-1],), x.dtype)], axis=-1)


def _prepare_layer(i, w_in, g_qa, w_qb, g_kva, w_kvb, rpb, w_out, w_gu, w_down, w_plg, w_ple):
    d = w_in.shape[0]
    q_rank, kv_rank = g_qa.shape[0], g_kva.shape[0]
    o = q_rank + kv_rank
    kpe = w_in[:, o:o + MLA_ROPE]
    w_f = jnp.concatenate([w_in[:, :o], _pad_lanes(kpe), _pad_lanes(_swap_halves(kpe))], axis=1)
    w_b = w_in[:, o + MLA_ROPE:]
    na_w = NA_HEADS * NA_HEAD_DIM
    diff_w = DIFF_HEADS * 2 * DIFF_QK_DIM
    scale_b = np.ones((1, w_b.shape[1]), np.float32)
    scale_b[:, :na_w] = NA_HEAD_DIM ** -0.5 * LOG2E
    scale_b[:, 3 * na_w:3 * na_w + diff_w] = DIFF_QK_DIM ** -0.5 * LOG2E
    wq = w_qb.reshape(q_rank, MLA_HEADS, MLA_NOPE + MLA_ROPE)
    pe = wq[..., MLA_NOPE:]
    wq = jnp.concatenate([wq[..., :MLA_NOPE].reshape(q_rank, -1), _pad_lanes(pe).reshape(q_rank, -1),
                          _pad_lanes(_swap_halves(pe)).reshape(q_rank, -1)], axis=1)
    mla_w = MLA_HEADS * MLA_V
    return dict(
        w_in=jnp.concatenate([w_b, w_f], axis=1).astype(BF16), n_bf16=w_b.shape[1], scale_b=jnp.asarray(scale_b),
        wq=wq.astype(BF16), wkv=w_kvb.astype(BF16),
        na_bias=_na_bias_table(rpb),
        wo_mla=w_out[:mla_w].astype(BF16), wo_na=w_out[mla_w:mla_w + na_w].astype(BF16),
        wo_diff=w_out[mla_w + na_w:].astype(BF16),
        w_gu=w_gu.astype(BF16), w_down=w_down.astype(BF16),
        w_plg=w_plg.astype(BF16), w_ple=w_ple.astype(BF16),
        lam_init=0.8 - 0.6 * math.exp(-0.3 * i),
    )


def _rope_tables(seq):
    half = MLA_ROPE // 2
    freqs = jnp.power(ROPE_THETA, -jnp.arange(half, dtype=F32) / half)
    ang = jnp.arange(seq, dtype=jnp.int32).astype(F32)[:, None] * freqs[None, :]
    cos, sin = jnp.cos(ang), jnp.sin(ang)
    return _pad_lanes(jnp.concatenate([cos, cos], axis=1)), _pad_lanes(jnp.concatenate([-sin, sin], axis=1))


def _trunk(x, p, layers, vecs, g_final):
    batch, seq, d = x.shape
    t = batch * seq
    h = x.reshape(t, d)
    rope_c, rope_s = _rope_tables(seq)
    slopes = jnp.exp2(-8.0 * jnp.arange(1, DIFF_HEADS + 1, dtype=F32) / DIFF_HEADS) * LOG2E
    nb = NA_HEADS * NA_HEAD_DIM // LANES
    for i, (lw, lv) in enumerate(zip(layers, vecs)):
        zb, zf = _in_proj(h, lv["g_attn"], lw["w_in"], lw["scale_b"], lw["n_bf16"])
        q, k, v = _mla_proj(zf, lv["g_qa"], lv["g_kva"], lw["wq"], lw["wkv"], rope_c, rope_s, seq)
        o_mla = _mla_flash(q, k, v, batch, seq)
        o_na = _na_attention(zb, (0, nb, 2 * nb), lw["na_bias"], batch, seq)
        o_diff = _diff_flash(zb, (3 * nb, 4 * nb, 5 * nb), slopes, lv["lam_q1"], lv["lam_k1"], lv["lam_q2"],
                             lv["lam_k2"], lv["g_subln"], lw["lam_init"], batch, seq)
        h = _out_proj(o_mla, o_na, o_diff, lw["wo_mla"], lw["wo_na"], lw["wo_diff"], h)
        h = _ffn(h, lv["g_ffn"], lw["w_gu"], lv["conv_w"], lv["conv_b"], lw["w_down"], seq)
        h = _ple(h, p[i].reshape(t, -1), lv["g_ple"], lw["w_plg"], lw["w_ple"], g_final.reshape(1, d),
                 final_norm=i == len(layers) - 1)
    return h.reshape(batch, seq, d)


def kernel(x_prompt, x_sample, p_prompt, p_sample, g_attn, w_in, g_qa, w_qb, g_kva, w_kvb, rpb, lam_q1, lam_k1,
           lam_q2, lam_k2, g_subln, w_out, g_ffn, w_gu, conv_w, conv_b, w_down, g_ple, w_plg, w_ple, g_final):
    depth = w_in.shape[0]
    layers = [_prepare_layer(i, w_in[i], g_qa[i], w_qb[i], g_kva[i], w_kvb[i], rpb[i], w_out[i], w_gu[i],
                             w_down[i], w_plg[i], w_ple[i]) for i in range(depth)]
    row = lambda a: a.reshape(1, -1)
    vecs = [dict(g_attn=row(g_attn[i]), g_qa=row(g_qa[i]), g_kva=row(g_kva[i]), lam_q1=row(lam_q1[i]),
                 lam_k1=row(lam_k1[i]), lam_q2=row(lam_q2[i]), lam_k2=row(lam_k2[i]), g_subln=row(g_subln[i]),
                 g_ffn=row(g_ffn[i]), conv_w=conv_w[i], conv_b=row(conv_b[i]), g_ple=row(g_ple[i]))
            for i in range(depth)]
    return (_trunk(x_prompt, p_prompt, layers, vecs, g_final), _trunk(x_sample, p_sample, layers, vecs, g_final))
```
